```python
import math
import jax
import jax.numpy as jnp
from jax import lax
import numpy as np

D_MODEL = 4096
BATCH = 4
SEQ = 2048
DEPTH = 2
DEC_BATCH = 8
DEC_SEQ = 8
PAST_LEN = 16384
PAGE_SIZE = 128

N_A_LAYERS = DEPTH // 2
N_B_LAYERS = DEPTH - N_A_LAYERS

GLA_HEADS = 4
GLA_DK_TOT = D_MODEL // 2
GLA_DV_TOT = D_MODEL
GLA_DK = GLA_DK_TOT // GLA_HEADS
GLA_DV = GLA_DV_TOT // GLA_HEADS
GLA_GATE_RANK = 16
GLA_GATE_TAU = 16.0
GLA_CHUNK = 64
GLA_IN_COLS = 2 * GLA_DK_TOT + 2 * GLA_DV_TOT + GLA_GATE_RANK

HEAD_DIM = 128
N_HEADS = D_MODEL // HEAD_DIM
N_KV_HEADS = 8
GROUP = N_HEADS // N_KV_HEADS
MOBA_BLOCK = 256
MOBA_TOPK = 3
MOBA_Q_CHUNK = 32
ROPE_THETA = 10000.0

PEER_HEADS = 8
PEER_N_KEYS = 128
PEER_N_EXPERTS = PEER_N_KEYS * PEER_N_KEYS
PEER_TOPK = 16
PEER_KEY_DIM = 256
PEER_HALF = PEER_KEY_DIM // 2
PEER_TOKEN_CHUNK = 64

RMS_EPS = 1e-6
NEG_INF = -1e30

kernel_name = "yoco_gla_moba_peer_step"


def _rmsnorm(x, g):
    xf = x.astype(jnp.float32)
    y = xf * lax.rsqrt(jnp.mean(xf * xf, axis=-1, keepdims=True) + RMS_EPS)
    return (y * g.astype(jnp.float32)).astype(x.dtype)


def _rope(x, pos):
    half = x.shape[-1] // 2
    inv = ROPE_THETA ** (-jnp.arange(half, dtype=jnp.float32) / half)
    ang = pos.astype(jnp.float32)[:, None] * inv[None, :]
    cos = jnp.cos(ang)[None, :, None, :]
    sin = jnp.sin(ang)[None, :, None, :]
    xf = x.astype(jnp.float32)
    x1, x2 = xf[..., :half], xf[..., half:]
    return jnp.concatenate([x1 * cos - x2 * sin, x2 * cos + x1 * sin], axis=-1).astype(x.dtype)


def _gla_chunked(q, k, v, log_a, s0):
    B, L, H, _ = q.shape
    dv = v.shape[-1]
    c = math.gcd(L, GLA_CHUNK)
    n = L // c

    def to_chunks(t):
        return t.reshape(B, n, c, H, t.shape[-1]).transpose(1, 0, 3, 2, 4)

    tril = jnp.tril(jnp.ones((c, c), dtype=bool))

    def step(s, inp):
        qc, kc, vc, gc = inp
        b = jnp.cumsum(gc, axis=2)
        qe = qc * jnp.exp(b)
        ke = kc * jnp.exp(-b)
        att = jnp.where(tril, jnp.einsum('bhcd,bhsd->bhcs', qe, ke), 0.0)
        o = jnp.einsum('bhcd,bhde->bhce', qe, s) + jnp.einsum('bhcs,bhse->bhce', att, vc)
        b_last = b[:, :, -1:, :]
        s_new = jnp.exp(b_last[:, :, 0, :, None]) * s + jnp.einsum('bhsd,bhse->bhde', kc * jnp.exp(b_last - b), vc)
        return s_new, o

    s_fin, o = lax.scan(step, s0, (to_chunks(q), to_chunks(k), to_chunks(v), to_chunks(log_a)))
    o = o.transpose(1, 0, 3, 2, 4).reshape(B, L, H, dv)
    return o, s_fin


def _gla_mixer(hn, s0, w_in, w_gate2, b_gate, g_norm, w_out):
    B, L, _ = hn.shape
    f32 = jnp.float32
    proj = (hn @ w_in).astype(f32)
    q, k, v, r, z = jnp.split(proj, [GLA_DK_TOT, 2 * GLA_DK_TOT, 2 * GLA_DK_TOT + GLA_DV_TOT, 2 * GLA_DK_TOT + 2 * GLA_DV_TOT], axis=-1)
    log_a = jax.nn.log_sigmoid(z @ w_gate2.astype(f32) + b_gate.astype(f32)) / GLA_GATE_TAU
    qh = q.reshape(B, L, GLA_HEADS, GLA_DK) * (GLA_DK ** -0.5)
    kh = k.reshape(B, L, GLA_HEADS, GLA_DK)
    vh = v.reshape(B, L, GLA_HEADS, GLA_DV)
    ah = log_a.reshape(B, L, GLA_HEADS, GLA_DK)
    o, s_fin = _gla_chunked(qh, kh, vh, ah, s0.astype(f32))
    o = o * lax.rsqrt(jnp.mean(o * o, axis=-1, keepdims=True) + RMS_EPS) * g_norm.astype(f32)
    o = o.reshape(B, L, GLA_DV_TOT) * jax.nn.silu(r)
    return o.astype(hn.dtype) @ w_out, s_fin.astype(s0.dtype)


def _shared_kv(h, pos, norm_kv, w_kv, k_norm):
    B, L, _ = h.shape
    kv = _rmsnorm(h, norm_kv) @ w_kv
    k, v = jnp.split(kv, 2, axis=-1)
    k = _rope(_rmsnorm(k.reshape(B, L, N_KV_HEADS, HEAD_DIM), k_norm), pos)
    v = v.reshape(B, L, N_KV_HEADS, HEAD_DIM)
    return k, v


def _queries(hn, pos, w_q, q_norm):
    B, L, _ = hn.shape
    q = (hn @ w_q).reshape(B, L, N_HEADS, HEAD_DIM)
    return _rope(_rmsnorm(q, q_norm), pos)


def _dense_stats(q, k, v, mask):
    N, Tq, H, hd = q.shape
    qg = q.reshape(N, Tq, N_KV_HEADS, GROUP, hd)
    s = jnp.einsum('nqkgd,nskd->nqkgs', qg, k) * (hd ** -0.5)
    s = jnp.where(mask[:, None, None, :], s, NEG_INF)
    m = jnp.max(s, axis=-1)
    p = jnp.exp(s - m[..., None])
    l = jnp.sum(p, axis=-1)
    acc = jnp.einsum('nqkgs,nskd->nqkgd', p, v)
    return m.reshape(N, Tq, H), l.reshape(N, Tq, H), acc.reshape(N, Tq, H, hd)


def _sparse_stats(q, k_sel, v_sel, valid):
    hd = q.shape[-1]
    s = jnp.einsum('...hd,...hjsd->...hjs', q, k_sel.astype(jnp.float32)) * (hd ** -0.5)
    s = jnp.where(valid[..., None], s, NEG_INF)
    m = jnp.max(s, axis=(-2, -1))
    p = jnp.where(valid[..., None], jnp.exp(s - m[..., None, None]), 0.0)
    l = jnp.sum(p, axis=(-2, -1))
    acc = jnp.einsum('...hjs,...hjsd->...hd', p, v_sel.astype(jnp.float32))
    return m, l, acc


def _merge(a, b):
    ma, la, acca = a
    mb, lb, accb = b
    m = jnp.maximum(ma, mb)
    ea = jnp.exp(ma - m)
    eb = jnp.exp(mb - m)
    return (acca * ea[..., None] + accb * eb[..., None]) / (la * ea + lb * eb)[..., None]


def _moba_prompt(q, k, v):
    f32 = jnp.float32
    B, S, H, hd = q.shape
    q, k, v = q.astype(f32), k.astype(f32), v.astype(f32)
    nblk = -(-S // MOBA_BLOCK)
    pad = nblk * MOBA_BLOCK - S
    padw = ((0, 0), (0, pad), (0, 0), (0, 0))
    qp, kp, vp = jnp.pad(q, padw), jnp.pad(k, padw), jnp.pad(v, padw)
    kb = kp.reshape(B, nblk, MOBA_BLOCK, N_KV_HEADS, hd)
    vb = vp.reshape(B, nblk, MOBA_BLOCK, N_KV_HEADS, hd)
    tril = jnp.tril(jnp.ones((MOBA_BLOCK, MOBA_BLOCK), dtype=bool))
    own = _dense_stats(qp.reshape(B * nblk, MOBA_BLOCK, H, hd), kb.reshape(B * nblk, MOBA_BLOCK, N_KV_HEADS, hd), vb.reshape(B * nblk, MOBA_BLOCK, N_KV_HEADS, hd), tril)
    own = tuple(t.reshape(B, nblk * MOBA_BLOCK, *t.shape[2:])[:, :S] for t in own)
    means = jnp.mean(kb, axis=2)
    gate = jnp.einsum('bskgd,bnkd->bskgn', q.reshape(B, S, N_KV_HEADS, GROUP, hd), means).reshape(B, S, H, nblk)
    qblk = jnp.arange(S) // MOBA_BLOCK
    cand = jnp.arange(nblk)[None, :] < qblk[:, None]
    gate = jnp.where(cand[None, :, None, :], gate, NEG_INF)
    kk = min(MOBA_TOPK, nblk)
    _, idx = lax.top_k(gate, kk)
    valid = idx < qblk[None, :, None, None]
    kbt = kb.transpose(0, 3, 1, 2, 4)
    vbt = vb.transpose(0, 3, 1, 2, 4)
    kvh = (jnp.arange(H) // GROUP)[None, :, None]
    qc_len = math.gcd(S, MOBA_Q_CHUNK)
    nq = S // qc_len

    def step(args):
        b, qc, ic, vc = args
        k_sel = kbt[b][kvh, ic]
        v_sel = vbt[b][kvh, ic]
        return _sparse_stats(qc, k_sel, v_sel, vc)

    sp = lax.map(step, (jnp.repeat(jnp.arange(B), nq), q.reshape(B * nq, qc_len, H, hd), idx.reshape(B * nq, qc_len, H, kk), valid.reshape(B * nq, qc_len, H, kk)))
    sp = tuple(t.reshape(B, S, *t.shape[2:]) for t in sp)
    return _merge(sp, own)


def _moba_sample(q, k_new, v_new, cache_k, cache_v, page_table):
    f32 = jnp.float32
    DB, T, H, hd = q.shape
    q = q.astype(f32)
    ppb = MOBA_BLOCK // PAGE_SIZE
    n_full = PAST_LEN // MOBA_BLOCK
    tail_len = PAST_LEN - n_full * MOBA_BLOCK
    k_tail = cache_k[page_table[:, n_full * ppb:]].reshape(DB, tail_len, N_KV_HEADS, hd)
    v_tail = cache_v[page_table[:, n_full * ppb:]].reshape(DB, tail_len, N_KV_HEADS, hd)
    k_own = jnp.concatenate([k_tail.astype(f32), k_new.astype(f32)], axis=1)
    v_own = jnp.concatenate([v_tail.astype(f32), v_new.astype(f32)], axis=1)
    qpos = PAST_LEN + jnp.arange(T)
    kpos = n_full * MOBA_BLOCK + jnp.arange(tail_len + T)
    mask = (kpos[None, :] <= qpos[:, None]) & (kpos[None, :] // MOBA_BLOCK == qpos[:, None] // MOBA_BLOCK)
    own = _dense_stats(q, k_own, v_own, mask)
    if n_full == 0:
        return own[2] / own[1][..., None]
    k_past = cache_k[page_table[:, :n_full * ppb]].astype(f32).reshape(DB, n_full, MOBA_BLOCK, N_KV_HEADS, hd)
    means = jnp.mean(k_past, axis=2)
    gate = jnp.einsum('btkgd,bnkd->btkgn', q.reshape(DB, T, N_KV_HEADS, GROUP, hd), means).reshape(DB, T, H, n_full)
    qblk = qpos // MOBA_BLOCK
    cand = jnp.arange(n_full)[None, :] < qblk[:, None]
    gate = jnp.where(cand[None, :, None, :], gate, NEG_INF)
    kk = min(MOBA_TOPK, n_full)
    _, idx = lax.top_k(gate, kk)
    valid = idx < qblk[None, :, None, None]
    logical = idx[..., None] * ppb + jnp.arange(ppb)
    phys = page_table[jnp.arange(DB)[:, None, None, None, None], logical]
    kvh = (jnp.arange(H) // GROUP)[None, None, :, None, None]
    k_sel = cache_k[phys, :, kvh].reshape(DB, T, H, kk, MOBA_BLOCK, hd)
    v_sel = cache_v[phys, :, kvh].reshape(DB, T, H, kk, MOBA_BLOCK, hd)
    sp = _sparse_stats(q, k_sel, v_sel, valid)
    return _merge(sp, own)


def _peer(xn, w_query, sub_keys, u, v):
    f32 = jnp.float32
    B, L, D = xn.shape
    n = B * L
    xf = xn.reshape(n, D)
    q = (xf @ w_query).astype(f32).reshape(n, PEER_HEADS, 2, PEER_HALF)
    s = jnp.einsum('nhpd,hpkd->nhpk', q, sub_keys.astype(f32))
    s1, i1 = lax.top_k(s[:, :, 0], PEER_TOPK)
    s2, i2 = lax.top_k(s[:, :, 1], PEER_TOPK)
    cand = (s1[..., :, None] + s2[..., None, :]).reshape(n, PEER_HEADS, PEER_TOPK * PEER_TOPK)
    cidx = (i1[..., :, None] * PEER_N_KEYS + i2[..., None, :]).reshape(n, PEER_HEADS, PEER_TOPK * PEER_TOPK)
    sc, sel = lax.top_k(cand, PEER_TOPK)
    eidx = jnp.take_along_axis(cidx, sel, axis=-1)
    g = jax.nn.softmax(sc, axis=-1)
    c = math.gcd(n, PEER_TOKEN_CHUNK)

    def step(args):
        xc, ec, gc = args
        act = jax.nn.gelu(jnp.einsum('chkd,cd->chk', u[ec].astype(f32), xc.astype(f32)), approximate=False)
        return jnp.einsum('chk,chkd->cd', gc * act, v[ec].astype(f32))

    y = lax.map(step, (xf.reshape(n // c, c, D), eidx.reshape(n // c, c, PEER_HEADS, PEER_TOPK), g.reshape(n // c, c, PEER_HEADS, PEER_TOPK)))
    return y.reshape(B, L, D).astype(xn.dtype)


def _trunk(x, pos, gla_state0, attend, norm_mix, norm_ffn, gla_w_in, gla_w_gate2, gla_b_gate, gla_norm, gla_w_out, norm_kv, w_kv, k_norm, w_q, q_norm, w_o, peer_w_query, peer_sub_keys, peer_u, peer_v):
    B, L, _ = x.shape
    h = x
    gla_states = []
    k_sh = None
    v_sh = None
    for layer in range(DEPTH):
        hn = _rmsnorm(h, norm_mix[layer])
        if layer < N_A_LAYERS:
            o, s_fin = _gla_mixer(hn, gla_state0[layer], gla_w_in[layer], gla_w_gate2[layer], gla_b_gate[layer], gla_norm[layer], gla_w_out[layer])
            gla_states.append(s_fin)
        else:
            j = layer - N_A_LAYERS
            q = _queries(hn, pos, w_q[j], q_norm[j])
            o = attend(q, k_sh, v_sh).reshape(B, L, N_HEADS * HEAD_DIM).astype(x.dtype) @ w_o[j]
        h = h + o
        h = h + _peer(_rmsnorm(h, norm_ffn[layer]), peer_w_query[layer], peer_sub_keys[layer], peer_u[layer], peer_v[layer])
        if layer == N_A_LAYERS - 1:
            k_sh, v_sh = _shared_kv(h, pos, norm_kv, w_kv, k_norm)
    return h, jnp.stack(gla_states), k_sh, v_sh


def setup_inputs(seed: int = 0) -> dict:
    key = jax.random.key(seed)
    ks = jax.random.split(key, 24)
    f32 = jnp.float32
    n_pages = PAST_LEN // PAGE_SIZE
    n_phys = (5 * DEC_BATCH * n_pages) // 4

    def nrm(k, shape, scale):
        return jax.random.normal(k, shape, f32) * scale

    page_table = jax.random.permutation(ks[5], n_phys)[:DEC_BATCH * n_pages].reshape(DEC_BATCH, n_pages).astype(jnp.int32)
    return {
        'x_prompt': nrm(ks[0], (BATCH, SEQ, D_MODEL), 1.0),
        'x_sample': nrm(ks[1], (DEC_BATCH, DEC_SEQ, D_MODEL), 1.0),
        'cache_k': nrm(ks[2], (n_phys, PAGE_SIZE, N_KV_HEADS, HEAD_DIM), 1.0),
        'cache_v': nrm(ks[3], (n_phys, PAGE_SIZE, N_KV_HEADS, HEAD_DIM), 1.0),
        'state_gla': nrm(ks[4], (N_A_LAYERS, DEC_BATCH, GLA_HEADS, GLA_DK, GLA_DV), 1.0),
        'page_table': page_table,
        'norm_mix': 1.0 + nrm(ks[6], (DEPTH, D_MODEL), 0.02),
        'norm_ffn': 1.0 + nrm(ks[7], (DEPTH, D_MODEL), 0.02),
        'gla_w_in': nrm(ks[8], (N_A_LAYERS, D_MODEL, GLA_IN_COLS), D_MODEL ** -0.5),
        'gla_w_gate2': nrm(ks[9], (N_A_LAYERS, GLA_GATE_RANK, GLA_DK_TOT), GLA_GATE_RANK ** -0.5),
        'gla_b_gate': nrm(ks[10], (N_A_LAYERS, GLA_DK_TOT), 0.02),
        'gla_norm': 1.0 + nrm(ks[11], (N_A_LAYERS, GLA_HEADS, GLA_DV), 0.02),
        'gla_w_out': nrm(ks[12], (N_A_LAYERS, GLA_DV_TOT, D_MODEL), GLA_DV_TOT ** -0.5),
        'norm_kv': 1.0 + nrm(ks[13], (D_MODEL,), 0.02),
        'w_kv': nrm(ks[14], (D_MODEL, 2 * N_KV_HEADS * HEAD_DIM), D_MODEL ** -0.5),
        'k_norm': 1.0 + nrm(ks[15], (HEAD_DIM,), 0.02),
        'w_q': nrm(ks[16], (N_B_LAYERS, D_MODEL, N_HEADS * HEAD_DIM), D_MODEL ** -0.5),
        'q_norm': 1.0 + nrm(ks[17], (N_B_LAYERS, HEAD_DIM), 0.02),
        'w_o': nrm(ks[18], (N_B_LAYERS, N_HEADS * HEAD_DIM, D_MODEL), (N_HEADS * HEAD_DIM) ** -0.5),
        'peer_w_query': nrm(ks[19], (DEPTH, D_MODEL, PEER_HEADS * PEER_KEY_DIM), D_MODEL ** -0.5),
        'peer_sub_keys': nrm(ks[20], (DEPTH, PEER_HEADS, 2, PEER_N_KEYS, PEER_HALF), PEER_HALF ** -0.5),
        'peer_u': nrm(ks[21], (DEPTH, PEER_N_EXPERTS, D_MODEL), D_MODEL ** -0.5),
        'peer_v': nrm(ks[22], (DEPTH, PEER_N_EXPERTS, D_MODEL), (PEER_HEADS * PEER_TOPK) ** -0.5),
    }


def reference(x_prompt, x_sample, cache_k, cache_v, state_gla, page_table, norm_mix, norm_ffn, gla_w_in, gla_w_gate2, gla_b_gate, gla_norm, gla_w_out, norm_kv, w_kv, k_norm, w_q, q_norm, w_o, peer_w_query, peer_sub_keys, peer_u, peer_v):
    weights = (norm_mix, norm_ffn, gla_w_in, gla_w_gate2, gla_b_gate, gla_norm, gla_w_out, norm_kv, w_kv, k_norm, w_q, q_norm, w_o, peer_w_query, peer_sub_keys, peer_u, peer_v)
    pos_p = jnp.arange(x_prompt.shape[1], dtype=jnp.int32)
    s0_p = jnp.zeros((N_A_LAYERS, x_prompt.shape[0], GLA_HEADS, GLA_DK, GLA_DV), dtype=state_gla.dtype)
    y_prompt, gla_state_prompt, k_prompt, v_prompt = _trunk(x_prompt, pos_p, s0_p, _moba_prompt, *weights)
    pos_s = PAST_LEN + jnp.arange(x_sample.shape[1], dtype=jnp.int32)

    def attend_sample(q, k, v):
        return _moba_sample(q, k, v, cache_k, cache_v, page_table)

    y_sample, gla_state_sample, k_sample, v_sample = _trunk(x_sample, pos_s, state_gla, attend_sample, *weights)
    return (y_prompt, y_sample, gla_state_prompt, gla_state_sample, k_prompt, v_prompt, k_sample, v_sample)
```

```python
import functools
import math
from typing import NamedTuple

import numpy as np
import jax
import jax.numpy as jnp
from jax import lax
from jax.experimental import pallas as pl
from jax.experimental.pallas import tpu as pltpu

F32 = jnp.float32
BF16 = jnp.bfloat16

LANES = 128
RMS_EPS = 1e-6
NEG_INF = -1e30
ROPE_THETA = 10000.0
VMEM_LIMIT_BYTES = 56 * 1024 * 1024


class Cfg(NamedTuple):
    d_model: int = 4096
    gla_heads: int = 4
    gla_gate_rank: int = 16
    gla_gate_tau: float = 16.0
    gla_chunk: int = 64
    head_dim: int = 128
    n_kv_heads: int = 8
    moba_block: int = 256
    moba_topk: int = 3
    page_size: int = 128
    peer_heads: int = 8
    peer_n_keys: int = 128
    peer_topk: int = 16
    peer_key_dim: int = 256

    @property
    def gla_dk_tot(self):
        return self.d_model // 2

    @property
    def gla_dk(self):
        return self.gla_dk_tot // self.gla_heads

    @property
    def gla_dv(self):
        return self.d_model // self.gla_heads

    @property
    def n_heads(self):
        return self.d_model // self.head_dim

    @property
    def group(self):
        return self.n_heads // self.n_kv_heads

    @property
    def kv_dim(self):
        return self.n_kv_heads * self.head_dim


CFG = Cfg()


def _params(*sem):
    return pltpu.CompilerParams(dimension_semantics=sem, vmem_limit_bytes=VMEM_LIMIT_BYTES)


def _pick(n, prefs):
    for p in prefs:
        if n % p == 0:
            return p
    return n


def _rmsnorm_body(x_ref, g_ref, *o_refs):
    x = x_ref[...]
    y = x * lax.rsqrt(jnp.mean(x * x, axis=-1, keepdims=True) + RMS_EPS)
    for i, o_ref in enumerate(o_refs):
        o_ref[...] = (y * g_ref[i:i + 1, :]).astype(o_ref.dtype)


def rmsnorm_cast(x, gains):
    m, d = x.shape
    g = gains.shape[0]
    tm = _pick(m, (256, 128, 64))
    return pl.pallas_call(
        _rmsnorm_body,
        grid=(m // tm,),
        in_specs=[pl.BlockSpec((tm, d), lambda i: (i, 0)), pl.BlockSpec((g, d), lambda i: (0, 0))],
        out_specs=[pl.BlockSpec((tm, d), lambda i: (i, 0))] * g,
        out_shape=[jax.ShapeDtypeStruct((m, d), BF16)] * g,
        compiler_params=_params("parallel"),
        name="rmsnorm_cast",
    )(x, gains)


def _mm_body(x_ref, w_ref, o_ref):
    o_ref[...] = jnp.dot(x_ref[...], w_ref[...], preferred_element_type=F32).astype(o_ref.dtype)


def _mm_res_body(x_ref, w_ref, r_ref, o_ref):
    o_ref[...] = (r_ref[...] + jnp.dot(x_ref[...], w_ref[...], preferred_element_type=F32)).astype(o_ref.dtype)


def matmul(x, w, residual=None, out_dtype=F32, name="matmul"):
    m, k = x.shape
    n = w.shape[1]
    tm = _pick(m, (512, 256, 128, 64))
    tn = _pick(n, (512, 256, 128))
    in_specs = [pl.BlockSpec((tm, k), lambda j, i: (i, 0)), pl.BlockSpec((k, tn), lambda j, i: (0, j))]
    args = [x, w]
    body = _mm_body
    if residual is not None:
        in_specs.append(pl.BlockSpec((tm, tn), lambda j, i: (i, j)))
        args.append(residual)
        body = _mm_res_body
    return pl.pallas_call(
        body,
        grid=(n // tn, m // tm),
        in_specs=in_specs,
        out_specs=pl.BlockSpec((tm, tn), lambda j, i: (i, j)),
        out_shape=jax.ShapeDtypeStruct((m, n), out_dtype),
        compiler_params=_params("parallel", "parallel"),
        name=name,
    )(*args)


def _cumsum_rows(g):
    c = g.shape[0]
    row = lax.broadcasted_iota(jnp.int32, g.shape, 0)
    b = g
    s = 1
    while s < c:
        b = b + jnp.where(row >= s, pltpu.roll(b, s, axis=0), 0.0)
        s *= 2
    return b


def _log_sigmoid(x):
    return -(jnp.maximum(-x, 0.0) + jnp.log1p(jnp.exp(-jnp.abs(x))))


def _gla_body(q_ref, k_ref, v_ref, r_ref, z_ref, wg_ref, bg_ref, gn_ref, s0_ref, og_ref, sf_ref, st_ref,
              *, chunk, nsub, q_scale, inv_tau):
    t = pl.program_id(2)

    @pl.when(t == 0)
    def _():
        st_ref[...] = s0_ref[0, 0].T

    za = jnp.dot(z_ref[0].astype(BF16), wg_ref[...], preferred_element_type=F32) + bg_ref[...]
    log_a = _log_sigmoid(za) * inv_tau
    tril = (lax.broadcasted_iota(jnp.int32, (chunk, chunk), 0) >= lax.broadcasted_iota(jnp.int32, (chunk, chunk), 1))
    nt = (((1,), (1,)), ((), ()))
    tn = (((0,), (0,)), ((), ()))
    for i in range(nsub):
        sl = slice(i * chunk, (i + 1) * chunk)
        b = _cumsum_rows(log_a[sl])
        b_last = b[chunk - 1:chunk, :]
        q = q_ref[0, sl, :] * q_scale
        k = k_ref[0, sl, :]
        v = v_ref[0, sl, :].astype(BF16)
        qe = (q * jnp.exp(b)).astype(BF16)
        ke = (k * jnp.exp(-b)).astype(BF16)
        kd = (k * jnp.exp(b_last - b)).astype(BF16)
        att = lax.dot_general(qe, ke, nt, preferred_element_type=F32)
        att = jnp.where(tril, att, 0.0).astype(BF16)
        st = st_ref[...]
        o = lax.dot_general(qe, st.astype(BF16), nt, preferred_element_type=F32)
        o = o + jnp.dot(att, v, preferred_element_type=F32)
        st_ref[...] = st * jnp.exp(b_last) + lax.dot_general(v, kd, tn, preferred_element_type=F32)
        on = o * lax.rsqrt(jnp.mean(o * o, axis=-1, keepdims=True) + RMS_EPS) * gn_ref[...]
        r = r_ref[0, sl, :]
        og_ref[0, sl, :] = (on * (r * jax.nn.sigmoid(r))).astype(og_ref.dtype)

    @pl.when(t == pl.num_programs(2) - 1)
    def _():
        sf_ref[0, 0] = st_ref[...].T


def gla(proj, z, wg, bg, gn, s0, cfg):
    bsz, seq, _ = proj.shape
    h, dk, dv = cfg.gla_heads, cfg.gla_dk, cfg.gla_dv
    chunk = math.gcd(seq, cfg.gla_chunk)
    rows = _pick(seq, (4 * chunk, 2 * chunk, chunk))
    nsub = rows // chunk
    kq, kk, kv, kr = 0, h, (2 * h * dk) // dv, (2 * h * dk) // dv + h
    body = functools.partial(_gla_body, chunk=chunk, nsub=nsub, q_scale=dk ** -0.5, inv_tau=1.0 / cfg.gla_gate_tau)
    return pl.pallas_call(
        body,
        grid=(bsz, h, seq // rows),
        in_specs=[
            pl.BlockSpec((1, rows, dk), lambda b, hh, t: (b, t, kq + hh)),
            pl.BlockSpec((1, rows, dk), lambda b, hh, t: (b, t, kk + hh)),
            pl.BlockSpec((1, rows, dv), lambda b, hh, t: (b, t, kv + hh)),
            pl.BlockSpec((1, rows, dv), lambda b, hh, t: (b, t, kr + hh)),
            pl.BlockSpec((1, rows, LANES), lambda b, hh, t: (b, t, 0)),
            pl.BlockSpec((LANES, dk), lambda b, hh, t: (0, hh)),
            pl.BlockSpec((1, dk), lambda b, hh, t: (0, hh)),
            pl.BlockSpec((1, dv), lambda b, hh, t: (0, hh)),
            pl.BlockSpec((1, 1, dk, dv), lambda b, hh, t: (b, hh, 0, 0)),
        ],
        out_specs=[
            pl.BlockSpec((1, rows, dv), lambda b, hh, t: (b, t, hh)),
            pl.BlockSpec((1, 1, dk, dv), lambda b, hh, t: (b, hh, 0, 0)),
        ],
        out_shape=[
            jax.ShapeDtypeStruct((bsz, seq, h * dv), BF16),
            jax.ShapeDtypeStruct((bsz, h, dk, dv), F32),
        ],
        scratch_shapes=[pltpu.VMEM((dv, dk), F32)],
        compiler_params=_params("parallel", "parallel", "arbitrary"),
        name="gla",
    )(proj, proj, proj, proj, z, wg, bg, gn, s0)


def _headrope_body(x_ref, g_ref, cos_ref, sin_ref, *o_refs, nh, hd):
    for hh in range(nh):
        x = x_ref[:, hh * hd:(hh + 1) * hd]
        y = x * lax.rsqrt(jnp.mean(x * x, axis=-1, keepdims=True) + RMS_EPS) * g_ref[...]
        out = y * cos_ref[...] + pltpu.roll(y, hd // 2, axis=1) * sin_ref[...]
        for o_ref in o_refs:
            o_ref[:, hh * hd:(hh + 1) * hd] = out.astype(o_ref.dtype)


def head_norm_rope(x, gain, cos, sin, seq, out_dtypes, hd):
    n, width = x.shape
    nh = width // hd
    tm = _pick(seq, (256, 128, 64, 32, 16, 8))
    per = seq // tm
    return pl.pallas_call(
        functools.partial(_headrope_body, nh=nh, hd=hd),
        grid=(n // tm,),
        in_specs=[
            pl.BlockSpec((tm, width), lambda i: (i, 0)),
            pl.BlockSpec((1, hd), lambda i: (0, 0)),
            pl.BlockSpec((tm, hd), lambda i: (i % per, 0)),
            pl.BlockSpec((tm, hd), lambda i: (i % per, 0)),
        ],
        out_specs=[pl.BlockSpec((tm, width), lambda i: (i, 0)) for _ in out_dtypes],
        out_shape=[jax.ShapeDtypeStruct((n, width), dt) for dt in out_dtypes],
        compiler_params=_params("parallel"),
        name="head_norm_rope",
    )(x, gain, cos, sin)


def _rope_tables(pos0, seq, hd):
    half = hd // 2
    inv = ROPE_THETA ** (-np.arange(half, dtype=np.float64) / half)
    ang = (pos0 + np.arange(seq, dtype=np.float64))[:, None] * inv[None, :]
    cos, sin = np.cos(ang), np.sin(ang)
    return (jnp.asarray(np.concatenate([cos, cos], axis=1), F32),
            jnp.asarray(np.concatenate([-sin, sin], axis=1), F32))


def _top_values(s, k):
    vals = []
    cur = s
    for i in range(k):
        m = jnp.max(cur, axis=0, keepdims=True)
        vals.append(m)
        if i + 1 < k:
            cur = jnp.where(cur == m, NEG_INF, cur)
    return vals


def _peer_route_body(q_ref, keys_ref, s1_ref, s2_ref, e2_ref, c_ref, tau_ref, *, heads, nkeys, half, topk):
    nt = (((1,), (1,)), ((), ()))
    tm = q_ref.shape[0]
    sub = 8
    row = lax.broadcasted_iota(jnp.int32, (sub, tm), 0)
    for hh in range(heads):
        st = []
        for p in range(2):
            g = 2 * hh + p
            qg = q_ref[:, g * half:(g + 1) * half].astype(BF16)
            st.append(lax.dot_general(keys_ref[g], qg, nt, preferred_element_type=F32))
        s1, s2 = st
        v1 = _top_values(s1, topk)
        v2 = _top_values(s2, topk)
        v2g = []
        for g0 in range(0, topk, sub):
            grp = jnp.full((sub, tm), NEG_INF, F32)
            for b in range(g0, min(g0 + sub, topk)):
                grp = jnp.where(row == b - g0, v2[b], grp)
            v2g.append(grp)
        cands = []
        for a in range(topk):
            bmax = topk // (a + 1)
            for gi, grp in enumerate(v2g):
                if gi * sub < bmax:
                    cands.append(jnp.where(row < bmax - gi * sub, v1[a] + grp, NEG_INF))
        cur = cands
        tau = None
        for i in range(topk):
            tau = functools.reduce(jnp.maximum, [jnp.max(c, axis=0, keepdims=True) for c in cur])
            if i + 1 < topk:
                cur = [jnp.where(c == tau, NEG_INF, c) for c in cur]
        m1, m2 = v1[0], v2[0]
        mx = m1 + m2
        z = functools.reduce(
            jnp.add, [jnp.sum(jnp.where(c >= tau, jnp.exp(c - mx), 0.0), axis=0, keepdims=True) for c in cands])
        s1_ref[hh] = s1
        s2_ref[hh] = s2
        e2_ref[hh] = jnp.exp(s2 - m2)
        c_ref[hh] = jnp.exp(s1 - m1) / z
        tau_ref[hh:hh + 1, :] = tau


def peer_route(q, keys, cfg):
    n = q.shape[0]
    heads, nkeys, half = cfg.peer_heads, cfg.peer_n_keys, cfg.peer_key_dim // 2
    tm = _pick(n, (256, 128))
    tab = jax.ShapeDtypeStruct((heads, nkeys, n), F32)
    tab_spec = pl.BlockSpec((heads, nkeys, tm), lambda i: (0, 0, i))
    return pl.pallas_call(
        functools.partial(_peer_route_body, heads=heads, nkeys=nkeys, half=half, topk=cfg.peer_topk),
        grid=(n // tm,),
        in_specs=[pl.BlockSpec((tm, q.shape[1]), lambda i: (i, 0)),
                  pl.BlockSpec(keys.shape, lambda i: (0, 0, 0))],
        out_specs=[tab_spec, tab_spec, tab_spec, tab_spec, pl.BlockSpec((heads, tm), lambda i: (0, i))],
        out_shape=[tab, tab, tab, tab, jax.ShapeDtypeStruct((heads, n), F32)],
        compiler_params=_params("parallel"),
        name="peer_route",
    )(q, keys)


def _gelu(x):
    return 0.5 * x * (1.0 + lax.erf(x * (1.0 / math.sqrt(2.0))))


def _peer_dense_body(xt_ref, u_ref, vt_ref, s1_ref, s2_ref, e2_ref, c_ref, tau_ref, o_ref, *, heads, nkeys, te):
    j = pl.program_id(1)

    @pl.when(j == 0)
    def _():
        o_ref[...] = jnp.zeros_like(o_ref)

    act = _gelu(jnp.dot(u_ref[...], xt_ref[...], preferred_element_type=F32))
    gates = []
    for r in range(te // nkeys):
        i1 = j * (te // nkeys) + r
        w = None
        for hh in range(heads):
            s1row = s1_ref[hh, pl.ds(i1, 1), :]
            crow = c_ref[hh, pl.ds(i1, 1), :]
            hit = (s2_ref[hh] + s1row) >= tau_ref[hh:hh + 1, :]
            term = jnp.where(hit, e2_ref[hh], 0.0) * crow
            w = term if w is None else w + term
        gates.append(w)
    gate = gates[0] if len(gates) == 1 else jnp.concatenate(gates, axis=0)
    ht = (gate * act).astype(BF16)
    o_ref[...] += jnp.dot(vt_ref[...], ht, preferred_element_type=F32)


def peer_dense(xt, u, vt, s1, s2, e2, c, tau, cfg):
    d, n = xt.shape
    e = u.shape[0]
    heads, nkeys = cfg.peer_heads, cfg.peer_n_keys
    tm = _pick(n, (512, 256, 128))
    te = 2 * nkeys
    tab_spec = pl.BlockSpec((heads, nkeys, tm), lambda i, j: (0, 0, i))
    return pl.pallas_call(
        functools.partial(_peer_dense_body, heads=heads, nkeys=nkeys, te=te),
        grid=(n // tm, e // te),
        in_specs=[
            pl.BlockSpec((d, tm), lambda i, j: (0, i)),
            pl.BlockSpec((te, d), lambda i, j: (j, 0)),
            pl.BlockSpec((d, te), lambda i, j: (0, j)),
            tab_spec, tab_spec, tab_spec, tab_spec,
            pl.BlockSpec((heads, tm), lambda i, j: (0, i)),
        ],
        out_specs=pl.BlockSpec((d, tm), lambda i, j: (0, i)),
        out_shape=jax.ShapeDtypeStruct((d, n), F32),
        compiler_params=_params("parallel", "arbitrary"),
        name="peer_dense",
    )(xt, u, vt, s1, s2, e2, c, tau)


def peer_block(h, g_ffn, w_query, keys, u, vt, cfg):
    n = h.shape[0]
    npad = -(-n // LANES) * LANES
    (xn,) = rmsnorm_cast(h, g_ffn[None, :])
    if npad != n:
        xn = jnp.pad(xn, ((0, npad - n), (0, 0)))
    q = matmul(xn, w_query, name="peer_query")
    s1, s2, e2, c, tau = peer_route(q, keys, cfg)
    yt = peer_dense(xn.T, u, vt, s1, s2, e2, c, tau, cfg)
    return h + yt.T[:n]


def _moba_prompt_body(q_ref, k_ref, v_ref, o_ref, means_ref, sel_ref, m_ref, l_ref, acc_ref,
                      *, blk, nblk, group, hd, topk, scale):
    qb = pl.program_id(2)
    nt = (((1,), (1,)), ((), ()))
    tn = (((0,), (0,)), ((), ()))
    rows = group * blk

    @pl.when(qb == 0)
    def _():
        for n in range(nblk):
            means_ref[n:n + 1, :] = jnp.mean(k_ref[0, n * blk:(n + 1) * blk, :], axis=0, keepdims=True)

    q4 = jnp.concatenate([q_ref[0, :, g * hd:(g + 1) * hd] for g in range(group)], axis=0)

    gate = lax.dot_general(means_ref[...].astype(BF16), q4, nt, preferred_element_type=F32)
    bidx = lax.broadcasted_iota(jnp.int32, gate.shape, 0)
    cand = bidx < qb
    gate = jnp.where(cand, gate, NEG_INF)
    rank = jnp.zeros(gate.shape, F32)
    for mm in range(nblk):
        gm = gate[mm:mm + 1, :]
        beats = (gm > gate) | ((gm == gate) & (mm < bidx))
        rank = rank + jnp.where(beats, 1.0, 0.0)
    sel_ref[...] = jnp.where((rank < topk) & cand, 1.0, 0.0)

    kpos = lax.broadcasted_iota(jnp.int32, (blk, rows), 0)
    qpos = lax.broadcasted_iota(jnp.int32, (blk, rows), 1) % blk
    k_own = k_ref[0, pl.ds(qb * blk, blk), :].astype(BF16)
    v_own = v_ref[0, pl.ds(qb * blk, blk), :].astype(BF16)
    s = lax.dot_general(k_own, q4, nt, preferred_element_type=F32) * scale
    s = jnp.where(kpos <= qpos, s, NEG_INF)
    m0 = jnp.max(s, axis=0, keepdims=True)
    p = jnp.exp(s - m0)
    m_ref[...] = m0
    l_ref[...] = jnp.sum(p, axis=0, keepdims=True)
    acc_ref[...] = lax.dot_general(v_own, p.astype(BF16), tn, preferred_element_type=F32)

    def past(n, carry):
        kb = k_ref[0, pl.ds(n * blk, blk), :].astype(BF16)
        vb = v_ref[0, pl.ds(n * blk, blk), :].astype(BF16)
        on = sel_ref[pl.ds(n, 1), :] > 0.5
        sb = lax.dot_general(kb, q4, nt, preferred_element_type=F32) * scale
        sb = jnp.where(on, sb, NEG_INF)
        m_old = m_ref[...]
        m_new = jnp.maximum(m_old, jnp.max(sb, axis=0, keepdims=True))
        pb = jnp.where(on, jnp.exp(sb - m_new), 0.0)
        alpha = jnp.exp(m_old - m_new)
        m_ref[...] = m_new
        l_ref[...] = alpha * l_ref[...] + jnp.sum(pb, axis=0, keepdims=True)
        acc_ref[...] = alpha * acc_ref[...] + lax.dot_general(vb, pb.astype(BF16), tn, preferred_element_type=F32)
        return carry

    lax.fori_loop(0, qb, past, 0)

    out = (acc_ref[...] / l_ref[...]).T
    for g in range(group):
        o_ref[0, :, g * hd:(g + 1) * hd] = out[g * blk:(g + 1) * blk, :].astype(o_ref.dtype)


def moba_prompt(q, k, v, cfg):
    bsz, seq, _ = q.shape
    blk, hd, group, kvh = cfg.moba_block, cfg.head_dim, cfg.group, cfg.n_kv_heads
    assert seq % blk == 0
    nblk = seq // blk
    rows = group * blk
    nsel = -(-nblk // 8) * 8
    body = functools.partial(_moba_prompt_body, blk=blk, nblk=nblk, group=group, hd=hd, topk=cfg.moba_topk,
                             scale=hd ** -0.5)
    return pl.pallas_call(
        body,
        grid=(bsz, kvh, nblk),
        in_specs=[
            pl.BlockSpec((1, blk, group * hd), lambda b, kh, i: (b, i, kh)),
            pl.BlockSpec((1, seq, hd), lambda b, kh, i: (b, 0, kh)),
            pl.BlockSpec((1, seq, hd), lambda b, kh, i: (b, 0, kh)),
        ],
        out_specs=pl.BlockSpec((1, blk, group * hd), lambda b, kh, i: (b, i, kh)),
        out_shape=jax.ShapeDtypeStruct(q.shape, BF16),
        scratch_shapes=[
            pltpu.VMEM((nblk, hd), F32),
            pltpu.VMEM((nblk, rows), F32),
            pltpu.VMEM((1, rows), F32),
            pltpu.VMEM((1, rows), F32),
            pltpu.VMEM((hd, rows), F32),
        ],
        compiler_params=_params("parallel", "parallel", "arbitrary"),
        name="moba_prompt",
    )(q, k, v)


def _block_means_body(pt_ref, k_ref, o_ref, *, inv_rows):
    p = pl.program_id(2)
    part = jnp.sum(k_ref[0], axis=0, keepdims=True) * inv_rows

    @pl.when(p == 0)
    def _():
        o_ref[0, 0] = part

    @pl.when(p != 0)
    def _():
        o_ref[0, 0] += part


def block_means(cache_k, page_table, n_full, ppb, cfg):
    db = page_table.shape[0]
    ps, width = cache_k.shape[1], cache_k.shape[2]
    return pl.pallas_call(
        functools.partial(_block_means_body, inv_rows=1.0 / (ps * ppb)),
        grid_spec=pltpu.PrefetchScalarGridSpec(
            num_scalar_prefetch=1,
            grid=(db, n_full, ppb),
            in_specs=[pl.BlockSpec((1, ps, width), lambda b, n, p, pt: (pt[b, n * ppb + p], 0, 0))],
            out_specs=pl.BlockSpec((1, 1, 1, width), lambda b, n, p, pt: (b, n, 0, 0)),
        ),
        out_shape=jax.ShapeDtypeStruct((db, n_full, 1, width), F32),
        compiler_params=_params("parallel", "parallel", "arbitrary"),
        name="block_means",
    )(page_table, cache_k)


def _moba_select_body(q_ref, means_ref, sel_ref, *, topk):
    nt = (((1,), (1,)), ((), ()))
    gate = lax.dot_general(q_ref[0], means_ref[0].astype(BF16), nt, preferred_element_type=F32)
    lane = lax.broadcasted_iota(jnp.int32, gate.shape, 1)
    nb = gate.shape[1]
    sel = jnp.zeros(gate.shape, F32)
    for _ in range(topk):
        m = jnp.max(gate, axis=1, keepdims=True)
        first = jnp.min(jnp.where(gate == m, lane, nb), axis=1, keepdims=True)
        pick = lane == first
        sel = jnp.where(pick, 1.0, sel)
        gate = jnp.where(pick, -3.0e38, gate)
    sel_ref[0] = sel


def moba_select(qbd, means, cfg):
    db, rows, width = qbd.shape
    n_full = means.shape[1]
    return pl.pallas_call(
        functools.partial(_moba_select_body, topk=min(cfg.moba_topk, n_full)),
        grid=(db,),
        in_specs=[pl.BlockSpec((1, rows, width), lambda b: (b, 0, 0)),
                  pl.BlockSpec((1, n_full, width), lambda b: (b, 0, 0))],
        out_specs=pl.BlockSpec((1, rows, n_full), lambda b: (b, 0, 0)),
        out_shape=jax.ShapeDtypeStruct((db, rows, n_full), F32),
        compiler_params=_params("parallel"),
        name="moba_select",
    )(qbd, means)


def _moba_sample_body(pt_ref, q_ref, k_ref, v_ref, sel_ref, kn_ref, vn_ref, o_ref, m_ref, l_ref, acc_ref,
                      *, scale, t_new, hd, kvh, rows_per_kv):
    pg = pl.program_id(1)
    nt = (((1,), (1,)), ((), ()))
    q = q_ref[0]

    @pl.when(pg == 0)
    def _():
        m_ref[...] = jnp.full(m_ref.shape, NEG_INF, F32)
        l_ref[...] = jnp.zeros(l_ref.shape, F32)
        acc_ref[...] = jnp.zeros(acc_ref.shape, F32)

    def absorb(kb, vb, on):
        s = lax.dot_general(q, kb, nt, preferred_element_type=F32) * scale
        s = jnp.where(on, s, NEG_INF)
        m_old = m_ref[...]
        m_new = jnp.maximum(m_old, jnp.max(s, axis=1, keepdims=True))
        p = jnp.where(on, jnp.exp(s - m_new), 0.0)
        alpha = jnp.exp(m_old - m_new)
        m_ref[...] = m_new
        l_ref[...] = alpha * l_ref[...] + jnp.sum(p, axis=1, keepdims=True)
        acc_ref[...] = alpha * acc_ref[...] + jnp.dot(p.astype(BF16), vb, preferred_element_type=F32)

    absorb(k_ref[0].astype(BF16), v_ref[0].astype(BF16), sel_ref[0, 0] > 0.5)

    @pl.when(pg == pl.num_programs(1) - 1)
    def _():
        shape = (q.shape[0], kn_ref.shape[1])
        tq = lax.broadcasted_iota(jnp.int32, shape, 0) % t_new
        tk = lax.broadcasted_iota(jnp.int32, shape, 1)
        absorb(kn_ref[0].astype(BF16), vn_ref[0].astype(BF16), tk <= tq)
        out = acc_ref[...] / l_ref[...]
        for kh in range(kvh):
            r0 = kh * rows_per_kv
            o_ref[0, r0:r0 + rows_per_kv, :] = out[r0:r0 + rows_per_kv, kh * hd:(kh + 1) * hd]


def moba_sample(qbd, cache_k, cache_v, page_table, sel, k_new, v_new, t_new, cfg):
    db, rows, width = qbd.shape
    n_pages = page_table.shape[1]
    ps = cache_k.shape[1]
    hd, kvh = cfg.head_dim, cfg.n_kv_heads
    body = functools.partial(_moba_sample_body, scale=hd ** -0.5, t_new=t_new, hd=hd, kvh=kvh,
                             rows_per_kv=rows // kvh)
    return pl.pallas_call(
        body,
        grid_spec=pltpu.PrefetchScalarGridSpec(
            num_scalar_prefetch=1,
            grid=(db, n_pages),
            in_specs=[
                pl.BlockSpec((1, rows, width), lambda b, p, pt: (b, 0, 0)),
                pl.BlockSpec((1, ps, width), lambda b, p, pt: (pt[b, p], 0, 0)),
                pl.BlockSpec((1, ps, width), lambda b, p, pt: (pt[b, p], 0, 0)),
                pl.BlockSpec((1, 1, rows, 1), lambda b, p, pt: (b, p, 0, 0)),
                pl.BlockSpec((1, k_new.shape[1], width), lambda b, p, pt: (b, 0, 0)),
                pl.BlockSpec((1, v_new.shape[1], width), lambda b, p, pt: (b, 0, 0)),
            ],
            out_specs=pl.BlockSpec((1, rows, hd), lambda b, p, pt: (b, 0, 0)),
            scratch_shapes=[
                pltpu.VMEM((rows, 1), F32),
                pltpu.VMEM((rows, 1), F32),
                pltpu.VMEM((rows, width), F32),
            ],
        ),
        out_shape=jax.ShapeDtypeStruct((db, rows, hd), F32),
        compiler_params=_params("parallel", "arbitrary"),
        name="moba_sample",
    )(page_table, qbd, cache_k, cache_v, sel, k_new, v_new)


class Weights(NamedTuple):
    norm_mix: jax.Array
    norm_ffn: jax.Array
    w_in: jax.Array
    w_z: jax.Array
    w_gate2: jax.Array
    b_gate: jax.Array
    gla_norm: jax.Array
    w_out: jax.Array
    norm_kv: jax.Array
    w_kv: jax.Array
    k_norm: jax.Array
    w_q: jax.Array
    q_norm: jax.Array
    w_o: jax.Array
    peer_w_query: tuple
    peer_keys: tuple
    peer_u: tuple
    peer_vt: tuple


def _prep_weights(norm_mix, norm_ffn, gla_w_in, gla_w_gate2, gla_b_gate, gla_norm, gla_w_out, norm_kv, w_kv,
                  k_norm, w_q, q_norm, w_o, peer_w_query, peer_sub_keys, peer_u, peer_v, cfg):
    main = 2 * cfg.gla_dk_tot + 2 * cfg.d_model
    rank = cfg.gla_gate_rank
    w_in = gla_w_in[0]
    depth = norm_mix.shape[0]
    return Weights(
        norm_mix=norm_mix,
        norm_ffn=norm_ffn,
        w_in=w_in[:, :main].astype(BF16),
        w_z=jnp.pad(w_in[:, main:], ((0, 0), (0, LANES - rank))).astype(BF16),
        w_gate2=jnp.pad(gla_w_gate2[0], ((0, LANES - rank), (0, 0))).astype(BF16),
        b_gate=gla_b_gate[0][None, :],
        gla_norm=gla_norm[0].reshape(1, -1),
        w_out=gla_w_out[0].astype(BF16),
        norm_kv=norm_kv,
        w_kv=w_kv.astype(BF16),
        k_norm=k_norm[None, :],
        w_q=w_q[0].astype(BF16),
        q_norm=q_norm[0][None, :],
        w_o=w_o[0].astype(BF16),
        peer_w_query=tuple(peer_w_query[l].astype(BF16) for l in range(depth)),
        peer_keys=tuple(peer_sub_keys[l].reshape(2 * cfg.peer_heads, cfg.peer_n_keys, -1).astype(BF16)
                        for l in range(depth)),
        peer_u=tuple(peer_u[l].astype(BF16) for l in range(depth)),
        peer_vt=tuple(peer_v[l].T.astype(BF16) for l in range(depth)),
    )


def _trunk(x, pos0, s0, attend, w, cfg):
    bsz, seq, d = x.shape
    n = bsz * seq
    hd = cfg.head_dim
    xf = x.reshape(n, d)

    (hn,) = rmsnorm_cast(xf, w.norm_mix[0:1])
    proj = matmul(hn, w.w_in, name="gla_in")
    z = matmul(hn, w.w_z, name="gla_gate_in")
    og, s_fin = gla(proj.reshape(bsz, seq, -1), z.reshape(bsz, seq, -1), w.w_gate2, w.b_gate, w.gla_norm, s0, cfg)
    h = matmul(og.reshape(n, -1), w.w_out, residual=xf, name="gla_out")
    h = peer_block(h, w.norm_ffn[0], w.peer_w_query[0], w.peer_keys[0], w.peer_u[0], w.peer_vt[0], cfg)

    kvn, hn1 = rmsnorm_cast(h, jnp.stack([w.norm_kv, w.norm_mix[1]]))
    kv = matmul(kvn, w.w_kv, name="kv_proj")
    cos, sin = _rope_tables(pos0, seq, hd)
    (k,) = head_norm_rope(kv[:, :cfg.kv_dim], w.k_norm, cos, sin, seq, (F32,), hd)
    v = kv[:, cfg.kv_dim:]
    (q,) = head_norm_rope(matmul(hn1, w.w_q, name="q_proj"), w.q_norm, cos, sin, seq, (BF16,), hd)

    o = attend(q, k, v)
    h = matmul(o, w.w_o, residual=h, name="attn_out")
    h = peer_block(h, w.norm_ffn[1], w.peer_w_query[1], w.peer_keys[1], w.peer_u[1], w.peer_vt[1], cfg)
    return (h.reshape(bsz, seq, d), s_fin[None],
            k.reshape(bsz, seq, cfg.n_kv_heads, hd), v.reshape(bsz, seq, cfg.n_kv_heads, hd))


def _attend_prompt(bsz, seq, cfg):
    def attend(q, k, v):
        n = q.shape[0]
        o = moba_prompt(q.reshape(bsz, seq, -1), k.reshape(bsz, seq, -1), v.reshape(bsz, seq, -1), cfg)
        return o.reshape(n, -1)
    return attend


def _attend_sample(db, t_new, cache_k, cache_v, page_table, past_len, cfg):
    hd, kvh, group, nh = cfg.head_dim, cfg.n_kv_heads, cfg.group, cfg.n_heads
    blk, ps = cfg.moba_block, cfg.page_size
    ppb = blk // ps
    n_full = past_len // blk
    assert past_len % blk == 0 and n_full > 0 and t_new <= blk and t_new <= LANES
    n_pages = n_full * ppb
    width = kvh * hd
    ck = cache_k.reshape(cache_k.shape[0], ps, width)
    cv = cache_v.reshape(cache_v.shape[0], ps, width)
    pt = page_table[:, :n_pages]

    def attend(q, k, v):
        q4 = q.reshape(db, t_new, kvh, group, hd).transpose(0, 2, 3, 1, 4)
        eye = jnp.eye(kvh, dtype=q.dtype)
        qbd = (q4[:, :, :, :, None, :] * eye[None, :, None, None, :, None]).reshape(db, nh * t_new, width)
        means = block_means(ck, pt, n_full, ppb, cfg).reshape(db, n_full, width)
        sel = moba_select(qbd, means, cfg)
        sel_pages = jnp.repeat(sel.transpose(0, 2, 1), ppb, axis=1)[..., None]
        pad = ((0, 0), (0, LANES - t_new), (0, 0))
        k_new = jnp.pad(k.reshape(db, t_new, width), pad)
        v_new = jnp.pad(v.reshape(db, t_new, width), pad)
        o = moba_sample(qbd, ck, cv, pt, sel_pages, k_new, v_new, t_new, cfg)
        o = o.reshape(db, nh, t_new, hd).transpose(0, 2, 1, 3).reshape(db * t_new, nh * hd)
        return o.astype(BF16)
    return attend


def _forward(x_prompt, x_sample, cache_k, cache_v, state_gla, page_table, weights, past_len, cfg):
    w = _prep_weights(*weights, cfg)
    bsz, seq, _ = x_prompt.shape
    db, t_new, _ = x_sample.shape
    s0_p = jnp.zeros((bsz, cfg.gla_heads, cfg.gla_dk, cfg.gla_dv), state_gla.dtype)
    y_p, st_p, k_p, v_p = _trunk(x_prompt, 0, s0_p, _attend_prompt(bsz, seq, cfg), w, cfg)
    attend_s = _attend_sample(db, t_new, cache_k, cache_v, page_table, past_len, cfg)
    y_s, st_s, k_s, v_s = _trunk(x_sample, past_len, state_gla[0], attend_s, w, cfg)
    return (y_p, y_s, st_p, st_s, k_p, v_p, k_s, v_s)


def kernel(x_prompt, x_sample, cache_k, cache_v, state_gla, page_table, norm_mix, norm_ffn, gla_w_in, gla_w_gate2, gla_b_gate, gla_norm, gla_w_out, norm_kv, w_kv, k_norm, w_q, q_norm, w_o, peer_w_query, peer_sub_keys, peer_u, peer_v):
    weights = (norm_mix, norm_ffn, gla_w_in, gla_w_gate2, gla_b_gate, gla_norm, gla_w_out, norm_kv, w_kv, k_norm,
               w_q, q_norm, w_o, peer_w_query, peer_sub_keys, peer_u, peer_v)
    past_len = page_table.shape[1] * CFG.page_size
    return _forward(x_prompt, x_sample, cache_k, cache_v, state_gla, page_table, weights, past_len, CFG)
```

```python
import functools
import math
from typing import NamedTuple

import numpy as np
import jax
import jax.numpy as jnp
from jax import lax
from jax.experimental import pallas as pl
from jax.experimental.pallas import tpu as pltpu

F32 = jnp.float32
BF16 = jnp.bfloat16

LANES = 128
MXU_COLS = 256
RMS_EPS = 1e-6
NEG_INF = -1e30
ROPE_THETA = 10000.0
VMEM_LIMIT_BYTES = 56 * 1024 * 1024


class Cfg(NamedTuple):
    d_model: int = 4096
    gla_heads: int = 4
    gla_gate_rank: int = 16
    gla_gate_tau: float = 16.0
    gla_chunk: int = 64
    head_dim: int = 128
    n_kv_heads: int = 8
    moba_block: int = 256
    moba_topk: int = 3
    page_size: int = 128
    peer_heads: int = 8
    peer_n_keys: int = 128
    peer_topk: int = 16
    peer_key_dim: int = 256

    @property
    def gla_dk_tot(self):
        return self.d_model // 2

    @property
    def gla_dk(self):
        return self.gla_dk_tot // self.gla_heads

    @property
    def gla_dv(self):
        return self.d_model // self.gla_heads

    @property
    def n_heads(self):
        return self.d_model // self.head_dim

    @property
    def group(self):
        return self.n_heads // self.n_kv_heads

    @property
    def kv_dim(self):
        return self.n_kv_heads * self.head_dim


CFG = Cfg()


def _params(*sem, flags=None):
    return pltpu.CompilerParams(dimension_semantics=sem, vmem_limit_bytes=VMEM_LIMIT_BYTES, flags=flags)


def _pick(n, prefs):
    for p in prefs:
        if n % p == 0:
            return p
    return n


def _rmsnorm_body(x_ref, g_ref, *o_refs):
    x = x_ref[...]
    y = x * lax.rsqrt(jnp.mean(x * x, axis=-1, keepdims=True) + RMS_EPS)
    for i, o_ref in enumerate(o_refs):
        o_ref[...] = (y * g_ref[i:i + 1, :]).astype(o_ref.dtype)


def rmsnorm_cast(x, gains):
    m, d = x.shape
    g = gains.shape[0]
    tm = _pick(m, (256, 128, 64))
    return pl.pallas_call(
        _rmsnorm_body,
        grid=(m // tm,),
        in_specs=[pl.BlockSpec((tm, d), lambda i: (i, 0)), pl.BlockSpec((g, d), lambda i: (0, 0))],
        out_specs=[pl.BlockSpec((tm, d), lambda i: (i, 0))] * g,
        out_shape=[jax.ShapeDtypeStruct((m, d), BF16)] * g,
        compiler_params=_params("parallel"),
        name="rmsnorm_cast",
    )(x, gains)


def _mm_body(x_ref, w_ref, o_ref):
    o_ref[...] = jnp.dot(x_ref[...], w_ref[...], preferred_element_type=F32).astype(o_ref.dtype)


def _mm_res_body(x_ref, w_ref, r_ref, o_ref):
    o_ref[...] = (r_ref[...] + jnp.dot(x_ref[...], w_ref[...], preferred_element_type=F32)).astype(o_ref.dtype)


def matmul(x, w, residual=None, out_dtype=F32, name="matmul"):
    m, k = x.shape
    n = w.shape[1]
    tm = _pick(m, (512, 256, 128, 64))
    tn = _pick(n, (512, 256, 128))
    in_specs = [pl.BlockSpec((tm, k), lambda j, i: (i, 0)), pl.BlockSpec((k, tn), lambda j, i: (0, j))]
    args = [x, w]
    body = _mm_body
    if residual is not None:
        in_specs.append(pl.BlockSpec((tm, tn), lambda j, i: (i, j)))
        args.append(residual)
        body = _mm_res_body
    return pl.pallas_call(
        body,
        grid=(n // tn, m // tm),
        in_specs=in_specs,
        out_specs=pl.BlockSpec((tm, tn), lambda j, i: (i, j)),
        out_shape=jax.ShapeDtypeStruct((m, n), out_dtype),
        compiler_params=_params("parallel", "parallel"),
        name=name,
    )(*args)


def _cumsum_rows(g):
    c = g.shape[0]
    row = lax.broadcasted_iota(jnp.int32, g.shape, 0)
    b = g
    s = 1
    while s < c:
        b = b + jnp.where(row >= s, pltpu.roll(b, s, axis=0), 0.0)
        s *= 2
    return b


def _log_sigmoid(x):
    return -(jnp.maximum(-x, 0.0) + jnp.log1p(jnp.exp(-jnp.abs(x))))


def _gla_body(q_ref, k_ref, v_ref, r_ref, z_ref, wg_ref, bg_ref, gn_ref, s0_ref, og_ref, sf_ref, st_ref,
              *, chunk, nsub, q_scale, inv_tau):
    t = pl.program_id(2)

    @pl.when(t == 0)
    def _():
        st_ref[...] = s0_ref[0, 0].T

    za = jnp.dot(z_ref[0].astype(BF16), wg_ref[...], preferred_element_type=F32) + bg_ref[...]
    log_a = _log_sigmoid(za) * inv_tau
    tril = (lax.broadcasted_iota(jnp.int32, (chunk, chunk), 0) >= lax.broadcasted_iota(jnp.int32, (chunk, chunk), 1))
    nt = (((1,), (1,)), ((), ()))
    tn = (((0,), (0,)), ((), ()))
    for i in range(nsub):
        sl = slice(i * chunk, (i + 1) * chunk)
        b = _cumsum_rows(log_a[sl])
        b_last = b[chunk - 1:chunk, :]
        q = q_ref[0, sl, :] * q_scale
        k = k_ref[0, sl, :]
        v = v_ref[0, sl, :].astype(BF16)
        qe = (q * jnp.exp(b)).astype(BF16)
        ke = (k * jnp.exp(-b)).astype(BF16)
        kd = (k * jnp.exp(b_last - b)).astype(BF16)
        att = lax.dot_general(qe, ke, nt, preferred_element_type=F32)
        att = jnp.where(tril, att, 0.0).astype(BF16)
        st = st_ref[...]
        o = lax.dot_general(qe, st.astype(BF16), nt, preferred_element_type=F32)
        o = o + jnp.dot(att, v, preferred_element_type=F32)
        st_ref[...] = st * jnp.exp(b_last) + lax.dot_general(v, kd, tn, preferred_element_type=F32)
        on = o * lax.rsqrt(jnp.mean(o * o, axis=-1, keepdims=True) + RMS_EPS) * gn_ref[...]
        r = r_ref[0, sl, :]
        og_ref[0, sl, :] = (on * (r * jax.nn.sigmoid(r))).astype(og_ref.dtype)

    @pl.when(t == pl.num_programs(2) - 1)
    def _():
        sf_ref[0, 0] = st_ref[...].T


def gla(proj, z, wg, bg, gn, s0, cfg):
    bsz, seq, _ = proj.shape
    h, dk, dv = cfg.gla_heads, cfg.gla_dk, cfg.gla_dv
    chunk = math.gcd(seq, cfg.gla_chunk)
    rows = _pick(seq, (4 * chunk, 2 * chunk, chunk))
    nsub = rows // chunk
    kq, kk, kv, kr = 0, h, (2 * h * dk) // dv, (2 * h * dk) // dv + h
    body = functools.partial(_gla_body, chunk=chunk, nsub=nsub, q_scale=dk ** -0.5, inv_tau=1.0 / cfg.gla_gate_tau)
    return pl.pallas_call(
        body,
        grid=(bsz, h, seq // rows),
        in_specs=[
            pl.BlockSpec((1, rows, dk), lambda b, hh, t: (b, t, kq + hh)),
            pl.BlockSpec((1, rows, dk), lambda b, hh, t: (b, t, kk + hh)),
            pl.BlockSpec((1, rows, dv), lambda b, hh, t: (b, t, kv + hh)),
            pl.BlockSpec((1, rows, dv), lambda b, hh, t: (b, t, kr + hh)),
            pl.BlockSpec((1, rows, LANES), lambda b, hh, t: (b, t, 0)),
            pl.BlockSpec((LANES, dk), lambda b, hh, t: (0, hh)),
            pl.BlockSpec((1, dk), lambda b, hh, t: (0, hh)),
            pl.BlockSpec((1, dv), lambda b, hh, t: (0, hh)),
            pl.BlockSpec((1, 1, dk, dv), lambda b, hh, t: (b, hh, 0, 0)),
        ],
        out_specs=[
            pl.BlockSpec((1, rows, dv), lambda b, hh, t: (b, t, hh)),
            pl.BlockSpec((1, 1, dk, dv), lambda b, hh, t: (b, hh, 0, 0)),
        ],
        out_shape=[
            jax.ShapeDtypeStruct((bsz, seq, h * dv), BF16),
            jax.ShapeDtypeStruct((bsz, h, dk, dv), F32),
        ],
        scratch_shapes=[pltpu.VMEM((dv, dk), F32)],
        compiler_params=_params("parallel", "parallel", "arbitrary"),
        name="gla",
    )(proj, proj, proj, proj, z, wg, bg, gn, s0)


def _headrope_body(x_ref, g_ref, cos_ref, sin_ref, *o_refs, nh, hd):
    for hh in range(nh):
        x = x_ref[:, hh * hd:(hh + 1) * hd]
        y = x * lax.rsqrt(jnp.mean(x * x, axis=-1, keepdims=True) + RMS_EPS) * g_ref[...]
        out = y * cos_ref[...] + pltpu.roll(y, hd // 2, axis=1) * sin_ref[...]
        for o_ref in o_refs:
            o_ref[:, hh * hd:(hh + 1) * hd] = out.astype(o_ref.dtype)


def head_norm_rope(x, gain, cos, sin, seq, out_dtypes, hd):
    n, width = x.shape
    nh = width // hd
    tm = _pick(seq, (256, 128, 64, 32, 16, 8))
    per = seq // tm
    return pl.pallas_call(
        functools.partial(_headrope_body, nh=nh, hd=hd),
        grid=(n // tm,),
        in_specs=[
            pl.BlockSpec((tm, width), lambda i: (i, 0)),
            pl.BlockSpec((1, hd), lambda i: (0, 0)),
            pl.BlockSpec((tm, hd), lambda i: (i % per, 0)),
            pl.BlockSpec((tm, hd), lambda i: (i % per, 0)),
        ],
        out_specs=[pl.BlockSpec((tm, width), lambda i: (i, 0)) for _ in out_dtypes],
        out_shape=[jax.ShapeDtypeStruct((n, width), dt) for dt in out_dtypes],
        compiler_params=_params("parallel"),
        name="head_norm_rope",
    )(x, gain, cos, sin)


def _rope_tables(pos0, seq, hd):
    half = hd // 2
    inv = ROPE_THETA ** (-np.arange(half, dtype=np.float64) / half)
    ang = (pos0 + np.arange(seq, dtype=np.float64))[:, None] * inv[None, :]
    cos, sin = np.cos(ang), np.sin(ang)
    return (jnp.asarray(np.concatenate([cos, cos], axis=1), F32),
            jnp.asarray(np.concatenate([-sin, sin], axis=1), F32))


def _top_values(s, k):
    vals = []
    cur = s
    for i in range(k):
        m = jnp.max(cur, axis=0, keepdims=True)
        vals.append(m)
        if i + 1 < k:
            cur = jnp.where(cur == m, NEG_INF, cur)
    return vals


ROW_TAU, ROW_M1, ROW_M2, ROW_INVZ = range(4)


def _peer_route_body(q_ref, keys_ref, s1_ref, s2_ref, rows_ref, *, heads, nkeys, half, topk):
    nt = (((1,), (1,)), ((), ()))
    tm = q_ref.shape[0]
    sub = 8
    row = lax.broadcasted_iota(jnp.int32, (sub, tm), 0)
    for hh in range(heads):
        st = []
        for p in range(2):
            g = 2 * hh + p
            qg = q_ref[:, g * half:(g + 1) * half].astype(BF16)
            st.append(lax.dot_general(keys_ref[g], qg, nt, preferred_element_type=F32))
        s1, s2 = st
        v1 = _top_values(s1, topk)
        v2 = _top_values(s2, topk)
        v2g = []
        for g0 in range(0, topk, sub):
            grp = jnp.full((sub, tm), NEG_INF, F32)
            for b in range(g0, min(g0 + sub, topk)):
                grp = jnp.where(row == b - g0, v2[b], grp)
            v2g.append(grp)
        cands = []
        for a in range(topk):
            bmax = topk // (a + 1)
            for gi, grp in enumerate(v2g):
                if gi * sub < bmax:
                    cands.append(jnp.where(row < bmax - gi * sub, v1[a] + grp, NEG_INF))
        cur = cands
        tau = None
        for i in range(topk):
            tau = functools.reduce(jnp.maximum, [jnp.max(c, axis=0, keepdims=True) for c in cur])
            if i + 1 < topk:
                cur = [jnp.where(c == tau, NEG_INF, c) for c in cur]
        m1, m2 = v1[0], v2[0]
        mx = m1 + m2
        z = functools.reduce(
            jnp.add, [jnp.sum(jnp.where(c >= tau, jnp.exp(c - mx), 0.0), axis=0, keepdims=True) for c in cands])
        s1_ref[hh] = s1
        s2_ref[hh] = s2
        rows_ref[ROW_TAU, hh:hh + 1, :] = tau
        rows_ref[ROW_M1, hh:hh + 1, :] = m1
        rows_ref[ROW_M2, hh:hh + 1, :] = m2
        rows_ref[ROW_INVZ, hh:hh + 1, :] = 1.0 / z


def peer_route(q, keys, cfg):
    n = q.shape[0]
    heads, nkeys, half = cfg.peer_heads, cfg.peer_n_keys, cfg.peer_key_dim // 2
    tm = _pick(n, (256, 128))
    tab = jax.ShapeDtypeStruct((heads, nkeys, n), F32)
    tab_spec = pl.BlockSpec((heads, nkeys, tm), lambda i: (0, 0, i))
    return pl.pallas_call(
        functools.partial(_peer_route_body, heads=heads, nkeys=nkeys, half=half, topk=cfg.peer_topk),
        grid=(n // tm,),
        in_specs=[pl.BlockSpec((tm, q.shape[1]), lambda i: (i, 0)),
                  pl.BlockSpec(keys.shape, lambda i: (0, 0, 0))],
        out_specs=[tab_spec, tab_spec, pl.BlockSpec((4, heads, tm), lambda i: (0, 0, i))],
        out_shape=[tab, tab, jax.ShapeDtypeStruct((4, heads, n), F32)],
        compiler_params=_params("parallel"),
        name="peer_route",
    )(q, keys)


def _gelu(x):
    return 0.5 * x * (1.0 + lax.erf(x * (1.0 / math.sqrt(2.0))))


SUBLANES = 8
GATE_LANES = 512


def _peer_row_tables(bc_ref, jt, s1_ref, rows_ref, *, heads, nsub):
    tm = bc_ref.shape[-1]
    for r in range(nsub):
        i1 = jt * nsub + r
        for hh in range(heads):
            s1row = s1_ref[hh, pl.ds(i1, 1), :]
            crow = jnp.exp(s1row - rows_ref[ROW_M1, hh:hh + 1, :]) * rows_ref[ROW_INVZ, hh:hh + 1, :]
            bc_ref[hh, r, 0] = jnp.broadcast_to(s1row, (SUBLANES, tm))
            bc_ref[hh, r, 1] = jnp.broadcast_to(crow, (SUBLANES, tm))


def _peer_gate_passes(dst_ref, s2_ref, e2_ref, taub_ref, bc_ref, *, heads, nkeys, nsub):
    tm = dst_ref.shape[-1]
    tw = min(tm, GATE_LANES)

    def one(r, c, l):
        rows, cols = slice(c, c + SUBLANES), slice(l, l + tw)
        w = None
        for hh in range(heads):
            hit = (s2_ref[hh, rows, cols] + bc_ref[hh, r, 0, :, cols]) >= taub_ref[hh, :, cols]
            term = jnp.where(hit, e2_ref[hh, rows, cols], 0.0) * bc_ref[hh, r, 1, :, cols]
            w = term if w is None else w + term
        dst_ref[r * nkeys + c:r * nkeys + c + SUBLANES, cols] = w

    return [functools.partial(one, r, c, l)
            for r in range(nsub) for c in range(0, nkeys, SUBLANES) for l in range(0, tm, tw)]


def _peer_dense_body(xt_ref, u_ref, v_ref, s1_ref, s2_ref, rows_ref, o_ref, e2_ref, taub_ref, bc_ref,
                     gate_a_ref, gate_b_ref, ht_ref, *, heads, nkeys, nsub):
    j = pl.program_id(1)
    last = pl.num_programs(1) - 1
    row_tables = functools.partial(_peer_row_tables, bc_ref, s1_ref=s1_ref, rows_ref=rows_ref, heads=heads, nsub=nsub)
    gate_passes = functools.partial(_peer_gate_passes, s2_ref=s2_ref, e2_ref=e2_ref, taub_ref=taub_ref, bc_ref=bc_ref,
                                    heads=heads, nkeys=nkeys, nsub=nsub)
    tm = xt_ref.shape[1]
    d = o_ref.shape[1]

    @pl.when(j == 0)
    def _():
        o_ref[...] = jnp.zeros_like(o_ref)
        for hh in range(heads):
            e2_ref[hh] = jnp.exp(s2_ref[hh] - rows_ref[ROW_M2, hh:hh + 1, :])
            taub_ref[hh] = jnp.broadcast_to(rows_ref[ROW_TAU, hh:hh + 1, :], (SUBLANES, tm))
        row_tables(0)
        for run in gate_passes(gate_a_ref):
            run()

    def step(cur_ref, nxt_ref):
        row_tables(jnp.minimum(j + 1, last))
        passes = gate_passes(nxt_ref)
        tw = min(tm, MXU_COLS)
        n_tok, n_out = tm // tw, d // MXU_COLS
        n_first = n_tok * nsub
        first = len(passes) // 2
        quota = ([first // n_first + (i < first % n_first) for i in range(n_first)]
                 + [(len(passes) - first) // n_out + (i < (len(passes) - first) % n_out) for i in range(n_out)])
        it = iter(passes)
        for c in range(n_tok):
            cols = slice(c * tw, (c + 1) * tw)
            for r in range(nsub):
                rows = slice(r * nkeys, (r + 1) * nkeys)
                act = _gelu(jnp.dot(u_ref[rows, :], xt_ref[:, cols], preferred_element_type=F32))
                ht_ref[cols, rows] = (cur_ref[rows, cols] * act).T.astype(BF16)
                for _ in range(quota[c * nsub + r]):
                    next(it)()
        for p in range(n_out):
            cols = slice(p * MXU_COLS, (p + 1) * MXU_COLS)
            o_ref[:, cols] += jnp.dot(ht_ref[...], v_ref[:, cols], preferred_element_type=F32)
            for _ in range(quota[n_first + p]):
                next(it)()

    @pl.when(j % 2 == 0)
    def _():
        step(gate_a_ref, gate_b_ref)

    @pl.when(j % 2 == 1)
    def _():
        step(gate_b_ref, gate_a_ref)


def peer_dense(xt, u, v, s1, s2, rows, cfg):
    d, n = xt.shape
    e = u.shape[0]
    heads, nkeys = cfg.peer_heads, cfg.peer_n_keys
    tm = _pick(n, (512, 256, 128))
    nsub = 4
    te = nsub * nkeys
    once = dict(pipeline_mode=pl.Buffered(1))
    tab_spec = pl.BlockSpec((heads, nkeys, tm), lambda i, j: (0, 0, i), **once)
    return pl.pallas_call(
        functools.partial(_peer_dense_body, heads=heads, nkeys=nkeys, nsub=nsub),
        grid=(n // tm, e // te),
        in_specs=[
            pl.BlockSpec((d, tm), lambda i, j: (0, i), **once),
            pl.BlockSpec((te, d), lambda i, j: (j, 0)),
            pl.BlockSpec((te, d), lambda i, j: (j, 0)),
            tab_spec, tab_spec,
            pl.BlockSpec((4, heads, tm), lambda i, j: (0, 0, i), **once),
        ],
        out_specs=pl.BlockSpec((tm, d), lambda i, j: (i, 0)),
        out_shape=jax.ShapeDtypeStruct((n, d), F32),
        scratch_shapes=[
            pltpu.VMEM((heads, nkeys, tm), F32),
            pltpu.VMEM((heads, SUBLANES, tm), F32),
            pltpu.VMEM((heads, nsub, 2, SUBLANES, tm), F32),
            pltpu.VMEM((te, tm), F32),
            pltpu.VMEM((te, tm), F32),
            pltpu.VMEM((tm, te), BF16),
        ],
        compiler_params=_params("parallel", "arbitrary"),
        name="peer_dense",
    )(xt, u, v, s1, s2, rows)


def peer_block(h, g_ffn, w_query, keys, u, v, cfg):
    n = h.shape[0]
    npad = -(-n // LANES) * LANES
    (xn,) = rmsnorm_cast(h, g_ffn[None, :])
    if npad != n:
        xn = jnp.pad(xn, ((0, npad - n), (0, 0)))
    q = matmul(xn, w_query, name="peer_query")
    s1, s2, rows = peer_route(q, keys, cfg)
    y = peer_dense(xn.T, u, v, s1, s2, rows, cfg)
    return h + y[:n]


def _moba_prompt_body(q_ref, k_ref, v_ref, o_ref, means_ref, sel_ref, m_ref, l_ref, acc_ref,
                      *, blk, nblk, group, hd, topk, scale):
    qb = pl.program_id(2)
    nt = (((1,), (1,)), ((), ()))
    tn = (((0,), (0,)), ((), ()))
    rows = group * blk

    @pl.when(qb == 0)
    def _():
        for n in range(nblk):
            means_ref[n:n + 1, :] = jnp.mean(k_ref[0, n * blk:(n + 1) * blk, :], axis=0, keepdims=True)

    q4 = jnp.concatenate([q_ref[0, :, g * hd:(g + 1) * hd] for g in range(group)], axis=0)

    gate = lax.dot_general(means_ref[...].astype(BF16), q4, nt, preferred_element_type=F32)
    bidx = lax.broadcasted_iota(jnp.int32, gate.shape, 0)
    cand = bidx < qb
    gate = jnp.where(cand, gate, NEG_INF)
    rank = jnp.zeros(gate.shape, F32)
    for mm in range(nblk):
        gm = gate[mm:mm + 1, :]
        beats = (gm > gate) | ((gm == gate) & (mm < bidx))
        rank = rank + jnp.where(beats, 1.0, 0.0)
    sel_ref[...] = jnp.where((rank < topk) & cand, 1.0, 0.0)

    kpos = lax.broadcasted_iota(jnp.int32, (blk, rows), 0)
    qpos = lax.broadcasted_iota(jnp.int32, (blk, rows), 1) % blk
    k_own = k_ref[0, pl.ds(qb * blk, blk), :].astype(BF16)
    v_own = v_ref[0, pl.ds(qb * blk, blk), :].astype(BF16)
    s = lax.dot_general(k_own, q4, nt, preferred_element_type=F32) * scale
    s = jnp.where(kpos <= qpos, s, NEG_INF)
    m0 = jnp.max(s, axis=0, keepdims=True)
    p = jnp.exp(s - m0)
    m_ref[...] = m0
    l_ref[...] = jnp.sum(p, axis=0, keepdims=True)
    acc_ref[...] = lax.dot_general(v_own, p.astype(BF16), tn, preferred_element_type=F32)

    def past(n, carry):
        kb = k_ref[0, pl.ds(n * blk, blk), :].astype(BF16)
        vb = v_ref[0, pl.ds(n * blk, blk), :].astype(BF16)
        on = sel_ref[pl.ds(n, 1), :] > 0.5
        sb = lax.dot_general(kb, q4, nt, preferred_element_type=F32) * scale
        sb = jnp.where(on, sb, NEG_INF)
        m_old = m_ref[...]
        m_new = jnp.maximum(m_old, jnp.max(sb, axis=0, keepdims=True))
        pb = jnp.where(on, jnp.exp(sb - m_new), 0.0)
        alpha = jnp.exp(m_old - m_new)
        m_ref[...] = m_new
        l_ref[...] = alpha * l_ref[...] + jnp.sum(pb, axis=0, keepdims=True)
        acc_ref[...] = alpha * acc_ref[...] + lax.dot_general(vb, pb.astype(BF16), tn, preferred_element_type=F32)
        return carry

    lax.fori_loop(0, qb, past, 0)

    out = (acc_ref[...] / l_ref[...]).T
    for g in range(group):
        o_ref[0, :, g * hd:(g + 1) * hd] = out[g * blk:(g + 1) * blk, :].astype(o_ref.dtype)


def moba_prompt(q, k, v, cfg):
    bsz, seq, _ = q.shape
    blk, hd, group, kvh = cfg.moba_block, cfg.head_dim, cfg.group, cfg.n_kv_heads
    assert seq % blk == 0
    nblk = seq // blk
    rows = group * blk
    nsel = -(-nblk // 8) * 8
    body = functools.partial(_moba_prompt_body, blk=blk, nblk=nblk, group=group, hd=hd, topk=cfg.moba_topk,
                             scale=hd ** -0.5)
    return pl.pallas_call(
        body,
        grid=(bsz, kvh, nblk),
        in_specs=[
            pl.BlockSpec((1, blk, group * hd), lambda b, kh, i: (b, i, kh)),
            pl.BlockSpec((1, seq, hd), lambda b, kh, i: (b, 0, kh)),
            pl.BlockSpec((1, seq, hd), lambda b, kh, i: (b, 0, kh)),
        ],
        out_specs=pl.BlockSpec((1, blk, group * hd), lambda b, kh, i: (b, i, kh)),
        out_shape=jax.ShapeDtypeStruct(q.shape, BF16),
        scratch_shapes=[
            pltpu.VMEM((nblk, hd), F32),
            pltpu.VMEM((nblk, rows), F32),
            pltpu.VMEM((1, rows), F32),
            pltpu.VMEM((1, rows), F32),
            pltpu.VMEM((hd, rows), F32),
        ],
        compiler_params=_params("parallel", "parallel", "arbitrary"),
        name="moba_prompt",
    )(q, k, v)


def _block_means_body(pt_ref, *refs, inv_rows):
    k_refs, o_ref = refs[:-1], refs[-1]
    o_ref[0, 0] = functools.reduce(jnp.add, [jnp.sum(k_ref[0], axis=0) for k_ref in k_refs]) * inv_rows


def _page_specs(ppb, ps, kvh, hd):
    return [pl.BlockSpec((1, ps, kvh, hd), functools.partial(lambda p, b, n, pt: (pt[b, n * ppb + p], 0, 0, 0), p))
            for p in range(ppb)]


def block_means(cache_k, page_table, n_full, ppb, cfg):
    db = page_table.shape[0]
    _, ps, kvh, hd = cache_k.shape
    return pl.pallas_call(
        functools.partial(_block_means_body, inv_rows=1.0 / (ps * ppb)),
        grid_spec=pltpu.PrefetchScalarGridSpec(
            num_scalar_prefetch=1,
            grid=(db, n_full),
            in_specs=_page_specs(ppb, ps, kvh, hd),
            out_specs=pl.BlockSpec((1, 1, kvh, hd), lambda b, n, pt: (b, n, 0, 0)),
        ),
        out_shape=jax.ShapeDtypeStruct((db, n_full, kvh, hd), F32),
        compiler_params=_params("parallel", "parallel"),
        name="block_means",
    )(page_table, *([cache_k] * ppb))


def _moba_select_body(q_ref, means_ref, sel_ref, *, topk, kvh):
    nt = (((1,), (1,)), ((), ()))
    gate = jnp.concatenate(
        [lax.dot_general(q_ref[0, kh], means_ref[0, :, kh, :].astype(BF16), nt, preferred_element_type=F32)
         for kh in range(kvh)], axis=0)
    lane = lax.broadcasted_iota(jnp.int32, gate.shape, 1)
    nb = gate.shape[1]
    sel = jnp.zeros(gate.shape, F32)
    for _ in range(topk):
        m = jnp.max(gate, axis=1, keepdims=True)
        first = jnp.min(jnp.where(gate == m, lane, nb), axis=1, keepdims=True)
        pick = lane == first
        sel = jnp.where(pick, 1.0, sel)
        gate = jnp.where(pick, -3.0e38, gate)
    sel_ref[0] = sel


def moba_select(q4, means, cfg):
    db, kvh, rpk, hd = q4.shape
    n_full = means.shape[1]
    rows = kvh * rpk
    return pl.pallas_call(
        functools.partial(_moba_select_body, topk=min(cfg.moba_topk, n_full), kvh=kvh),
        grid=(db,),
        in_specs=[pl.BlockSpec((1, kvh, rpk, hd), lambda b: (b, 0, 0, 0)),
                  pl.BlockSpec((1, n_full, kvh, hd), lambda b: (b, 0, 0, 0))],
        out_specs=pl.BlockSpec((1, rows, n_full), lambda b: (b, 0, 0)),
        out_shape=jax.ShapeDtypeStruct((db, rows, n_full), F32),
        compiler_params=_params("parallel"),
        name="moba_select",
    )(q4, means)


def _moba_sample_body(pt_ref, *refs, ppb, scale, t_new, kvh, rpk):
    q_ref, sel_ref, kn_ref, vn_ref = refs[0], refs[1 + 2 * ppb], refs[2 + 2 * ppb], refs[3 + 2 * ppb]
    k_refs, v_refs = refs[1:1 + ppb], refs[1 + ppb:1 + 2 * ppb]
    o_ref, m_ref, l_ref, acc_ref = refs[4 + 2 * ppb:]
    n = pl.program_id(1)
    nt = (((1,), (1,)), ((), ()))

    @pl.when(n == 0)
    def _():
        m_ref[...] = jnp.full(m_ref.shape, NEG_INF, F32)
        l_ref[...] = jnp.zeros(l_ref.shape, F32)
        acc_ref[...] = jnp.zeros(acc_ref.shape, F32)

    def absorb(k_pages, v_pages, on):
        s = jnp.concatenate(
            [jnp.concatenate(
                [lax.dot_general(q_ref[0, kh], kp[0, :, kh, :].astype(BF16), nt, preferred_element_type=F32)
                 for kp in k_pages], axis=1) for kh in range(kvh)], axis=0) * scale
        s = jnp.where(on, s, NEG_INF)
        m_old = m_ref[...]
        m_new = jnp.maximum(m_old, jnp.max(s, axis=1, keepdims=True))
        p = jnp.where(on, jnp.exp(s - m_new), 0.0).astype(BF16)
        alpha = jnp.exp(m_old - m_new)
        m_ref[...] = m_new
        l_ref[...] = alpha * l_ref[...] + jnp.sum(p.astype(F32), axis=1, keepdims=True)
        ps = k_pages[0].shape[1]
        pv = jnp.concatenate(
            [functools.reduce(jnp.add, [
                jnp.dot(p[kh * rpk:(kh + 1) * rpk, i * ps:(i + 1) * ps], vp[0, :, kh, :].astype(BF16),
                        preferred_element_type=F32) for i, vp in enumerate(v_pages)])
             for kh in range(kvh)], axis=0)
        acc_ref[...] = alpha * acc_ref[...] + pv

    absorb(k_refs, v_refs, sel_ref[0, 0] > 0.5)

    @pl.when(n == pl.num_programs(1) - 1)
    def _():
        shape = (kvh * rpk, kn_ref.shape[1])
        tq = lax.broadcasted_iota(jnp.int32, shape, 0) % t_new
        tk = lax.broadcasted_iota(jnp.int32, shape, 1)
        absorb([kn_ref], [vn_ref], tk <= tq)
        o_ref[0] = acc_ref[...] / l_ref[...]


def moba_sample(q4, cache_k, cache_v, page_table, sel, k_new, v_new, t_new, ppb, cfg):
    db, kvh, rpk, hd = q4.shape
    rows = kvh * rpk
    n_full = sel.shape[1]
    ps = cache_k.shape[1]
    body = functools.partial(_moba_sample_body, ppb=ppb, scale=hd ** -0.5, t_new=t_new, kvh=kvh, rpk=rpk)
    new_spec = pl.BlockSpec((1, ps, kvh, hd), lambda b, n, pt: (b, 0, 0, 0))
    return pl.pallas_call(
        body,
        grid_spec=pltpu.PrefetchScalarGridSpec(
            num_scalar_prefetch=1,
            grid=(db, n_full),
            in_specs=([pl.BlockSpec((1, kvh, rpk, hd), lambda b, n, pt: (b, 0, 0, 0))]
                      + _page_specs(ppb, ps, kvh, hd) + _page_specs(ppb, ps, kvh, hd)
                      + [pl.BlockSpec((1, 1, rows, 1), lambda b, n, pt: (b, n, 0, 0)), new_spec, new_spec]),
            out_specs=pl.BlockSpec((1, rows, hd), lambda b, n, pt: (b, 0, 0)),
            scratch_shapes=[
                pltpu.VMEM((rows, 1), F32),
                pltpu.VMEM((rows, 1), F32),
                pltpu.VMEM((rows, hd), F32),
            ],
        ),
        out_shape=jax.ShapeDtypeStruct((db, rows, hd), F32),
        compiler_params=_params("parallel", "arbitrary"),
        name="moba_sample",
    )(page_table, q4, *([cache_k] * ppb), *([cache_v] * ppb), sel, k_new, v_new)


def _cast_body(x_ref, o_ref):
    o_ref[...] = x_ref[0].astype(o_ref.dtype)


def cast_layer(w, layer, cols=None):
    _, r, c = w.shape
    cols = c if cols is None else cols
    tr = next(t for t in (512, 256, 128, 64, 32, 16) if r % t == 0 and t * cols * 4 <= 8 * 1024 * 1024)
    return pl.pallas_call(
        _cast_body,
        grid=(r // tr,),
        in_specs=[pl.BlockSpec((1, tr, cols), lambda i: (layer, i, 0))],
        out_specs=pl.BlockSpec((tr, cols), lambda i: (i, 0)),
        out_shape=jax.ShapeDtypeStruct((r, cols), BF16),
        compiler_params=_params("parallel"),
        name="cast_bf16",
    )(w)


class Weights(NamedTuple):
    norm_mix: jax.Array
    norm_ffn: jax.Array
    w_in: jax.Array
    w_z: jax.Array
    w_gate2: jax.Array
    b_gate: jax.Array
    gla_norm: jax.Array
    w_out: jax.Array
    norm_kv: jax.Array
    w_kv: jax.Array
    k_norm: jax.Array
    w_q: jax.Array
    q_norm: jax.Array
    w_o: jax.Array
    peer_w_query: tuple
    peer_keys: tuple
    peer_u: tuple
    peer_v: tuple


def _prep_weights(norm_mix, norm_ffn, gla_w_in, gla_w_gate2, gla_b_gate, gla_norm, gla_w_out, norm_kv, w_kv,
                  k_norm, w_q, q_norm, w_o, peer_w_query, peer_sub_keys, peer_u, peer_v, cfg):
    main = 2 * cfg.gla_dk_tot + 2 * cfg.d_model
    rank = cfg.gla_gate_rank
    w_in = gla_w_in[0]
    depth = norm_mix.shape[0]
    return Weights(
        norm_mix=norm_mix,
        norm_ffn=norm_ffn,
        w_in=cast_layer(gla_w_in, 0, cols=main),
        w_z=jnp.pad(w_in[:, main:], ((0, 0), (0, LANES - rank))).astype(BF16),
        w_gate2=jnp.pad(gla_w_gate2[0], ((0, LANES - rank), (0, 0))).astype(BF16),
        b_gate=gla_b_gate[0][None, :],
        gla_norm=gla_norm[0].reshape(1, -1),
        w_out=gla_w_out[0].astype(BF16),
        norm_kv=norm_kv,
        w_kv=w_kv.astype(BF16),
        k_norm=k_norm[None, :],
        w_q=w_q[0].astype(BF16),
        q_norm=q_norm[0][None, :],
        w_o=w_o[0].astype(BF16),
        peer_w_query=tuple(peer_w_query[l].astype(BF16) for l in range(depth)),
        peer_keys=tuple(peer_sub_keys[l].reshape(2 * cfg.peer_heads, cfg.peer_n_keys, -1).astype(BF16)
                        for l in range(depth)),
        peer_u=tuple(cast_layer(peer_u, l) for l in range(depth)),
        peer_v=tuple(cast_layer(peer_v, l) for l in range(depth)),
    )


def _trunk(x, pos0, s0, attend, w, cfg):
    bsz, seq, d = x.shape
    n = bsz * seq
    hd = cfg.head_dim
    xf = x.reshape(n, d)

    (hn,) = rmsnorm_cast(xf, w.norm_mix[0:1])
    proj = matmul(hn, w.w_in, name="gla_in")
    z = matmul(hn, w.w_z, name="gla_gate_in")
    og, s_fin = gla(proj.reshape(bsz, seq, -1), z.reshape(bsz, seq, -1), w.w_gate2, w.b_gate, w.gla_norm, s0, cfg)
    h = matmul(og.reshape(n, -1), w.w_out, residual=xf, name="gla_out")
    h = peer_block(h, w.norm_ffn[0], w.peer_w_query[0], w.peer_keys[0], w.peer_u[0], w.peer_v[0], cfg)

    kvn, hn1 = rmsnorm_cast(h, jnp.stack([w.norm_kv, w.norm_mix[1]]))
    kv = matmul(kvn, w.w_kv, name="kv_proj")
    cos, sin = _rope_tables(pos0, seq, hd)
    (k,) = head_norm_rope(kv[:, :cfg.kv_dim], w.k_norm, cos, sin, seq, (F32,), hd)
    v = kv[:, cfg.kv_dim:]
    (q,) = head_norm_rope(matmul(hn1, w.w_q, name="q_proj"), w.q_norm, cos, sin, seq, (BF16,), hd)

    o = attend(q, k, v)
    h = matmul(o, w.w_o, residual=h, name="attn_out")
    h = peer_block(h, w.norm_ffn[1], w.peer_w_query[1], w.peer_keys[1], w.peer_u[1], w.peer_v[1], cfg)
    return (h.reshape(bsz, seq, d), s_fin[None],
            k.reshape(bsz, seq, cfg.n_kv_heads, hd), v.reshape(bsz, seq, cfg.n_kv_heads, hd))


def _attend_prompt(bsz, seq, cfg):
    def attend(q, k, v):
        n = q.shape[0]
        o = moba_prompt(q.reshape(bsz, seq, -1), k.reshape(bsz, seq, -1), v.reshape(bsz, seq, -1), cfg)
        return o.reshape(n, -1)
    return attend


def _attend_sample(db, t_new, cache_k, cache_v, page_table, past_len, cfg):
    hd, kvh, group, nh = cfg.head_dim, cfg.n_kv_heads, cfg.group, cfg.n_heads
    blk, ps = cfg.moba_block, cfg.page_size
    ppb = blk // ps
    n_full = past_len // blk
    assert past_len % blk == 0 and n_full > 0 and t_new <= ps
    pt = page_table[:, :n_full * ppb]

    def attend(q, k, v):
        q4 = q.reshape(db, t_new, kvh, group, hd).transpose(0, 2, 3, 1, 4).reshape(db, kvh, group * t_new, hd)
        means = block_means(cache_k, pt, n_full, ppb, cfg)
        sel = moba_select(q4, means, cfg)
        sel = sel.transpose(0, 2, 1)[..., None]
        pad = ((0, 0), (0, ps - t_new), (0, 0), (0, 0))
        k_new = jnp.pad(k.reshape(db, t_new, kvh, hd), pad)
        v_new = jnp.pad(v.reshape(db, t_new, kvh, hd), pad)
        o = moba_sample(q4, cache_k, cache_v, pt, sel, k_new, v_new, t_new, ppb, cfg)
        o = o.reshape(db, nh, t_new, hd).transpose(0, 2, 1, 3).reshape(db * t_new, nh * hd)
        return o.astype(BF16)
    return attend


def _forward(x_prompt, x_sample, cache_k, cache_v, state_gla, page_table, weights, past_len, cfg):
    w = _prep_weights(*weights, cfg)
    bsz, seq, _ = x_prompt.shape
    db, t_new, _ = x_sample.shape
    s0_p = jnp.zeros((bsz, cfg.gla_heads, cfg.gla_dk, cfg.gla_dv), state_gla.dtype)
    y_p, st_p, k_p, v_p = _trunk(x_prompt, 0, s0_p, _attend_prompt(bsz, seq, cfg), w, cfg)
    attend_s = _attend_sample(db, t_new, cache_k, cache_v, page_table, past_len, cfg)
    y_s, st_s, k_s, v_s = _trunk(x_sample, past_len, state_gla[0], attend_s, w, cfg)
    return (y_p, y_s, st_p, st_s, k_p, v_p, k_s, v_s)


def kernel(x_prompt, x_sample, cache_k, cache_v, state_gla, page_table, norm_mix, norm_ffn, gla_w_in, gla_w_gate2, gla_b_gate, gla_norm, gla_w_out, norm_kv, w_kv, k_norm, w_q, q_norm, w_o, peer_w_query, peer_sub_keys, peer_u, peer_v):
    weights = (norm_mix, norm_ffn, gla_w_in, gla_w_gate2, gla_b_gate, gla_norm, gla_w_out, norm_kv, w_kv, k_norm,
               w_q, q_norm, w_o, peer_w_query, peer_sub_keys, peer_u, peer_v)
    past_len = page_table.shape[1] * CFG.page_size
    return _forward(x_prompt, x_sample, cache_k, cache_v, state_gla, page_table, weights, past_len, CFG)
```

```python
import functools
import math
from typing import NamedTuple

import numpy as np
import jax
import jax.numpy as jnp
from jax import lax
from jax.experimental import pallas as pl
from jax.experimental.pallas import tpu as pltpu

F32 = jnp.float32
BF16 = jnp.bfloat16

LANES = 128
MXU_COLS = 256
RMS_EPS = 1e-6
NEG_INF = -1e30
ROPE_THETA = 10000.0
VMEM_LIMIT_BYTES = 56 * 1024 * 1024


class Cfg(NamedTuple):
    d_model: int = 4096
    gla_heads: int = 4
    gla_gate_rank: int = 16
    gla_gate_tau: float = 16.0
    gla_chunk: int = 64
    head_dim: int = 128
    n_kv_heads: int = 8
    moba_block: int = 256
    moba_topk: int = 3
    page_size: int = 128
    peer_heads: int = 8
    peer_n_keys: int = 128
    peer_topk: int = 16
    peer_key_dim: int = 256

    @property
    def gla_dk_tot(self):
        return self.d_model // 2

    @property
    def gla_dk(self):
        return self.gla_dk_tot // self.gla_heads

    @property
    def gla_dv(self):
        return self.d_model // self.gla_heads

    @property
    def n_heads(self):
        return self.d_model // self.head_dim

    @property
    def group(self):
        return self.n_heads // self.n_kv_heads

    @property
    def kv_dim(self):
        return self.n_kv_heads * self.head_dim


CFG = Cfg()


def _params(*sem, flags=None):
    return pltpu.CompilerParams(dimension_semantics=sem, vmem_limit_bytes=VMEM_LIMIT_BYTES, flags=flags)


def _pick(n, prefs):
    for p in prefs:
        if n % p == 0:
            return p
    return n


def _rmsnorm_body(x_ref, g_ref, *o_refs, transposed):
    x = x_ref[...]
    y = x * lax.rsqrt(jnp.mean(x * x, axis=-1, keepdims=True) + RMS_EPS)
    ng = g_ref.shape[0]
    for i in range(ng):
        o_refs[i][...] = (y * g_ref[i:i + 1, :]).astype(o_refs[i].dtype)
    if transposed:
        o_refs[ng][...] = (y * g_ref[0:1, :]).T.astype(o_refs[ng].dtype)


def rmsnorm_cast(x, gains, transposed=False):
    m, d = x.shape
    g = gains.shape[0]
    tm = _pick(m, (256, 128, 64))
    out_specs = [pl.BlockSpec((tm, d), lambda i: (i, 0))] * g
    out_shape = [jax.ShapeDtypeStruct((m, d), BF16)] * g
    if transposed:
        out_specs = out_specs + [pl.BlockSpec((d, tm), lambda i: (0, i))]
        out_shape = out_shape + [jax.ShapeDtypeStruct((d, m), BF16)]
    return pl.pallas_call(
        functools.partial(_rmsnorm_body, transposed=transposed),
        grid=(m // tm,),
        in_specs=[pl.BlockSpec((tm, d), lambda i: (i, 0)), pl.BlockSpec((g, d), lambda i: (0, 0))],
        out_specs=out_specs,
        out_shape=out_shape,
        compiler_params=_params("parallel"),
        name="rmsnorm_cast",
    )(x, gains)


def _mm_body(x_ref, w_ref, o_ref):
    o_ref[...] = jnp.dot(x_ref[...], w_ref[...], preferred_element_type=F32).astype(o_ref.dtype)


def _mm_res_body(x_ref, w_ref, r_ref, o_ref):
    o_ref[...] = (r_ref[...] + jnp.dot(x_ref[...], w_ref[...], preferred_element_type=F32)).astype(o_ref.dtype)


def matmul(x, w, residual=None, out_dtype=F32, name="matmul"):
    m, k = x.shape
    n = w.shape[1]
    tm = _pick(m, (512, 256, 128, 64))
    tn = _pick(n, (512, 256, 128))
    in_specs = [pl.BlockSpec((tm, k), lambda j, i: (i, 0)), pl.BlockSpec((k, tn), lambda j, i: (0, j))]
    args = [x, w]
    body = _mm_body
    if residual is not None:
        in_specs.append(pl.BlockSpec((tm, tn), lambda j, i: (i, j)))
        args.append(residual)
        body = _mm_res_body
    return pl.pallas_call(
        body,
        grid=(n // tn, m // tm),
        in_specs=in_specs,
        out_specs=pl.BlockSpec((tm, tn), lambda j, i: (i, j)),
        out_shape=jax.ShapeDtypeStruct((m, n), out_dtype),
        compiler_params=_params("parallel", "parallel"),
        name=name,
    )(*args)


def _cumsum_rows(g):
    c = g.shape[0]
    row = lax.broadcasted_iota(jnp.int32, g.shape, 0)
    b = g
    s = 1
    while s < c:
        b = b + jnp.where(row >= s, pltpu.roll(b, s, axis=0), 0.0)
        s *= 2
    return b


def _log_sigmoid(x):
    return -(jnp.maximum(-x, 0.0) + jnp.log1p(jnp.exp(-jnp.abs(x))))


def _gla_body(q_ref, k_ref, v_ref, r_ref, z_ref, wg_ref, bg_ref, gn_ref, s0_ref, og_ref, sf_ref, st_ref,
              *, chunk, nsub, q_scale, inv_tau):
    t = pl.program_id(2)

    @pl.when(t == 0)
    def _():
        st_ref[...] = s0_ref[0, 0].T

    za = jnp.dot(z_ref[0].astype(BF16), wg_ref[...], preferred_element_type=F32) + bg_ref[...]
    log_a = _log_sigmoid(za) * inv_tau
    tril = (lax.broadcasted_iota(jnp.int32, (chunk, chunk), 0) >= lax.broadcasted_iota(jnp.int32, (chunk, chunk), 1))
    nt = (((1,), (1,)), ((), ()))
    tn = (((0,), (0,)), ((), ()))
    for i in range(nsub):
        sl = slice(i * chunk, (i + 1) * chunk)
        b = _cumsum_rows(log_a[sl])
        b_last = b[chunk - 1:chunk, :]
        q = q_ref[0, sl, :] * q_scale
        k = k_ref[0, sl, :]
        v = v_ref[0, sl, :].astype(BF16)
        qe = (q * jnp.exp(b)).astype(BF16)
        ke = (k * jnp.exp(-b)).astype(BF16)
        kd = (k * jnp.exp(b_last - b)).astype(BF16)
        att = lax.dot_general(qe, ke, nt, preferred_element_type=F32)
        att = jnp.where(tril, att, 0.0).astype(BF16)
        st = st_ref[...]
        o = lax.dot_general(qe, st.astype(BF16), nt, preferred_element_type=F32)
        o = o + jnp.dot(att, v, preferred_element_type=F32)
        st_ref[...] = st * jnp.exp(b_last) + lax.dot_general(v, kd, tn, preferred_element_type=F32)
        on = o * lax.rsqrt(jnp.mean(o * o, axis=-1, keepdims=True) + RMS_EPS) * gn_ref[...]
        r = r_ref[0, sl, :]
        og_ref[0, sl, :] = (on * (r * jax.nn.sigmoid(r))).astype(og_ref.dtype)

    @pl.when(t == pl.num_programs(2) - 1)
    def _():
        sf_ref[0, 0] = st_ref[...].T


def gla(proj, z, wg, bg, gn, s0, cfg):
    bsz, seq, _ = proj.shape
    h, dk, dv = cfg.gla_heads, cfg.gla_dk, cfg.gla_dv
    chunk = math.gcd(seq, cfg.gla_chunk)
    rows = _pick(seq, (4 * chunk, 2 * chunk, chunk))
    nsub = rows // chunk
    kq, kk, kv, kr = 0, h, (2 * h * dk) // dv, (2 * h * dk) // dv + h
    body = functools.partial(_gla_body, chunk=chunk, nsub=nsub, q_scale=dk ** -0.5, inv_tau=1.0 / cfg.gla_gate_tau)
    return pl.pallas_call(
        body,
        grid=(bsz, h, seq // rows),
        in_specs=[
            pl.BlockSpec((1, rows, dk), lambda b, hh, t: (b, t, kq + hh)),
            pl.BlockSpec((1, rows, dk), lambda b, hh, t: (b, t, kk + hh)),
            pl.BlockSpec((1, rows, dv), lambda b, hh, t: (b, t, kv + hh)),
            pl.BlockSpec((1, rows, dv), lambda b, hh, t: (b, t, kr + hh)),
            pl.BlockSpec((1, rows, LANES), lambda b, hh, t: (b, t, 0)),
            pl.BlockSpec((LANES, dk), lambda b, hh, t: (0, hh)),
            pl.BlockSpec((1, dk), lambda b, hh, t: (0, hh)),
            pl.BlockSpec((1, dv), lambda b, hh, t: (0, hh)),
            pl.BlockSpec((1, 1, dk, dv), lambda b, hh, t: (b, hh, 0, 0)),
        ],
        out_specs=[
            pl.BlockSpec((1, rows, dv), lambda b, hh, t: (b, t, hh)),
            pl.BlockSpec((1, 1, dk, dv), lambda b, hh, t: (b, hh, 0, 0)),
        ],
        out_shape=[
            jax.ShapeDtypeStruct((bsz, seq, h * dv), BF16),
            jax.ShapeDtypeStruct((bsz, h, dk, dv), F32),
        ],
        scratch_shapes=[pltpu.VMEM((dv, dk), F32)],
        compiler_params=_params("parallel", "parallel", "arbitrary"),
        name="gla",
    )(proj, proj, proj, proj, z, wg, bg, gn, s0)


def _headrope_body(x_ref, g_ref, cos_ref, sin_ref, *o_refs, nh, hd):
    for hh in range(nh):
        x = x_ref[:, hh * hd:(hh + 1) * hd]
        y = x * lax.rsqrt(jnp.mean(x * x, axis=-1, keepdims=True) + RMS_EPS) * g_ref[...]
        out = y * cos_ref[...] + pltpu.roll(y, hd // 2, axis=1) * sin_ref[...]
        for o_ref in o_refs:
            o_ref[:, hh * hd:(hh + 1) * hd] = out.astype(o_ref.dtype)


def head_norm_rope(x, gain, cos, sin, seq, out_dtypes, hd):
    n, width = x.shape
    nh = width // hd
    tm = _pick(seq, (256, 128, 64, 32, 16, 8))
    per = seq // tm
    return pl.pallas_call(
        functools.partial(_headrope_body, nh=nh, hd=hd),
        grid=(n // tm,),
        in_specs=[
            pl.BlockSpec((tm, width), lambda i: (i, 0)),
            pl.BlockSpec((1, hd), lambda i: (0, 0)),
            pl.BlockSpec((tm, hd), lambda i: (i % per, 0)),
            pl.BlockSpec((tm, hd), lambda i: (i % per, 0)),
        ],
        out_specs=[pl.BlockSpec((tm, width), lambda i: (i, 0)) for _ in out_dtypes],
        out_shape=[jax.ShapeDtypeStruct((n, width), dt) for dt in out_dtypes],
        compiler_params=_params("parallel"),
        name="head_norm_rope",
    )(x, gain, cos, sin)


def _rope_tables(pos0, seq, hd):
    half = hd // 2
    inv = ROPE_THETA ** (-np.arange(half, dtype=np.float64) / half)
    ang = (pos0 + np.arange(seq, dtype=np.float64))[:, None] * inv[None, :]
    cos, sin = np.cos(ang), np.sin(ang)
    return (jnp.asarray(np.concatenate([cos, cos], axis=1), F32),
            jnp.asarray(np.concatenate([-sin, sin], axis=1), F32))


def _top_values(s, k):
    vals = []
    cur = s
    for i in range(k):
        m = jnp.max(cur, axis=0, keepdims=True)
        vals.append(m)
        if i + 1 < k:
            cur = jnp.where(cur == m, NEG_INF, cur)
    return vals


ROW_TAU, ROW_M1, ROW_M2, ROW_INVZ = range(4)


def _peer_route_body(q_ref, keys_ref, s1_ref, s2_ref, rows_ref, *, heads, nkeys, half, topk):
    nt = (((1,), (1,)), ((), ()))
    tm = q_ref.shape[0]
    sub = 8
    row = lax.broadcasted_iota(jnp.int32, (sub, tm), 0)
    for hh in range(heads):
        st = []
        for p in range(2):
            g = 2 * hh + p
            qg = q_ref[:, g * half:(g + 1) * half].astype(BF16)
            st.append(lax.dot_general(keys_ref[g], qg, nt, preferred_element_type=F32))
        s1, s2 = st
        v1 = _top_values(s1, topk)
        v2 = _top_values(s2, topk)
        v2g = []
        for g0 in range(0, topk, sub):
            grp = jnp.full((sub, tm), NEG_INF, F32)
            for b in range(g0, min(g0 + sub, topk)):
                grp = jnp.where(row == b - g0, v2[b], grp)
            v2g.append(grp)
        cands = []
        for a in range(topk):
            bmax = topk // (a + 1)
            for gi, grp in enumerate(v2g):
                if gi * sub < bmax:
                    cands.append(jnp.where(row < bmax - gi * sub, v1[a] + grp, NEG_INF))
        cur = cands
        tau = None
        for i in range(topk):
            tau = functools.reduce(jnp.maximum, [jnp.max(c, axis=0, keepdims=True) for c in cur])
            if i + 1 < topk:
                cur = [jnp.where(c == tau, NEG_INF, c) for c in cur]
        m1, m2 = v1[0], v2[0]
        mx = m1 + m2
        z = functools.reduce(
            jnp.add, [jnp.sum(jnp.where(c >= tau, jnp.exp(c - mx), 0.0), axis=0, keepdims=True) for c in cands])
        s1_ref[hh] = s1
        s2_ref[hh] = s2
        rows_ref[ROW_TAU, hh:hh + 1, :] = tau
        rows_ref[ROW_M1, hh:hh + 1, :] = m1
        rows_ref[ROW_M2, hh:hh + 1, :] = m2
        rows_ref[ROW_INVZ, hh:hh + 1, :] = 1.0 / z


def peer_route(q, keys, cfg):
    n = q.shape[0]
    heads, nkeys, half = cfg.peer_heads, cfg.peer_n_keys, cfg.peer_key_dim // 2
    tm = _pick(n, (256, 128))
    tab = jax.ShapeDtypeStruct((heads, nkeys, n), F32)
    tab_spec = pl.BlockSpec((heads, nkeys, tm), lambda i: (0, 0, i))
    return pl.pallas_call(
        functools.partial(_peer_route_body, heads=heads, nkeys=nkeys, half=half, topk=cfg.peer_topk),
        grid=(n // tm,),
        in_specs=[pl.BlockSpec((tm, q.shape[1]), lambda i: (i, 0)),
                  pl.BlockSpec(keys.shape, lambda i: (0, 0, 0))],
        out_specs=[tab_spec, tab_spec, pl.BlockSpec((4, heads, tm), lambda i: (0, 0, i))],
        out_shape=[tab, tab, jax.ShapeDtypeStruct((4, heads, n), F32)],
        compiler_params=_params("parallel"),
        name="peer_route",
    )(q, keys)


def _gelu(x):
    return 0.5 * x * (1.0 + lax.erf(x * (1.0 / math.sqrt(2.0))))


SUBLANES = 8
GATE_LANES = 512


def _peer_row_tables(bc_ref, jt, s1_ref, rows_ref, *, heads, nsub):
    tm = bc_ref.shape[-1]
    for r in range(nsub):
        i1 = jt * nsub + r
        for hh in range(heads):
            s1row = s1_ref[hh, pl.ds(i1, 1), :]
            crow = jnp.exp(s1row - rows_ref[ROW_M1, hh:hh + 1, :]) * rows_ref[ROW_INVZ, hh:hh + 1, :]
            bc_ref[hh, r, 0] = jnp.broadcast_to(s1row, (SUBLANES, tm))
            bc_ref[hh, r, 1] = jnp.broadcast_to(crow, (SUBLANES, tm))


def _peer_gate_passes(dst_ref, s2_ref, e2_ref, taub_ref, bc_ref, *, heads, nkeys, nsub):
    tm = dst_ref.shape[-1]
    tw = min(tm, GATE_LANES)

    def one(r, c, l):
        rows, cols = slice(c, c + SUBLANES), slice(l, l + tw)
        w = None
        for hh in range(heads):
            hit = (s2_ref[hh, rows, cols] + bc_ref[hh, r, 0, :, cols]) >= taub_ref[hh, :, cols]
            term = jnp.where(hit, e2_ref[hh, rows, cols], 0.0) * bc_ref[hh, r, 1, :, cols]
            w = term if w is None else w + term
        dst_ref[r * nkeys + c:r * nkeys + c + SUBLANES, cols] = w

    return [functools.partial(one, r, c, l)
            for r in range(nsub) for c in range(0, nkeys, SUBLANES) for l in range(0, tm, tw)]


def _peer_dense_body(xt_ref, u_ref, v_ref, s1_ref, s2_ref, rows_ref, h_ref, o_ref, e2_ref, taub_ref, bc_ref,
                     gate_a_ref, gate_b_ref, ht_ref, *, heads, nkeys, nsub):
    j = pl.program_id(1)
    last = pl.num_programs(1) - 1
    row_tables = functools.partial(_peer_row_tables, bc_ref, s1_ref=s1_ref, rows_ref=rows_ref, heads=heads, nsub=nsub)
    gate_passes = functools.partial(_peer_gate_passes, s2_ref=s2_ref, e2_ref=e2_ref, taub_ref=taub_ref, bc_ref=bc_ref,
                                    heads=heads, nkeys=nkeys, nsub=nsub)
    tm = xt_ref.shape[1]
    d = o_ref.shape[1]

    @pl.when(j == 0)
    def _():
        o_ref[...] = h_ref[...]
        for hh in range(heads):
            e2_ref[hh] = jnp.exp(s2_ref[hh] - rows_ref[ROW_M2, hh:hh + 1, :])
            taub_ref[hh] = jnp.broadcast_to(rows_ref[ROW_TAU, hh:hh + 1, :], (SUBLANES, tm))
        row_tables(0)
        for run in gate_passes(gate_a_ref):
            run()

    def step(cur_ref, nxt_ref):
        row_tables(jnp.minimum(j + 1, last))
        passes = gate_passes(nxt_ref)
        tw = min(tm, MXU_COLS)
        n_tok, n_out = tm // tw, d // MXU_COLS
        n_first = n_tok * nsub
        first = len(passes) // 2
        quota = ([first // n_first + (i < first % n_first) for i in range(n_first)]
                 + [(len(passes) - first) // n_out + (i < (len(passes) - first) % n_out) for i in range(n_out)])
        it = iter(passes)
        for c in range(n_tok):
            cols = slice(c * tw, (c + 1) * tw)
            for r in range(nsub):
                rows = slice(r * nkeys, (r + 1) * nkeys)
                act = _gelu(jnp.dot(u_ref[rows, :], xt_ref[:, cols], preferred_element_type=F32))
                ht_ref[cols, rows] = (cur_ref[rows, cols] * act).T.astype(BF16)
                for _ in range(quota[c * nsub + r]):
                    next(it)()
        for p in range(n_out):
            cols = slice(p * MXU_COLS, (p + 1) * MXU_COLS)
            o_ref[:, cols] += jnp.dot(ht_ref[...], v_ref[:, cols], preferred_element_type=F32)
            for _ in range(quota[n_first + p]):
                next(it)()

    @pl.when(j % 2 == 0)
    def _():
        step(gate_a_ref, gate_b_ref)

    @pl.when(j % 2 == 1)
    def _():
        step(gate_b_ref, gate_a_ref)


def peer_dense(xt, u, v, s1, s2, rows, h, cfg):
    d, n = xt.shape
    e = u.shape[0]
    heads, nkeys = cfg.peer_heads, cfg.peer_n_keys
    tm = _pick(n, (512, 256, 128))
    nsub = 4
    te = nsub * nkeys
    once = dict(pipeline_mode=pl.Buffered(1))
    tab_spec = pl.BlockSpec((heads, nkeys, tm), lambda i, j: (0, 0, i), **once)
    return pl.pallas_call(
        functools.partial(_peer_dense_body, heads=heads, nkeys=nkeys, nsub=nsub),
        grid=(n // tm, e // te),
        in_specs=[
            pl.BlockSpec((d, tm), lambda i, j: (0, i), **once),
            pl.BlockSpec((te, d), lambda i, j: (j, 0)),
            pl.BlockSpec((te, d), lambda i, j: (j, 0)),
            tab_spec, tab_spec,
            pl.BlockSpec((4, heads, tm), lambda i, j: (0, 0, i), **once),
            pl.BlockSpec((tm, d), lambda i, j: (i, 0), **once),
        ],
        out_specs=pl.BlockSpec((tm, d), lambda i, j: (i, 0)),
        out_shape=jax.ShapeDtypeStruct((n, d), F32),
        scratch_shapes=[
            pltpu.VMEM((heads, nkeys, tm), F32),
            pltpu.VMEM((heads, SUBLANES, tm), F32),
            pltpu.VMEM((heads, nsub, 2, SUBLANES, tm), F32),
            pltpu.VMEM((te, tm), F32),
            pltpu.VMEM((te, tm), F32),
            pltpu.VMEM((tm, te), BF16),
        ],
        compiler_params=_params("parallel", "arbitrary"),
        name="peer_dense",
    )(xt, u, v, s1, s2, rows, h)


def peer_block(h, g_ffn, w_query, keys, u, v, cfg):
    n = h.shape[0]
    npad = -(-n // LANES) * LANES
    hp = h if npad == n else jnp.pad(h, ((0, npad - n), (0, 0)))
    xn, xt = rmsnorm_cast(hp, g_ffn[None, :], transposed=True)
    q = matmul(xn, w_query, name="peer_query")
    s1, s2, rows = peer_route(q, keys, cfg)
    return peer_dense(xt, u, v, s1, s2, rows, hp, cfg)[:n]


def _moba_prompt_body(q_ref, k_ref, v_ref, o_ref, means_ref, sel_ref, m_ref, l_ref, acc_ref,
                      *, blk, nblk, group, hd, topk, scale):
    qb = pl.program_id(2)
    nt = (((1,), (1,)), ((), ()))
    tn = (((0,), (0,)), ((), ()))
    rows = group * blk

    @pl.when(qb == 0)
    def _():
        for n in range(nblk):
            means_ref[n:n + 1, :] = jnp.mean(k_ref[0, n * blk:(n + 1) * blk, :], axis=0, keepdims=True)

    q4 = jnp.concatenate([q_ref[0, :, g * hd:(g + 1) * hd] for g in range(group)], axis=0)

    gate = lax.dot_general(means_ref[...].astype(BF16), q4, nt, preferred_element_type=F32)
    bidx = lax.broadcasted_iota(jnp.int32, gate.shape, 0)
    cand = bidx < qb
    gate = jnp.where(cand, gate, NEG_INF)
    rank = jnp.zeros(gate.shape, F32)
    for mm in range(nblk):
        gm = gate[mm:mm + 1, :]
        beats = (gm > gate) | ((gm == gate) & (mm < bidx))
        rank = rank + jnp.where(beats, 1.0, 0.0)
    sel_ref[...] = jnp.where((rank < topk) & cand, 1.0, 0.0)

    kpos = lax.broadcasted_iota(jnp.int32, (blk, rows), 0)
    qpos = lax.broadcasted_iota(jnp.int32, (blk, rows), 1) % blk
    k_own = k_ref[0, pl.ds(qb * blk, blk), :].astype(BF16)
    v_own = v_ref[0, pl.ds(qb * blk, blk), :].astype(BF16)
    s = lax.dot_general(k_own, q4, nt, preferred_element_type=F32) * scale
    s = jnp.where(kpos <= qpos, s, NEG_INF)
    m0 = jnp.max(s, axis=0, keepdims=True)
    p = jnp.exp(s - m0)
    m_ref[...] = m0
    l_ref[...] = jnp.sum(p, axis=0, keepdims=True)
    acc_ref[...] = lax.dot_general(v_own, p.astype(BF16), tn, preferred_element_type=F32)

    def past(n, carry):
        kb = k_ref[0, pl.ds(n * blk, blk), :].astype(BF16)
        vb = v_ref[0, pl.ds(n * blk, blk), :].astype(BF16)
        on = sel_ref[pl.ds(n, 1), :] > 0.5
        sb = lax.dot_general(kb, q4, nt, preferred_element_type=F32) * scale
        sb = jnp.where(on, sb, NEG_INF)
        m_old = m_ref[...]
        m_new = jnp.maximum(m_old, jnp.max(sb, axis=0, keepdims=True))
        pb = jnp.where(on, jnp.exp(sb - m_new), 0.0)
        alpha = jnp.exp(m_old - m_new)
        m_ref[...] = m_new
        l_ref[...] = alpha * l_ref[...] + jnp.sum(pb, axis=0, keepdims=True)
        acc_ref[...] = alpha * acc_ref[...] + lax.dot_general(vb, pb.astype(BF16), tn, preferred_element_type=F32)
        return carry

    lax.fori_loop(0, qb, past, 0)

    out = (acc_ref[...] / l_ref[...]).T
    for g in range(group):
        o_ref[0, :, g * hd:(g + 1) * hd] = out[g * blk:(g + 1) * blk, :].astype(o_ref.dtype)


def moba_prompt(q, k, v, cfg):
    bsz, seq, _ = q.shape
    blk, hd, group, kvh = cfg.moba_block, cfg.head_dim, cfg.group, cfg.n_kv_heads
    assert seq % blk == 0
    nblk = seq // blk
    rows = group * blk
    nsel = -(-nblk // 8) * 8
    body = functools.partial(_moba_prompt_body, blk=blk, nblk=nblk, group=group, hd=hd, topk=cfg.moba_topk,
                             scale=hd ** -0.5)
    return pl.pallas_call(
        body,
        grid=(bsz, kvh, nblk),
        in_specs=[
            pl.BlockSpec((1, blk, group * hd), lambda b, kh, i: (b, i, kh)),
            pl.BlockSpec((1, seq, hd), lambda b, kh, i: (b, 0, kh)),
            pl.BlockSpec((1, seq, hd), lambda b, kh, i: (b, 0, kh)),
        ],
        out_specs=pl.BlockSpec((1, blk, group * hd), lambda b, kh, i: (b, i, kh)),
        out_shape=jax.ShapeDtypeStruct(q.shape, BF16),
        scratch_shapes=[
            pltpu.VMEM((nblk, hd), F32),
            pltpu.VMEM((nblk, rows), F32),
            pltpu.VMEM((1, rows), F32),
            pltpu.VMEM((1, rows), F32),
            pltpu.VMEM((hd, rows), F32),
        ],
        compiler_params=_params("parallel", "parallel", "arbitrary"),
        name="moba_prompt",
    )(q, k, v)


def _block_means_body(pt_ref, *refs, inv_rows):
    k_refs, o_ref = refs[:-1], refs[-1]
    o_ref[0, 0] = functools.reduce(jnp.add, [jnp.sum(k_ref[0], axis=0) for k_ref in k_refs]) * inv_rows


def _page_specs(ppb, ps, kvh, hd):
    return [pl.BlockSpec((1, ps, kvh, hd), functools.partial(lambda p, b, n, pt: (pt[b, n * ppb + p], 0, 0, 0), p))
            for p in range(ppb)]


def block_means(cache_k, page_table, n_full, ppb, cfg):
    db = page_table.shape[0]
    _, ps, kvh, hd = cache_k.shape
    return pl.pallas_call(
        functools.partial(_block_means_body, inv_rows=1.0 / (ps * ppb)),
        grid_spec=pltpu.PrefetchScalarGridSpec(
            num_scalar_prefetch=1,
            grid=(db, n_full),
            in_specs=_page_specs(ppb, ps, kvh, hd),
            out_specs=pl.BlockSpec((1, 1, kvh, hd), lambda b, n, pt: (b, n, 0, 0)),
        ),
        out_shape=jax.ShapeDtypeStruct((db, n_full, kvh, hd), F32),
        compiler_params=_params("parallel", "parallel"),
        name="block_means",
    )(page_table, *([cache_k] * ppb))


def _moba_select_body(q_ref, means_ref, sel_ref, *, topk, kvh):
    nt = (((1,), (1,)), ((), ()))
    gate = jnp.concatenate(
        [lax.dot_general(q_ref[0, kh], means_ref[0, :, kh, :].astype(BF16), nt, preferred_element_type=F32)
         for kh in range(kvh)], axis=0)
    lane = lax.broadcasted_iota(jnp.int32, gate.shape, 1)
    nb = gate.shape[1]
    sel = jnp.zeros(gate.shape, F32)
    for _ in range(topk):
        m = jnp.max(gate, axis=1, keepdims=True)
        first = jnp.min(jnp.where(gate == m, lane, nb), axis=1, keepdims=True)
        pick = lane == first
        sel = jnp.where(pick, 1.0, sel)
        gate = jnp.where(pick, -3.0e38, gate)
    sel_ref[0] = sel


def moba_select(q4, means, cfg):
    db, kvh, rpk, hd = q4.shape
    n_full = means.shape[1]
    rows = kvh * rpk
    return pl.pallas_call(
        functools.partial(_moba_select_body, topk=min(cfg.moba_topk, n_full), kvh=kvh),
        grid=(db,),
        in_specs=[pl.BlockSpec((1, kvh, rpk, hd), lambda b: (b, 0, 0, 0)),
                  pl.BlockSpec((1, n_full, kvh, hd), lambda b: (b, 0, 0, 0))],
        out_specs=pl.BlockSpec((1, rows, n_full), lambda b: (b, 0, 0)),
        out_shape=jax.ShapeDtypeStruct((db, rows, n_full), F32),
        compiler_params=_params("parallel"),
        name="moba_select",
    )(q4, means)


def _moba_sample_body(pt_ref, *refs, ppb, scale, t_new, kvh, rpk):
    q_ref, sel_ref, kn_ref, vn_ref = refs[0], refs[1 + 2 * ppb], refs[2 + 2 * ppb], refs[3 + 2 * ppb]
    k_refs, v_refs = refs[1:1 + ppb], refs[1 + ppb:1 + 2 * ppb]
    o_ref, own_ref, m_ref, l_ref, acc_ref = refs[4 + 2 * ppb:]
    n = pl.program_id(1)
    nt = (((1,), (1,)), ((), ()))
    q = q_ref[0]
    rows, page_rows = q.shape[0], kn_ref.shape[1]

    @pl.when(n == 0)
    def _():
        m_ref[...] = jnp.full(m_ref.shape, NEG_INF, F32)
        l_ref[...] = jnp.zeros(l_ref.shape, F32)
        acc_ref[...] = jnp.zeros(acc_ref.shape, F32)
        shape = (rows, ppb * page_rows)
        same = (lax.broadcasted_iota(jnp.int32, shape, 1) % kvh) == (lax.broadcasted_iota(jnp.int32, shape, 0) // rpk)
        own_ref[...] = jnp.where(same, 1.0, 0.0)

    def absorb(k_pages, v_pages, on):
        kb = jnp.concatenate([kp[0].astype(BF16) for kp in k_pages], axis=0)
        vb = jnp.concatenate([vp[0].astype(BF16) for vp in v_pages], axis=0)
        s = lax.dot_general(q, kb, nt, preferred_element_type=F32) * scale
        s = jnp.where(on, s, NEG_INF)
        m_old = m_ref[...]
        m_new = jnp.maximum(m_old, jnp.max(s, axis=1, keepdims=True))
        p = jnp.where(on, jnp.exp(s - m_new), 0.0)
        alpha = jnp.exp(m_old - m_new)
        m_ref[...] = m_new
        l_ref[...] = alpha * l_ref[...] + jnp.sum(p, axis=1, keepdims=True)
        acc_ref[...] = alpha * acc_ref[...] + jnp.dot(p.astype(BF16), vb, preferred_element_type=F32)

    absorb(k_refs, v_refs, (own_ref[...] * sel_ref[0, 0]) > 0.5)

    @pl.when(n == pl.num_programs(1) - 1)
    def _():
        shape = (rows, page_rows)
        tq = lax.broadcasted_iota(jnp.int32, shape, 0) % t_new
        tk = lax.broadcasted_iota(jnp.int32, shape, 1) // kvh
        absorb([kn_ref], [vn_ref], (own_ref[:, :page_rows] > 0.5) & (tk <= tq))
        o_ref[0] = acc_ref[...] / l_ref[...]


def moba_sample(q, cache_k, cache_v, page_table, sel, k_new, v_new, t_new, ppb, cfg):
    db, rows, hd = q.shape
    kvh = cfg.n_kv_heads
    n_full = sel.shape[1]
    page_rows = cache_k.shape[1]
    body = functools.partial(_moba_sample_body, ppb=ppb, scale=hd ** -0.5, t_new=t_new, kvh=kvh, rpk=rows // kvh)
    page_specs = [
        pl.BlockSpec((1, page_rows, hd), functools.partial(lambda p, b, n, pt: (pt[b, n * ppb + p], 0, 0), p))
        for p in range(ppb)]
    new_spec = pl.BlockSpec((1, page_rows, hd), lambda b, n, pt: (b, 0, 0))
    return pl.pallas_call(
        body,
        grid_spec=pltpu.PrefetchScalarGridSpec(
            num_scalar_prefetch=1,
            grid=(db, n_full),
            in_specs=([pl.BlockSpec((1, rows, hd), lambda b, n, pt: (b, 0, 0))] + page_specs + page_specs
                      + [pl.BlockSpec((1, 1, rows, 1), lambda b, n, pt: (b, n, 0, 0)), new_spec, new_spec]),
            out_specs=pl.BlockSpec((1, rows, hd), lambda b, n, pt: (b, 0, 0)),
            scratch_shapes=[
                pltpu.VMEM((rows, ppb * page_rows), F32),
                pltpu.VMEM((rows, 1), F32),
                pltpu.VMEM((rows, 1), F32),
                pltpu.VMEM((rows, hd), F32),
            ],
        ),
        out_shape=jax.ShapeDtypeStruct((db, rows, hd), F32),
        compiler_params=_params("parallel", "arbitrary"),
        name="moba_sample",
    )(page_table, q, *([cache_k] * ppb), *([cache_v] * ppb), sel, k_new, v_new)


def _cast_body(x_ref, o_ref):
    o_ref[...] = x_ref[0].astype(o_ref.dtype)


def cast_layer(w, layer, cols=None):
    _, r, c = w.shape
    cols = c if cols is None else cols
    tr = next(t for t in (512, 256, 128, 64, 32, 16) if r % t == 0 and t * cols * 4 <= 8 * 1024 * 1024)
    return pl.pallas_call(
        _cast_body,
        grid=(r // tr,),
        in_specs=[pl.BlockSpec((1, tr, cols), lambda i: (layer, i, 0))],
        out_specs=pl.BlockSpec((tr, cols), lambda i: (i, 0)),
        out_shape=jax.ShapeDtypeStruct((r, cols), BF16),
        compiler_params=_params("parallel"),
        name="cast_bf16",
    )(w)


def _cast_tail_body(x_ref, o_ref, *, valid):
    lane = lax.broadcasted_iota(jnp.int32, o_ref.shape, 1)
    o_ref[...] = jnp.where(lane < valid, x_ref[0], 0.0).astype(o_ref.dtype)


def cast_tail(w, layer, col0):
    _, r, c = w.shape
    assert col0 % LANES == 0 and 0 < c - col0 <= LANES
    tr = _pick(r, (512, 256, 128, 64))
    return pl.pallas_call(
        functools.partial(_cast_tail_body, valid=c - col0),
        grid=(r // tr,),
        in_specs=[pl.BlockSpec((1, tr, LANES), lambda i: (layer, i, col0 // LANES))],
        out_specs=pl.BlockSpec((tr, LANES), lambda i: (i, 0)),
        out_shape=jax.ShapeDtypeStruct((r, LANES), BF16),
        compiler_params=_params("parallel"),
        name="cast_tail_bf16",
    )(w)


class Weights(NamedTuple):
    norm_mix: jax.Array
    norm_ffn: jax.Array
    w_in: jax.Array
    w_z: jax.Array
    w_gate2: jax.Array
    b_gate: jax.Array
    gla_norm: jax.Array
    w_out: jax.Array
    norm_kv: jax.Array
    w_kv: jax.Array
    k_norm: jax.Array
    w_q: jax.Array
    q_norm: jax.Array
    w_o: jax.Array
    peer_w_query: tuple
    peer_keys: tuple
    peer_u: tuple
    peer_v: tuple


def _prep_weights(norm_mix, norm_ffn, gla_w_in, gla_w_gate2, gla_b_gate, gla_norm, gla_w_out, norm_kv, w_kv,
                  k_norm, w_q, q_norm, w_o, peer_w_query, peer_sub_keys, peer_u, peer_v, cfg):
    main = 2 * cfg.gla_dk_tot + 2 * cfg.d_model
    rank = cfg.gla_gate_rank
    depth = norm_mix.shape[0]
    return Weights(
        norm_mix=norm_mix,
        norm_ffn=norm_ffn,
        w_in=cast_layer(gla_w_in, 0, cols=main),
        w_z=cast_tail(gla_w_in, 0, main),
        w_gate2=jnp.pad(gla_w_gate2[0], ((0, LANES - rank), (0, 0))).astype(BF16),
        b_gate=gla_b_gate[0][None, :],
        gla_norm=gla_norm[0].reshape(1, -1),
        w_out=gla_w_out[0].astype(BF16),
        norm_kv=norm_kv,
        w_kv=w_kv.astype(BF16),
        k_norm=k_norm[None, :],
        w_q=w_q[0].astype(BF16),
        q_norm=q_norm[0][None, :],
        w_o=w_o[0].astype(BF16),
        peer_w_query=tuple(peer_w_query[l].astype(BF16) for l in range(depth)),
        peer_keys=tuple(peer_sub_keys[l].reshape(2 * cfg.peer_heads, cfg.peer_n_keys, -1).astype(BF16)
                        for l in range(depth)),
        peer_u=tuple(cast_layer(peer_u, l) for l in range(depth)),
        peer_v=tuple(cast_layer(peer_v, l) for l in range(depth)),
    )


def _trunk(x, pos0, s0, attend, w, cfg):
    bsz, seq, d = x.shape
    n = bsz * seq
    hd = cfg.head_dim
    xf = x.reshape(n, d)

    (hn,) = rmsnorm_cast(xf, w.norm_mix[0:1])
    proj = matmul(hn, w.w_in, name="gla_in")
    z = matmul(hn, w.w_z, name="gla_gate_in")
    og, s_fin = gla(proj.reshape(bsz, seq, -1), z.reshape(bsz, seq, -1), w.w_gate2, w.b_gate, w.gla_norm, s0, cfg)
    h = matmul(og.reshape(n, -1), w.w_out, residual=xf, name="gla_out")
    h = peer_block(h, w.norm_ffn[0], w.peer_w_query[0], w.peer_keys[0], w.peer_u[0], w.peer_v[0], cfg)

    kvn, hn1 = rmsnorm_cast(h, jnp.stack([w.norm_kv, w.norm_mix[1]]))
    kv = matmul(kvn, w.w_kv, name="kv_proj")
    cos, sin = _rope_tables(pos0, seq, hd)
    (k,) = head_norm_rope(kv[:, :cfg.kv_dim], w.k_norm, cos, sin, seq, (F32,), hd)
    v = kv[:, cfg.kv_dim:]
    (q,) = head_norm_rope(matmul(hn1, w.w_q, name="q_proj"), w.q_norm, cos, sin, seq, (BF16,), hd)

    o = attend(q, k, v)
    h = matmul(o, w.w_o, residual=h, name="attn_out")
    h = peer_block(h, w.norm_ffn[1], w.peer_w_query[1], w.peer_keys[1], w.peer_u[1], w.peer_v[1], cfg)
    return (h.reshape(bsz, seq, d), s_fin[None],
            k.reshape(bsz, seq, cfg.n_kv_heads, hd), v.reshape(bsz, seq, cfg.n_kv_heads, hd))


def _attend_prompt(bsz, seq, cfg):
    def attend(q, k, v):
        n = q.shape[0]
        o = moba_prompt(q.reshape(bsz, seq, -1), k.reshape(bsz, seq, -1), v.reshape(bsz, seq, -1), cfg)
        return o.reshape(n, -1)
    return attend


def _attend_sample(db, t_new, cache_k, cache_v, page_table, past_len, cfg):
    hd, kvh, group, nh = cfg.head_dim, cfg.n_kv_heads, cfg.group, cfg.n_heads
    blk, ps = cfg.moba_block, cfg.page_size
    ppb = blk // ps
    n_full = past_len // blk
    assert past_len % blk == 0 and n_full > 0 and t_new <= ps
    pt = page_table[:, :n_full * ppb]
    ck = cache_k.reshape(cache_k.shape[0], ps * kvh, hd)
    cv = cache_v.reshape(cache_v.shape[0], ps * kvh, hd)

    def attend(q, k, v):
        qr = q.reshape(db, t_new, nh, hd).transpose(0, 2, 1, 3).reshape(db, nh * t_new, hd)
        means = block_means(cache_k, pt, n_full, ppb, cfg)
        sel = moba_select(qr.reshape(db, kvh, group * t_new, hd), means, cfg)
        sel = sel.transpose(0, 2, 1)[..., None]
        pad = ((0, 0), (0, ps - t_new), (0, 0), (0, 0))
        k_new = jnp.pad(k.reshape(db, t_new, kvh, hd), pad).reshape(db, ps * kvh, hd)
        v_new = jnp.pad(v.reshape(db, t_new, kvh, hd), pad).reshape(db, ps * kvh, hd)
        o = moba_sample(qr, ck, cv, pt, sel, k_new, v_new, t_new, ppb, cfg)
        o = o.reshape(db, nh, t_new, hd).transpose(0, 2, 1, 3).reshape(db * t_new, nh * hd)
        return o.astype(BF16)
    return attend


def _forward(x_prompt, x_sample, cache_k, cache_v, state_gla, page_table, weights, past_len, cfg):
    w = _prep_weights(*weights, cfg)
    bsz, seq, _ = x_prompt.shape
    db, t_new, _ = x_sample.shape
    s0_p = jnp.zeros((bsz, cfg.gla_heads, cfg.gla_dk, cfg.gla_dv), state_gla.dtype)
    y_p, st_p, k_p, v_p = _trunk(x_prompt, 0, s0_p, _attend_prompt(bsz, seq, cfg), w, cfg)
    attend_s = _attend_sample(db, t_new, cache_k, cache_v, page_table, past_len, cfg)
    y_s, st_s, k_s, v_s = _trunk(x_sample, past_len, state_gla[0], attend_s, w, cfg)
    return (y_p, y_s, st_p, st_s, k_p, v_p, k_s, v_s)


def kernel(x_prompt, x_sample, cache_k, cache_v, state_gla, page_table, norm_mix, norm_ffn, gla_w_in, gla_w_gate2, gla_b_gate, gla_norm, gla_w_out, norm_kv, w_kv, k_norm, w_q, q_norm, w_o, peer_w_query, peer_sub_keys, peer_u, peer_v):
    weights = (norm_mix, norm_ffn, gla_w_in, gla_w_gate2, gla_b_gate, gla_norm, gla_w_out, norm_kv, w_kv, k_norm,
               w_q, q_norm, w_o, peer_w_query, peer_sub_keys, peer_u, peer_v)
    past_len = page_table.shape[1] * CFG.page_size
    return _forward(x_prompt, x_sample, cache_k, cache_v, state_gla, page_table, weights, past_len, CFG)
```

```python
import functools
import math
from typing import NamedTuple

import numpy as np
import jax
import jax.numpy as jnp
from jax import lax
from jax.experimental import pallas as pl
from jax.experimental.pallas import tpu as pltpu

F32 = jnp.float32
BF16 = jnp.bfloat16

LANES = 128
MXU_COLS = 256
RMS_EPS = 1e-6
NEG_INF = -1e30
ROPE_THETA = 10000.0
VMEM_LIMIT_BYTES = 56 * 1024 * 1024


class Cfg(NamedTuple):
    d_model: int = 4096
    gla_heads: int = 4
    gla_gate_rank: int = 16
    gla_gate_tau: float = 16.0
    gla_chunk: int = 64
    head_dim: int = 128
    n_kv_heads: int = 8
    moba_block: int = 256
    moba_topk: int = 3
    page_size: int = 128
    peer_heads: int = 8
    peer_n_keys: int = 128
    peer_topk: int = 16
    peer_key_dim: int = 256

    @property
    def gla_dk_tot(self):
        return self.d_model // 2

    @property
    def gla_dk(self):
        return self.gla_dk_tot // self.gla_heads

    @property
    def gla_dv(self):
        return self.d_model // self.gla_heads

    @property
    def n_heads(self):
        return self.d_model // self.head_dim

    @property
    def group(self):
        return self.n_heads // self.n_kv_heads

    @property
    def kv_dim(self):
        return self.n_kv_heads * self.head_dim


CFG = Cfg()


def _params(*sem, flags=None):
    return pltpu.CompilerParams(dimension_semantics=sem, vmem_limit_bytes=VMEM_LIMIT_BYTES, flags=flags)


def _pick(n, prefs):
    for p in prefs:
        if n % p == 0:
            return p
    return n


def _rmsnorm_body(x_ref, g_ref, *o_refs, transposed):
    x = x_ref[...]
    y = x * lax.rsqrt(jnp.mean(x * x, axis=-1, keepdims=True) + RMS_EPS)
    ng = g_ref.shape[0]
    for i in range(ng):
        o_refs[i][...] = (y * g_ref[i:i + 1, :]).astype(o_refs[i].dtype)
    if transposed:
        o_refs[ng][...] = (y * g_ref[0:1, :]).T.astype(o_refs[ng].dtype)


def rmsnorm_cast(x, gains, transposed=False):
    m, d = x.shape
    g = gains.shape[0]
    tm = _pick(m, (256, 128, 64))
    out_specs = [pl.BlockSpec((tm, d), lambda i: (i, 0))] * g
    out_shape = [jax.ShapeDtypeStruct((m, d), BF16)] * g
    if transposed:
        out_specs = out_specs + [pl.BlockSpec((d, tm), lambda i: (0, i))]
        out_shape = out_shape + [jax.ShapeDtypeStruct((d, m), BF16)]
    return pl.pallas_call(
        functools.partial(_rmsnorm_body, transposed=transposed),
        grid=(m // tm,),
        in_specs=[pl.BlockSpec((tm, d), lambda i: (i, 0)), pl.BlockSpec((g, d), lambda i: (0, 0))],
        out_specs=out_specs,
        out_shape=out_shape,
        compiler_params=_params("parallel"),
        name="rmsnorm_cast",
    )(x, gains)


def _mm_body(x_ref, w_ref, o_ref):
    o_ref[...] = jnp.dot(x_ref[...], w_ref[...], preferred_element_type=F32).astype(o_ref.dtype)


def _mm_res_body(x_ref, w_ref, r_ref, o_ref):
    o_ref[...] = (r_ref[...] + jnp.dot(x_ref[...], w_ref[...], preferred_element_type=F32)).astype(o_ref.dtype)


def matmul(x, w, residual=None, out_dtype=F32, name="matmul"):
    m, k = x.shape
    n = w.shape[1]
    tm = _pick(m, (512, 256, 128, 64))
    tn = _pick(n, (512, 256, 128))
    in_specs = [pl.BlockSpec((tm, k), lambda j, i: (i, 0)), pl.BlockSpec((k, tn), lambda j, i: (0, j))]
    args = [x, w]
    body = _mm_body
    if residual is not None:
        in_specs.append(pl.BlockSpec((tm, tn), lambda j, i: (i, j)))
        args.append(residual)
        body = _mm_res_body
    return pl.pallas_call(
        body,
        grid=(n // tn, m // tm),
        in_specs=in_specs,
        out_specs=pl.BlockSpec((tm, tn), lambda j, i: (i, j)),
        out_shape=jax.ShapeDtypeStruct((m, n), out_dtype),
        compiler_params=_params("parallel", "parallel"),
        name=name,
    )(*args)


def _cumsum_rows(g):
    c = g.shape[0]
    row = lax.broadcasted_iota(jnp.int32, g.shape, 0)
    b = g
    s = 1
    while s < c:
        b = b + jnp.where(row >= s, pltpu.roll(b, s, axis=0), 0.0)
        s *= 2
    return b


def _log_sigmoid(x):
    return -(jnp.maximum(-x, 0.0) + jnp.log1p(jnp.exp(-jnp.abs(x))))


def _gla_body(q_ref, k_ref, v_ref, r_ref, z_ref, wg_ref, bg_ref, gn_ref, s0_ref, og_ref, sf_ref, st_ref,
              *, chunk, nsub, q_scale, inv_tau):
    t = pl.program_id(2)

    @pl.when(t == 0)
    def _():
        st_ref[...] = s0_ref[0, 0].T

    za = jnp.dot(z_ref[0].astype(BF16), wg_ref[...], preferred_element_type=F32) + bg_ref[...]
    log_a = _log_sigmoid(za) * inv_tau
    tril = (lax.broadcasted_iota(jnp.int32, (chunk, chunk), 0) >= lax.broadcasted_iota(jnp.int32, (chunk, chunk), 1))
    nt = (((1,), (1,)), ((), ()))
    tn = (((0,), (0,)), ((), ()))
    for i in range(nsub):
        sl = slice(i * chunk, (i + 1) * chunk)
        b = _cumsum_rows(log_a[sl])
        b_last = b[chunk - 1:chunk, :]
        q = q_ref[0, sl, :] * q_scale
        k = k_ref[0, sl, :]
        v = v_ref[0, sl, :].astype(BF16)
        qe = (q * jnp.exp(b)).astype(BF16)
        ke = (k * jnp.exp(-b)).astype(BF16)
        kd = (k * jnp.exp(b_last - b)).astype(BF16)
        att = lax.dot_general(qe, ke, nt, preferred_element_type=F32)
        att = jnp.where(tril, att, 0.0).astype(BF16)
        st = st_ref[...]
        o = lax.dot_general(qe, st.astype(BF16), nt, preferred_element_type=F32)
        o = o + jnp.dot(att, v, preferred_element_type=F32)
        st_ref[...] = st * jnp.exp(b_last) + lax.dot_general(v, kd, tn, preferred_element_type=F32)
        on = o * lax.rsqrt(jnp.mean(o * o, axis=-1, keepdims=True) + RMS_EPS) * gn_ref[...]
        r = r_ref[0, sl, :]
        og_ref[0, sl, :] = (on * (r * jax.nn.sigmoid(r))).astype(og_ref.dtype)

    @pl.when(t == pl.num_programs(2) - 1)
    def _():
        sf_ref[0, 0] = st_ref[...].T


def gla(proj, z, wg, bg, gn, s0, cfg):
    bsz, seq, _ = proj.shape
    h, dk, dv = cfg.gla_heads, cfg.gla_dk, cfg.gla_dv
    chunk = math.gcd(seq, cfg.gla_chunk)
    rows = _pick(seq, (4 * chunk, 2 * chunk, chunk))
    nsub = rows // chunk
    kq, kk, kv, kr = 0, h, (2 * h * dk) // dv, (2 * h * dk) // dv + h
    body = functools.partial(_gla_body, chunk=chunk, nsub=nsub, q_scale=dk ** -0.5, inv_tau=1.0 / cfg.gla_gate_tau)
    return pl.pallas_call(
        body,
        grid=(bsz, h, seq // rows),
        in_specs=[
            pl.BlockSpec((1, rows, dk), lambda b, hh, t: (b, t, kq + hh)),
            pl.BlockSpec((1, rows, dk), lambda b, hh, t: (b, t, kk + hh)),
            pl.BlockSpec((1, rows, dv), lambda b, hh, t: (b, t, kv + hh)),
            pl.BlockSpec((1, rows, dv), lambda b, hh, t: (b, t, kr + hh)),
            pl.BlockSpec((1, rows, LANES), lambda b, hh, t: (b, t, 0)),
            pl.BlockSpec((LANES, dk), lambda b, hh, t: (0, hh)),
            pl.BlockSpec((1, dk), lambda b, hh, t: (0, hh)),
            pl.BlockSpec((1, dv), lambda b, hh, t: (0, hh)),
            pl.BlockSpec((1, 1, dk, dv), lambda b, hh, t: (b, hh, 0, 0)),
        ],
        out_specs=[
            pl.BlockSpec((1, rows, dv), lambda b, hh, t: (b, t, hh)),
            pl.BlockSpec((1, 1, dk, dv), lambda b, hh, t: (b, hh, 0, 0)),
        ],
        out_shape=[
            jax.ShapeDtypeStruct((bsz, seq, h * dv), BF16),
            jax.ShapeDtypeStruct((bsz, h, dk, dv), F32),
        ],
        scratch_shapes=[pltpu.VMEM((dv, dk), F32)],
        compiler_params=_params("parallel", "parallel", "arbitrary"),
        name="gla",
    )(proj, proj, proj, proj, z, wg, bg, gn, s0)


def _headrope_body(x_ref, g_ref, cos_ref, sin_ref, *o_refs, nh, hd):
    for hh in range(nh):
        x = x_ref[:, hh * hd:(hh + 1) * hd]
        y = x * lax.rsqrt(jnp.mean(x * x, axis=-1, keepdims=True) + RMS_EPS) * g_ref[...]
        out = y * cos_ref[...] + pltpu.roll(y, hd // 2, axis=1) * sin_ref[...]
        for o_ref in o_refs:
            o_ref[:, hh * hd:(hh + 1) * hd] = out.astype(o_ref.dtype)


def head_norm_rope(x, gain, cos, sin, seq, out_dtypes, hd):
    n, width = x.shape
    nh = width // hd
    tm = _pick(seq, (256, 128, 64, 32, 16, 8))
    per = seq // tm
    return pl.pallas_call(
        functools.partial(_headrope_body, nh=nh, hd=hd),
        grid=(n // tm,),
        in_specs=[
            pl.BlockSpec((tm, width), lambda i: (i, 0)),
            pl.BlockSpec((1, hd), lambda i: (0, 0)),
            pl.BlockSpec((tm, hd), lambda i: (i % per, 0)),
            pl.BlockSpec((tm, hd), lambda i: (i % per, 0)),
        ],
        out_specs=[pl.BlockSpec((tm, width), lambda i: (i, 0)) for _ in out_dtypes],
        out_shape=[jax.ShapeDtypeStruct((n, width), dt) for dt in out_dtypes],
        compiler_params=_params("parallel"),
        name="head_norm_rope",
    )(x, gain, cos, sin)


def _rope_tables(pos0, seq, hd):
    half = hd // 2
    inv = ROPE_THETA ** (-np.arange(half, dtype=np.float64) / half)
    ang = (pos0 + np.arange(seq, dtype=np.float64))[:, None] * inv[None, :]
    cos, sin = np.cos(ang), np.sin(ang)
    return (jnp.asarray(np.concatenate([cos, cos], axis=1), F32),
            jnp.asarray(np.concatenate([-sin, sin], axis=1), F32))


def _top_values(s, k):
    vals = []
    cur = s
    for i in range(k):
        m = jnp.max(cur, axis=0, keepdims=True)
        vals.append(m)
        if i + 1 < k:
            cur = jnp.where(cur == m, NEG_INF, cur)
    return vals


ROW_TAU, ROW_M1, ROW_M2, ROW_INVZ = range(4)


def _peer_route_body(q_ref, keys_ref, s1_ref, s2_ref, rows_ref, *, heads, nkeys, half, topk):
    nt = (((1,), (1,)), ((), ()))
    tm = q_ref.shape[0]
    sub = 8
    row = lax.broadcasted_iota(jnp.int32, (sub, tm), 0)
    for hh in range(heads):
        st = []
        for p in range(2):
            g = 2 * hh + p
            qg = q_ref[:, g * half:(g + 1) * half].astype(BF16)
            st.append(lax.dot_general(keys_ref[g], qg, nt, preferred_element_type=F32))
        s1, s2 = st
        v1 = _top_values(s1, topk)
        v2 = _top_values(s2, topk)
        v2g = []
        for g0 in range(0, topk, sub):
            grp = jnp.full((sub, tm), NEG_INF, F32)
            for b in range(g0, min(g0 + sub, topk)):
                grp = jnp.where(row == b - g0, v2[b], grp)
            v2g.append(grp)
        cands = []
        for a in range(topk):
            bmax = topk // (a + 1)
            for gi, grp in enumerate(v2g):
                if gi * sub < bmax:
                    cands.append(jnp.where(row < bmax - gi * sub, v1[a] + grp, NEG_INF))
        cur = cands
        tau = None
        for i in range(topk):
            tau = functools.reduce(jnp.maximum, [jnp.max(c, axis=0, keepdims=True) for c in cur])
            if i + 1 < topk:
                cur = [jnp.where(c == tau, NEG_INF, c) for c in cur]
        m1, m2 = v1[0], v2[0]
        mx = m1 + m2
        z = functools.reduce(
            jnp.add, [jnp.sum(jnp.where(c >= tau, jnp.exp(c - mx), 0.0), axis=0, keepdims=True) for c in cands])
        s1_ref[hh] = s1
        s2_ref[hh] = s2
        rows_ref[ROW_TAU, hh:hh + 1, :] = tau
        rows_ref[ROW_M1, hh:hh + 1, :] = m1
        rows_ref[ROW_M2, hh:hh + 1, :] = m2
        rows_ref[ROW_INVZ, hh:hh + 1, :] = 1.0 / z


def peer_route(q, keys, cfg):
    n = q.shape[0]
    heads, nkeys, half = cfg.peer_heads, cfg.peer_n_keys, cfg.peer_key_dim // 2
    tm = _pick(n, (256, 128))
    tab = jax.ShapeDtypeStruct((heads, nkeys, n), F32)
    tab_spec = pl.BlockSpec((heads, nkeys, tm), lambda i: (0, 0, i))
    return pl.pallas_call(
        functools.partial(_peer_route_body, heads=heads, nkeys=nkeys, half=half, topk=cfg.peer_topk),
        grid=(n // tm,),
        in_specs=[pl.BlockSpec((tm, q.shape[1]), lambda i: (i, 0)),
                  pl.BlockSpec(keys.shape, lambda i: (0, 0, 0))],
        out_specs=[tab_spec, tab_spec, pl.BlockSpec((4, heads, tm), lambda i: (0, 0, i))],
        out_shape=[tab, tab, jax.ShapeDtypeStruct((4, heads, n), F32)],
        compiler_params=_params("parallel"),
        name="peer_route",
    )(q, keys)


def _gelu(x):
    return 0.5 * x * (1.0 + lax.erf(x * (1.0 / math.sqrt(2.0))))


SUBLANES = 8
GATE_LANES = 512


def _peer_row_tables(bc_ref, jt, s1_ref, rows_ref, *, heads, nsub):
    tm = bc_ref.shape[-1]
    for r in range(nsub):
        i1 = jt * nsub + r
        for hh in range(heads):
            s1row = s1_ref[hh, pl.ds(i1, 1), :]
            crow = jnp.exp(s1row - rows_ref[ROW_M1, hh:hh + 1, :]) * rows_ref[ROW_INVZ, hh:hh + 1, :]
            bc_ref[hh, r, 0] = jnp.broadcast_to(s1row, (SUBLANES, tm))
            bc_ref[hh, r, 1] = jnp.broadcast_to(crow, (SUBLANES, tm))


def _peer_gate_passes(dst_ref, s2_ref, e2_ref, taub_ref, bc_ref, *, heads, nkeys, nsub):
    tm = dst_ref.shape[-1]
    tw = min(tm, GATE_LANES)

    def one(c, l):
        rows, cols = slice(c, c + SUBLANES), slice(l, l + tw)
        ws = [None] * nsub
        for hh in range(heads):
            s2t, e2t, tb = s2_ref[hh, rows, cols], e2_ref[hh, rows, cols], taub_ref[hh, :, cols]
            for r in range(nsub):
                hit = (s2t + bc_ref[hh, r, 0, :, cols]) >= tb
                term = jnp.where(hit, e2t, 0.0) * bc_ref[hh, r, 1, :, cols]
                ws[r] = term if ws[r] is None else ws[r] + term
        for r in range(nsub):
            dst_ref[r * nkeys + c:r * nkeys + c + SUBLANES, cols] = ws[r]

    return [functools.partial(one, c, l) for c in range(0, nkeys, SUBLANES) for l in range(0, tm, tw)]


def _peer_dense_body(xt_ref, u_ref, v_ref, s1_ref, s2_ref, rows_ref, o_ref, e2_ref, taub_ref, bc_ref,
                     gate_a_ref, gate_b_ref, ht_ref, *, heads, nkeys, nsub):
    j = pl.program_id(1)
    last = pl.num_programs(1) - 1
    row_tables = functools.partial(_peer_row_tables, bc_ref, s1_ref=s1_ref, rows_ref=rows_ref, heads=heads, nsub=nsub)
    gate_passes = functools.partial(_peer_gate_passes, s2_ref=s2_ref, e2_ref=e2_ref, taub_ref=taub_ref, bc_ref=bc_ref,
                                    heads=heads, nkeys=nkeys, nsub=nsub)
    tm = xt_ref.shape[1]
    d = o_ref.shape[1]

    @pl.when(j == 0)
    def _():
        o_ref[...] = jnp.zeros_like(o_ref)
        for hh in range(heads):
            e2_ref[hh] = jnp.exp(s2_ref[hh] - rows_ref[ROW_M2, hh:hh + 1, :])
            taub_ref[hh] = jnp.broadcast_to(rows_ref[ROW_TAU, hh:hh + 1, :], (SUBLANES, tm))
        row_tables(0)
        for run in gate_passes(gate_a_ref):
            run()

    def step(cur_ref, nxt_ref):
        row_tables(jnp.minimum(j + 1, last))
        passes = gate_passes(nxt_ref)
        tw = min(tm, MXU_COLS)
        n_tok, n_out = tm // tw, d // MXU_COLS
        n_first = n_tok * nsub
        first = len(passes) // 2
        quota = ([first // n_first + (i < first % n_first) for i in range(n_first)]
                 + [(len(passes) - first) // n_out + (i < (len(passes) - first) % n_out) for i in range(n_out)])
        it = iter(passes)
        for c in range(n_tok):
            cols = slice(c * tw, (c + 1) * tw)
            for r in range(nsub):
                rows = slice(r * nkeys, (r + 1) * nkeys)
                act = _gelu(jnp.dot(u_ref[rows, :], xt_ref[:, cols], preferred_element_type=F32))
                ht_ref[cols, rows] = (cur_ref[rows, cols] * act).T.astype(BF16)
                for _ in range(quota[c * nsub + r]):
                    next(it)()
        for p in range(n_out):
            cols = slice(p * MXU_COLS, (p + 1) * MXU_COLS)
            o_ref[:, cols] += jnp.dot(ht_ref[...], v_ref[:, cols], preferred_element_type=F32)
            for _ in range(quota[n_first + p]):
                next(it)()

    @pl.when(j % 2 == 0)
    def _():
        step(gate_a_ref, gate_b_ref)

    @pl.when(j % 2 == 1)
    def _():
        step(gate_b_ref, gate_a_ref)


def peer_dense(xt, u, v, s1, s2, rows, cfg):
    d, n = xt.shape
    e = u.shape[0]
    heads, nkeys = cfg.peer_heads, cfg.peer_n_keys
    tm = _pick(n, (1024, 512, 256, 128))
    nsub = 2
    te = nsub * nkeys
    once = dict(pipeline_mode=pl.Buffered(1))
    tab_spec = pl.BlockSpec((heads, nkeys, tm), lambda i, j: (0, 0, i), **once)
    return pl.pallas_call(
        functools.partial(_peer_dense_body, heads=heads, nkeys=nkeys, nsub=nsub),
        grid=(n // tm, e // te),
        in_specs=[
            pl.BlockSpec((d, tm), lambda i, j: (0, i), **once),
            pl.BlockSpec((te, d), lambda i, j: (j, 0)),
            pl.BlockSpec((te, d), lambda i, j: (j, 0)),
            tab_spec, tab_spec,
            pl.BlockSpec((4, heads, tm), lambda i, j: (0, 0, i), **once),
        ],
        out_specs=pl.BlockSpec((tm, d), lambda i, j: (i, 0), **once),
        out_shape=jax.ShapeDtypeStruct((n, d), F32),
        scratch_shapes=[
            pltpu.VMEM((heads, nkeys, tm), F32),
            pltpu.VMEM((heads, SUBLANES, tm), F32),
            pltpu.VMEM((heads, nsub, 2, SUBLANES, tm), F32),
            pltpu.VMEM((te, tm), F32),
            pltpu.VMEM((te, tm), F32),
            pltpu.VMEM((tm, te), BF16),
        ],
        compiler_params=_params("parallel", "arbitrary"),
        name="peer_dense",
    )(xt, u, v, s1, s2, rows)


def peer_block(h, g_ffn, w_query, keys, u, v, cfg):
    n = h.shape[0]
    npad = -(-n // LANES) * LANES
    hp = h if npad == n else jnp.pad(h, ((0, npad - n), (0, 0)))
    xn, xt = rmsnorm_cast(hp, g_ffn[None, :], transposed=True)
    q = matmul(xn, w_query, name="peer_query")
    s1, s2, rows = peer_route(q, keys, cfg)
    return h + peer_dense(xt, u, v, s1, s2, rows, cfg)[:n]


def _moba_prompt_body(q_ref, k_ref, v_ref, o_ref, means_ref, sel_ref, m_ref, l_ref, acc_ref,
                      *, blk, nblk, group, hd, topk, scale):
    qb = pl.program_id(2)
    nt = (((1,), (1,)), ((), ()))
    tn = (((0,), (0,)), ((), ()))
    rows = group * blk

    @pl.when(qb == 0)
    def _():
        for n in range(nblk):
            means_ref[n:n + 1, :] = jnp.mean(k_ref[0, n * blk:(n + 1) * blk, :], axis=0, keepdims=True)

    q4 = jnp.concatenate([q_ref[0, :, g * hd:(g + 1) * hd] for g in range(group)], axis=0)

    gate = lax.dot_general(means_ref[...].astype(BF16), q4, nt, preferred_element_type=F32)
    bidx = lax.broadcasted_iota(jnp.int32, gate.shape, 0)
    cand = bidx < qb
    gate = jnp.where(cand, gate, NEG_INF)
    rank = jnp.zeros(gate.shape, F32)
    for mm in range(nblk):
        gm = gate[mm:mm + 1, :]
        beats = (gm > gate) | ((gm == gate) & (mm < bidx))
        rank = rank + jnp.where(beats, 1.0, 0.0)
    sel_ref[...] = jnp.where((rank < topk) & cand, 1.0, 0.0)

    kpos = lax.broadcasted_iota(jnp.int32, (blk, rows), 0)
    qpos = lax.broadcasted_iota(jnp.int32, (blk, rows), 1) % blk
    k_own = k_ref[0, pl.ds(qb * blk, blk), :].astype(BF16)
    v_own = v_ref[0, pl.ds(qb * blk, blk), :].astype(BF16)
    s = lax.dot_general(k_own, q4, nt, preferred_element_type=F32) * scale
    s = jnp.where(kpos <= qpos, s, NEG_INF)
    m0 = jnp.max(s, axis=0, keepdims=True)
    p = jnp.exp(s - m0)
    m_ref[...] = m0
    l_ref[...] = jnp.sum(p, axis=0, keepdims=True)
    acc_ref[...] = lax.dot_general(v_own, p.astype(BF16), tn, preferred_element_type=F32)

    def past(n, carry):
        kb = k_ref[0, pl.ds(n * blk, blk), :].astype(BF16)
        vb = v_ref[0, pl.ds(n * blk, blk), :].astype(BF16)
        on = sel_ref[pl.ds(n, 1), :] > 0.5
        sb = lax.dot_general(kb, q4, nt, preferred_element_type=F32) * scale
        sb = jnp.where(on, sb, NEG_INF)
        m_old = m_ref[...]
        m_new = jnp.maximum(m_old, jnp.max(sb, axis=0, keepdims=True))
        pb = jnp.where(on, jnp.exp(sb - m_new), 0.0)
        alpha = jnp.exp(m_old - m_new)
        m_ref[...] = m_new
        l_ref[...] = alpha * l_ref[...] + jnp.sum(pb, axis=0, keepdims=True)
        acc_ref[...] = alpha * acc_ref[...] + lax.dot_general(vb, pb.astype(BF16), tn, preferred_element_type=F32)
        return carry

    lax.fori_loop(0, qb, past, 0)

    out = (acc_ref[...] / l_ref[...]).T
    for g in range(group):
        o_ref[0, :, g * hd:(g + 1) * hd] = out[g * blk:(g + 1) * blk, :].astype(o_ref.dtype)


def moba_prompt(q, k, v, cfg):
    bsz, seq, _ = q.shape
    blk, hd, group, kvh = cfg.moba_block, cfg.head_dim, cfg.group, cfg.n_kv_heads
    assert seq % blk == 0
    nblk = seq // blk
    rows = group * blk
    nsel = -(-nblk // 8) * 8
    body = functools.partial(_moba_prompt_body, blk=blk, nblk=nblk, group=group, hd=hd, topk=cfg.moba_topk,
                             scale=hd ** -0.5)
    return pl.pallas_call(
        body,
        grid=(bsz, kvh, nblk),
        in_specs=[
            pl.BlockSpec((1, blk, group * hd), lambda b, kh, i: (b, i, kh)),
            pl.BlockSpec((1, seq, hd), lambda b, kh, i: (b, 0, kh)),
            pl.BlockSpec((1, seq, hd), lambda b, kh, i: (b, 0, kh)),
        ],
        out_specs=pl.BlockSpec((1, blk, group * hd), lambda b, kh, i: (b, i, kh)),
        out_shape=jax.ShapeDtypeStruct(q.shape, BF16),
        scratch_shapes=[
            pltpu.VMEM((nblk, hd), F32),
            pltpu.VMEM((nblk, rows), F32),
            pltpu.VMEM((1, rows), F32),
            pltpu.VMEM((1, rows), F32),
            pltpu.VMEM((hd, rows), F32),
        ],
        compiler_params=_params("parallel", "parallel", "arbitrary"),
        name="moba_prompt",
    )(q, k, v)


def _block_means_body(pt_ref, *refs, inv_rows):
    k_refs, o_ref = refs[:-1], refs[-1]
    o_ref[0, 0] = functools.reduce(jnp.add, [jnp.sum(k_ref[0], axis=0) for k_ref in k_refs]) * inv_rows


def _page_specs(ppb, ps, kvh, hd):
    return [pl.BlockSpec((1, ps, kvh, hd), functools.partial(lambda p, b, n, pt: (pt[b, n * ppb + p], 0, 0, 0), p))
            for p in range(ppb)]


def block_means(cache_k, page_table, n_full, ppb, cfg):
    db = page_table.shape[0]
    _, ps, kvh, hd = cache_k.shape
    return pl.pallas_call(
        functools.partial(_block_means_body, inv_rows=1.0 / (ps * ppb)),
        grid_spec=pltpu.PrefetchScalarGridSpec(
            num_scalar_prefetch=1,
            grid=(db, n_full),
            in_specs=_page_specs(ppb, ps, kvh, hd),
            out_specs=pl.BlockSpec((1, 1, kvh, hd), lambda b, n, pt: (b, n, 0, 0)),
        ),
        out_shape=jax.ShapeDtypeStruct((db, n_full, kvh, hd), F32),
        compiler_params=_params("parallel", "parallel"),
        name="block_means",
    )(page_table, *([cache_k] * ppb))


def _moba_select_body(q_ref, means_ref, sel_ref, *, topk, kvh):
    nt = (((1,), (1,)), ((), ()))
    gate = jnp.concatenate(
        [lax.dot_general(q_ref[0, kh], means_ref[0, :, kh, :].astype(BF16), nt, preferred_element_type=F32)
         for kh in range(kvh)], axis=0)
    lane = lax.broadcasted_iota(jnp.int32, gate.shape, 1)
    nb = gate.shape[1]
    sel = jnp.zeros(gate.shape, F32)
    for _ in range(topk):
        m = jnp.max(gate, axis=1, keepdims=True)
        first = jnp.min(jnp.where(gate == m, lane, nb), axis=1, keepdims=True)
        pick = lane == first
        sel = jnp.where(pick, 1.0, sel)
        gate = jnp.where(pick, -3.0e38, gate)
    sel_ref[0] = sel


def moba_select(q4, means, cfg):
    db, kvh, rpk, hd = q4.shape
    n_full = means.shape[1]
    rows = kvh * rpk
    return pl.pallas_call(
        functools.partial(_moba_select_body, topk=min(cfg.moba_topk, n_full), kvh=kvh),
        grid=(db,),
        in_specs=[pl.BlockSpec((1, kvh, rpk, hd), lambda b: (b, 0, 0, 0)),
                  pl.BlockSpec((1, n_full, kvh, hd), lambda b: (b, 0, 0, 0))],
        out_specs=pl.BlockSpec((1, rows, n_full), lambda b: (b, 0, 0)),
        out_shape=jax.ShapeDtypeStruct((db, rows, n_full), F32),
        compiler_params=_params("parallel"),
        name="moba_select",
    )(q4, means)


def _moba_sample_body(pt_ref, *refs, ppb, scale, t_new, kvh, rpk):
    q_ref, sel_ref, kn_ref, vn_ref = refs[0], refs[1 + 2 * ppb], refs[2 + 2 * ppb], refs[3 + 2 * ppb]
    k_refs, v_refs = refs[1:1 + ppb], refs[1 + ppb:1 + 2 * ppb]
    o_ref, own_ref, m_ref, l_ref, acc_ref = refs[4 + 2 * ppb:]
    n = pl.program_id(1)
    nt = (((1,), (1,)), ((), ()))
    q = q_ref[0]
    rows, page_rows = q.shape[0], kn_ref.shape[1]

    @pl.when(n == 0)
    def _():
        m_ref[...] = jnp.full(m_ref.shape, NEG_INF, F32)
        l_ref[...] = jnp.zeros(l_ref.shape, F32)
        acc_ref[...] = jnp.zeros(acc_ref.shape, F32)
        shape = (rows, ppb * page_rows)
        same = (lax.broadcasted_iota(jnp.int32, shape, 1) % kvh) == (lax.broadcasted_iota(jnp.int32, shape, 0) // rpk)
        own_ref[...] = jnp.where(same, 1.0, 0.0)

    def absorb(k_pages, v_pages, on):
        kb = jnp.concatenate([kp[0].astype(BF16) for kp in k_pages], axis=0)
        vb = jnp.concatenate([vp[0].astype(BF16) for vp in v_pages], axis=0)
        s = lax.dot_general(q, kb, nt, preferred_element_type=F32) * scale
        s = jnp.where(on, s, NEG_INF)
        m_old = m_ref[...]
        m_new = jnp.maximum(m_old, jnp.max(s, axis=1, keepdims=True))
        p = jnp.where(on, jnp.exp(s - m_new), 0.0)
        alpha = jnp.exp(m_old - m_new)
        m_ref[...] = m_new
        l_ref[...] = alpha * l_ref[...] + jnp.sum(p, axis=1, keepdims=True)
        acc_ref[...] = alpha * acc_ref[...] + jnp.dot(p.astype(BF16), vb, preferred_element_type=F32)

    absorb(k_refs, v_refs, (own_ref[...] * sel_ref[0, 0]) > 0.5)

    @pl.when(n == pl.num_programs(1) - 1)
    def _():
        shape = (rows, page_rows)
        tq = lax.broadcasted_iota(jnp.int32, shape, 0) % t_new
        tk = lax.broadcasted_iota(jnp.int32, shape, 1) // kvh
        absorb([kn_ref], [vn_ref], (own_ref[:, :page_rows] > 0.5) & (tk <= tq))
        o_ref[0] = acc_ref[...] / l_ref[...]


def moba_sample(q, cache_k, cache_v, page_table, sel, k_new, v_new, t_new, ppb, cfg):
    db, rows, hd = q.shape
    kvh = cfg.n_kv_heads
    n_full = sel.shape[1]
    page_rows = cache_k.shape[1]
    body = functools.partial(_moba_sample_body, ppb=ppb, scale=hd ** -0.5, t_new=t_new, kvh=kvh, rpk=rows // kvh)
    page_specs = [
        pl.BlockSpec((1, page_rows, hd), functools.partial(lambda p, b, n, pt: (pt[b, n * ppb + p], 0, 0), p))
        for p in range(ppb)]
    new_spec = pl.BlockSpec((1, page_rows, hd), lambda b, n, pt: (b, 0, 0))
    return pl.pallas_call(
        body,
        grid_spec=pltpu.PrefetchScalarGridSpec(
            num_scalar_prefetch=1,
            grid=(db, n_full),
            in_specs=([pl.BlockSpec((1, rows, hd), lambda b, n, pt: (b, 0, 0))] + page_specs + page_specs
                      + [pl.BlockSpec((1, 1, rows, 1), lambda b, n, pt: (b, n, 0, 0)), new_spec, new_spec]),
            out_specs=pl.BlockSpec((1, rows, hd), lambda b, n, pt: (b, 0, 0)),
            scratch_shapes=[
                pltpu.VMEM((rows, ppb * page_rows), F32),
                pltpu.VMEM((rows, 1), F32),
                pltpu.VMEM((rows, 1), F32),
                pltpu.VMEM((rows, hd), F32),
            ],
        ),
        out_shape=jax.ShapeDtypeStruct((db, rows, hd), F32),
        compiler_params=_params("parallel", "arbitrary"),
        name="moba_sample",
    )(page_table, q, *([cache_k] * ppb), *([cache_v] * ppb), sel, k_new, v_new)


def _cast_body(x_ref, o_ref):
    o_ref[...] = x_ref[0].astype(o_ref.dtype)


def cast_layer(w, layer, cols=None):
    _, r, c = w.shape
    cols = c if cols is None else cols
    tr = next(t for t in (512, 256, 128, 64, 32, 16) if r % t == 0 and t * cols * 4 <= 8 * 1024 * 1024)
    return pl.pallas_call(
        _cast_body,
        grid=(r // tr,),
        in_specs=[pl.BlockSpec((1, tr, cols), lambda i: (layer, i, 0))],
        out_specs=pl.BlockSpec((tr, cols), lambda i: (i, 0)),
        out_shape=jax.ShapeDtypeStruct((r, cols), BF16),
        compiler_params=_params("parallel"),
        name="cast_bf16",
    )(w)


def _cast_tail_body(x_ref, o_ref, *, valid):
    lane = lax.broadcasted_iota(jnp.int32, o_ref.shape, 1)
    o_ref[...] = jnp.where(lane < valid, x_ref[0], 0.0).astype(o_ref.dtype)


def cast_tail(w, layer, col0):
    _, r, c = w.shape
    assert col0 % LANES == 0 and 0 < c - col0 <= LANES
    tr = _pick(r, (512, 256, 128, 64))
    return pl.pallas_call(
        functools.partial(_cast_tail_body, valid=c - col0),
        grid=(r // tr,),
        in_specs=[pl.BlockSpec((1, tr, LANES), lambda i: (layer, i, col0 // LANES))],
        out_specs=pl.BlockSpec((tr, LANES), lambda i: (i, 0)),
        out_shape=jax.ShapeDtypeStruct((r, LANES), BF16),
        compiler_params=_params("parallel"),
        name="cast_tail_bf16",
    )(w)


class Weights(NamedTuple):
    norm_mix: jax.Array
    norm_ffn: jax.Array
    w_in: jax.Array
    w_z: jax.Array
    w_gate2: jax.Array
    b_gate: jax.Array
    gla_norm: jax.Array
    w_out: jax.Array
    norm_kv: jax.Array
    w_kv: jax.Array
    k_norm: jax.Array
    w_q: jax.Array
    q_norm: jax.Array
    w_o: jax.Array
    peer_w_query: tuple
    peer_keys: tuple
    peer_u: tuple
    peer_v: tuple


def _prep_weights(norm_mix, norm_ffn, gla_w_in, gla_w_gate2, gla_b_gate, gla_norm, gla_w_out, norm_kv, w_kv,
                  k_norm, w_q, q_norm, w_o, peer_w_query, peer_sub_keys, peer_u, peer_v, cfg):
    main = 2 * cfg.gla_dk_tot + 2 * cfg.d_model
    rank = cfg.gla_gate_rank
    depth = norm_mix.shape[0]
    return Weights(
        norm_mix=norm_mix,
        norm_ffn=norm_ffn,
        w_in=cast_layer(gla_w_in, 0, cols=main),
        w_z=cast_tail(gla_w_in, 0, main),
        w_gate2=jnp.pad(gla_w_gate2[0], ((0, LANES - rank), (0, 0))).astype(BF16),
        b_gate=gla_b_gate[0][None, :],
        gla_norm=gla_norm[0].reshape(1, -1),
        w_out=gla_w_out[0].astype(BF16),
        norm_kv=norm_kv,
        w_kv=w_kv.astype(BF16),
        k_norm=k_norm[None, :],
        w_q=w_q[0].astype(BF16),
        q_norm=q_norm[0][None, :],
        w_o=w_o[0].astype(BF16),
        peer_w_query=tuple(peer_w_query[l].astype(BF16) for l in range(depth)),
        peer_keys=tuple(peer_sub_keys[l].reshape(2 * cfg.peer_heads, cfg.peer_n_keys, -1).astype(BF16)
                        for l in range(depth)),
        peer_u=tuple(cast_layer(peer_u, l) for l in range(depth)),
        peer_v=tuple(cast_layer(peer_v, l) for l in range(depth)),
    )


def _trunk(x, pos0, s0, attend, w, cfg):
    bsz, seq, d = x.shape
    n = bsz * seq
    hd = cfg.head_dim
    xf = x.reshape(n, d)

    (hn,) = rmsnorm_cast(xf, w.norm_mix[0:1])
    proj = matmul(hn, w.w_in, name="gla_in")
    z = matmul(hn, w.w_z, name="gla_gate_in")
    og, s_fin = gla(proj.reshape(bsz, seq, -1), z.reshape(bsz, seq, -1), w.w_gate2, w.b_gate, w.gla_norm, s0, cfg)
    h = matmul(og.reshape(n, -1), w.w_out, residual=xf, name="gla_out")
    h = peer_block(h, w.norm_ffn[0], w.peer_w_query[0], w.peer_keys[0], w.peer_u[0], w.peer_v[0], cfg)

    kvn, hn1 = rmsnorm_cast(h, jnp.stack([w.norm_kv, w.norm_mix[1]]))
    kv = matmul(kvn, w.w_kv, name="kv_proj")
    cos, sin = _rope_tables(pos0, seq, hd)
    (k,) = head_norm_rope(kv[:, :cfg.kv_dim], w.k_norm, cos, sin, seq, (F32,), hd)
    v = kv[:, cfg.kv_dim:]
    (q,) = head_norm_rope(matmul(hn1, w.w_q, name="q_proj"), w.q_norm, cos, sin, seq, (BF16,), hd)

    o = attend(q, k, v)
    h = matmul(o, w.w_o, residual=h, name="attn_out")
    h = peer_block(h, w.norm_ffn[1], w.peer_w_query[1], w.peer_keys[1], w.peer_u[1], w.peer_v[1], cfg)
    return (h.reshape(bsz, seq, d), s_fin[None],
            k.reshape(bsz, seq, cfg.n_kv_heads, hd), v.reshape(bsz, seq, cfg.n_kv_heads, hd))


def _attend_prompt(bsz, seq, cfg):
    def attend(q, k, v):
        n = q.shape[0]
        o = moba_prompt(q.reshape(bsz, seq, -1), k.reshape(bsz, seq, -1), v.reshape(bsz, seq, -1), cfg)
        return o.reshape(n, -1)
    return attend


def _attend_sample(db, t_new, cache_k, cache_v, page_table, past_len, cfg):
    hd, kvh, group, nh = cfg.head_dim, cfg.n_kv_heads, cfg.group, cfg.n_heads
    blk, ps = cfg.moba_block, cfg.page_size
    ppb = blk // ps
    n_full = past_len // blk
    assert past_len % blk == 0 and n_full > 0 and t_new <= ps
    pt = page_table[:, :n_full * ppb]
    ck = cache_k.reshape(cache_k.shape[0], ps * kvh, hd)
    cv = cache_v.reshape(cache_v.shape[0], ps * kvh, hd)

    def attend(q, k, v):
        qr = q.reshape(db, t_new, nh, hd).transpose(0, 2, 1, 3).reshape(db, nh * t_new, hd)
        means = block_means(cache_k, pt, n_full, ppb, cfg)
        sel = moba_select(qr.reshape(db, kvh, group * t_new, hd), means, cfg)
        sel = sel.transpose(0, 2, 1)[..., None]
        pad = ((0, 0), (0, ps - t_new), (0, 0), (0, 0))
        k_new = jnp.pad(k.reshape(db, t_new, kvh, hd), pad).reshape(db, ps * kvh, hd)
        v_new = jnp.pad(v.reshape(db, t_new, kvh, hd), pad).reshape(db, ps * kvh, hd)
        o = moba_sample(qr, ck, cv, pt, sel, k_new, v_new, t_new, ppb, cfg)
        o = o.reshape(db, nh, t_new, hd).transpose(0, 2, 1, 3).reshape(db * t_new, nh * hd)
        return o.astype(BF16)
    return attend


def _forward(x_prompt, x_sample, cache_k, cache_v, state_gla, page_table, weights, past_len, cfg):
    w = _prep_weights(*weights, cfg)
    bsz, seq, _ = x_prompt.shape
    db, t_new, _ = x_sample.shape
    s0_p = jnp.zeros((bsz, cfg.gla_heads, cfg.gla_dk, cfg.gla_dv), state_gla.dtype)
    y_p, st_p, k_p, v_p = _trunk(x_prompt, 0, s0_p, _attend_prompt(bsz, seq, cfg), w, cfg)
    attend_s = _attend_sample(db, t_new, cache_k, cache_v, page_table, past_len, cfg)
    y_s, st_s, k_s, v_s = _trunk(x_sample, past_len, state_gla[0], attend_s, w, cfg)
    return (y_p, y_s, st_p, st_s, k_p, v_p, k_s, v_s)


def kernel(x_prompt, x_sample, cache_k, cache_v, state_gla, page_table, norm_mix, norm_ffn, gla_w_in, gla_w_gate2, gla_b_gate, gla_norm, gla_w_out, norm_kv, w_kv, k_norm, w_q, q_norm, w_o, peer_w_query, peer_sub_keys, peer_u, peer_v):
    weights = (norm_mix, norm_ffn, gla_w_in, gla_w_gate2, gla_b_gate, gla_norm, gla_w_out, norm_kv, w_kv, k_norm,
               w_q, q_norm, w_o, peer_w_query, peer_sub_keys, peer_u, peer_v)
    past_len = page_table.shape[1] * CFG.page_size
    return _forward(x_prompt, x_sample, cache_k, cache_v, state_gla, page_table, weights, past_len, CFG)
```

```python
import functools
import math
from typing import NamedTuple

import numpy as np
import jax
import jax.numpy as jnp
from jax import lax
from jax.experimental import pallas as pl
from jax.experimental.pallas import tpu as pltpu

F32 = jnp.float32
BF16 = jnp.bfloat16

LANES = 128
MXU_COLS = 256
RMS_EPS = 1e-6
NEG_INF = -1e30
ROPE_THETA = 10000.0
VMEM_LIMIT_BYTES = 56 * 1024 * 1024


class Cfg(NamedTuple):
    d_model: int = 4096
    gla_heads: int = 4
    gla_gate_rank: int = 16
    gla_gate_tau: float = 16.0
    gla_chunk: int = 64
    head_dim: int = 128
    n_kv_heads: int = 8
    moba_block: int = 256
    moba_topk: int = 3
    page_size: int = 128
    peer_heads: int = 8
    peer_n_keys: int = 128
    peer_topk: int = 16
    peer_key_dim: int = 256

    @property
    def gla_dk_tot(self):
        return self.d_model // 2

    @property
    def gla_dk(self):
        return self.gla_dk_tot // self.gla_heads

    @property
    def gla_dv(self):
        return self.d_model // self.gla_heads

    @property
    def n_heads(self):
        return self.d_model // self.head_dim

    @property
    def group(self):
        return self.n_heads // self.n_kv_heads

    @property
    def kv_dim(self):
        return self.n_kv_heads * self.head_dim


CFG = Cfg()


def _params(*sem, flags=None):
    return pltpu.CompilerParams(dimension_semantics=sem, vmem_limit_bytes=VMEM_LIMIT_BYTES, flags=flags)


def _pick(n, prefs):
    for p in prefs:
        if n % p == 0:
            return p
    return n


def _rmsnorm_body(x_ref, g_ref, *o_refs, transposed):
    x = x_ref[...]
    y = x * lax.rsqrt(jnp.mean(x * x, axis=-1, keepdims=True) + RMS_EPS)
    ng = g_ref.shape[0]
    for i in range(ng):
        o_refs[i][...] = (y * g_ref[i:i + 1, :]).astype(o_refs[i].dtype)
    if transposed:
        o_refs[ng][...] = (y * g_ref[0:1, :]).T.astype(o_refs[ng].dtype)


def rmsnorm_cast(x, gains, transposed=False):
    m, d = x.shape
    g = gains.shape[0]
    tm = _pick(m, (256, 128, 64))
    out_specs = [pl.BlockSpec((tm, d), lambda i: (i, 0))] * g
    out_shape = [jax.ShapeDtypeStruct((m, d), BF16)] * g
    if transposed:
        out_specs = out_specs + [pl.BlockSpec((d, tm), lambda i: (0, i))]
        out_shape = out_shape + [jax.ShapeDtypeStruct((d, m), BF16)]
    return pl.pallas_call(
        functools.partial(_rmsnorm_body, transposed=transposed),
        grid=(m // tm,),
        in_specs=[pl.BlockSpec((tm, d), lambda i: (i, 0)), pl.BlockSpec((g, d), lambda i: (0, 0))],
        out_specs=out_specs,
        out_shape=out_shape,
        compiler_params=_params("parallel"),
        name="rmsnorm_cast",
    )(x, gains)


def _mm_body(x_ref, w_ref, o_ref):
    o_ref[...] = jnp.dot(x_ref[...], w_ref[...], preferred_element_type=F32).astype(o_ref.dtype)


def _mm_res_body(x_ref, w_ref, r_ref, o_ref):
    o_ref[...] = (r_ref[...] + jnp.dot(x_ref[...], w_ref[...], preferred_element_type=F32)).astype(o_ref.dtype)


def matmul(x, w, residual=None, out_dtype=F32, name="matmul"):
    m, k = x.shape
    n = w.shape[1]
    tm = _pick(m, (512, 256, 128, 64))
    tn = _pick(n, (1024, 512, 256, 128))
    in_specs = [pl.BlockSpec((tm, k), lambda j, i: (i, 0)), pl.BlockSpec((k, tn), lambda j, i: (0, j))]
    args = [x, w]
    body = _mm_body
    if residual is not None:
        in_specs.append(pl.BlockSpec((tm, tn), lambda j, i: (i, j)))
        args.append(residual)
        body = _mm_res_body
    return pl.pallas_call(
        body,
        grid=(n // tn, m // tm),
        in_specs=in_specs,
        out_specs=pl.BlockSpec((tm, tn), lambda j, i: (i, j)),
        out_shape=jax.ShapeDtypeStruct((m, n), out_dtype),
        compiler_params=_params("parallel", "parallel"),
        name=name,
    )(*args)


def _cumsum_rows(g):
    c = g.shape[0]
    row = lax.broadcasted_iota(jnp.int32, g.shape, 0)
    b = g
    s = 1
    while s < c:
        b = b + jnp.where(row >= s, pltpu.roll(b, s, axis=0), 0.0)
        s *= 2
    return b


def _log_sigmoid(x):
    return -(jnp.maximum(-x, 0.0) + jnp.log1p(jnp.exp(-jnp.abs(x))))


def _gla_body(q_ref, k_ref, v_ref, r_ref, z_ref, wg_ref, bg_ref, gn_ref, s0_ref, og_ref, sf_ref, st_ref,
              *, chunk, nsub, q_scale, inv_tau):
    t = pl.program_id(2)

    @pl.when(t == 0)
    def _():
        st_ref[...] = s0_ref[0, 0].T

    za = jnp.dot(z_ref[0].astype(BF16), wg_ref[...], preferred_element_type=F32) + bg_ref[...]
    log_a = _log_sigmoid(za) * inv_tau
    tril = (lax.broadcasted_iota(jnp.int32, (chunk, chunk), 0) >= lax.broadcasted_iota(jnp.int32, (chunk, chunk), 1))
    nt = (((1,), (1,)), ((), ()))
    tn = (((0,), (0,)), ((), ()))
    for i in range(nsub):
        sl = slice(i * chunk, (i + 1) * chunk)
        b = _cumsum_rows(log_a[sl])
        b_last = b[chunk - 1:chunk, :]
        q = q_ref[0, sl, :] * q_scale
        k = k_ref[0, sl, :]
        v = v_ref[0, sl, :].astype(BF16)
        qe = (q * jnp.exp(b)).astype(BF16)
        ke = (k * jnp.exp(-b)).astype(BF16)
        kd = (k * jnp.exp(b_last - b)).astype(BF16)
        att = lax.dot_general(qe, ke, nt, preferred_element_type=F32)
        att = jnp.where(tril, att, 0.0).astype(BF16)
        st = st_ref[...]
        o = lax.dot_general(qe, st.astype(BF16), nt, preferred_element_type=F32)
        o = o + jnp.dot(att, v, preferred_element_type=F32)
        st_ref[...] = st * jnp.exp(b_last) + lax.dot_general(v, kd, tn, preferred_element_type=F32)
        on = o * lax.rsqrt(jnp.mean(o * o, axis=-1, keepdims=True) + RMS_EPS) * gn_ref[...]
        r = r_ref[0, sl, :]
        og_ref[0, sl, :] = (on * (r * jax.nn.sigmoid(r))).astype(og_ref.dtype)

    @pl.when(t == pl.num_programs(2) - 1)
    def _():
        sf_ref[0, 0] = st_ref[...].T


def gla(proj, z, wg, bg, gn, s0, cfg):
    bsz, seq, _ = proj.shape
    h, dk, dv = cfg.gla_heads, cfg.gla_dk, cfg.gla_dv
    chunk = math.gcd(seq, cfg.gla_chunk)
    rows = _pick(seq, (4 * chunk, 2 * chunk, chunk))
    nsub = rows // chunk
    kq, kk, kv, kr = 0, h, (2 * h * dk) // dv, (2 * h * dk) // dv + h
    body = functools.partial(_gla_body, chunk=chunk, nsub=nsub, q_scale=dk ** -0.5, inv_tau=1.0 / cfg.gla_gate_tau)
    return pl.pallas_call(
        body,
        grid=(bsz, h, seq // rows),
        in_specs=[
            pl.BlockSpec((1, rows, dk), lambda b, hh, t: (b, t, kq + hh)),
            pl.BlockSpec((1, rows, dk), lambda b, hh, t: (b, t, kk + hh)),
            pl.BlockSpec((1, rows, dv), lambda b, hh, t: (b, t, kv + hh)),
            pl.BlockSpec((1, rows, dv), lambda b, hh, t: (b, t, kr + hh)),
            pl.BlockSpec((1, rows, LANES), lambda b, hh, t: (b, t, 0)),
            pl.BlockSpec((LANES, dk), lambda b, hh, t: (0, hh)),
            pl.BlockSpec((1, dk), lambda b, hh, t: (0, hh)),
            pl.BlockSpec((1, dv), lambda b, hh, t: (0, hh)),
            pl.BlockSpec((1, 1, dk, dv), lambda b, hh, t: (b, hh, 0, 0)),
        ],
        out_specs=[
            pl.BlockSpec((1, rows, dv), lambda b, hh, t: (b, t, hh)),
            pl.BlockSpec((1, 1, dk, dv), lambda b, hh, t: (b, hh, 0, 0)),
        ],
        out_shape=[
            jax.ShapeDtypeStruct((bsz, seq, h * dv), BF16),
            jax.ShapeDtypeStruct((bsz, h, dk, dv), F32),
        ],
        scratch_shapes=[pltpu.VMEM((dv, dk), F32)],
        compiler_params=_params("parallel", "parallel", "arbitrary"),
        name="gla",
    )(proj, proj, proj, proj, z, wg, bg, gn, s0)


def _headrope_body(x_ref, g_ref, cos_ref, sin_ref, *o_refs, nh, hd):
    for hh in range(nh):
        x = x_ref[:, hh * hd:(hh + 1) * hd]
        y = x * lax.rsqrt(jnp.mean(x * x, axis=-1, keepdims=True) + RMS_EPS) * g_ref[...]
        out = y * cos_ref[...] + pltpu.roll(y, hd // 2, axis=1) * sin_ref[...]
        for o_ref in o_refs:
            o_ref[:, hh * hd:(hh + 1) * hd] = out.astype(o_ref.dtype)


def head_norm_rope(x, gain, cos, sin, seq, out_dtypes, hd):
    n, width = x.shape
    nh = width // hd
    tm = _pick(seq, (256, 128, 64, 32, 16, 8))
    per = seq // tm
    return pl.pallas_call(
        functools.partial(_headrope_body, nh=nh, hd=hd),
        grid=(n // tm,),
        in_specs=[
            pl.BlockSpec((tm, width), lambda i: (i, 0)),
            pl.BlockSpec((1, hd), lambda i: (0, 0)),
            pl.BlockSpec((tm, hd), lambda i: (i % per, 0)),
            pl.BlockSpec((tm, hd), lambda i: (i % per, 0)),
        ],
        out_specs=[pl.BlockSpec((tm, width), lambda i: (i, 0)) for _ in out_dtypes],
        out_shape=[jax.ShapeDtypeStruct((n, width), dt) for dt in out_dtypes],
        compiler_params=_params("parallel"),
        name="head_norm_rope",
    )(x, gain, cos, sin)


def _rope_tables(pos0, seq, hd):
    half = hd // 2
    inv = ROPE_THETA ** (-np.arange(half, dtype=np.float64) / half)
    ang = (pos0 + np.arange(seq, dtype=np.float64))[:, None] * inv[None, :]
    cos, sin = np.cos(ang), np.sin(ang)
    return (jnp.asarray(np.concatenate([cos, cos], axis=1), F32),
            jnp.asarray(np.concatenate([-sin, sin], axis=1), F32))


def _top_values(s, k):
    vals = []
    cur = s
    for i in range(k):
        m = jnp.max(cur, axis=0, keepdims=True)
        vals.append(m)
        if i + 1 < k:
            cur = jnp.where(cur == m, NEG_INF, cur)
    return vals


ROW_TAU, ROW_M1, ROW_M2, ROW_INVZ = range(4)


def _peer_route_body(q_ref, keys_ref, s1_ref, s2_ref, rows_ref, *, heads, nkeys, half, topk):
    nt = (((1,), (1,)), ((), ()))
    tm = q_ref.shape[0]
    sub = 8
    row = lax.broadcasted_iota(jnp.int32, (sub, tm), 0)
    for hh in range(heads):
        st = []
        for p in range(2):
            g = 2 * hh + p
            qg = q_ref[:, g * half:(g + 1) * half].astype(BF16)
            st.append(lax.dot_general(keys_ref[g], qg, nt, preferred_element_type=F32))
        s1, s2 = st
        v1 = _top_values(s1, topk)
        v2 = _top_values(s2, topk)
        v2g = []
        for g0 in range(0, topk, sub):
            grp = jnp.full((sub, tm), NEG_INF, F32)
            for b in range(g0, min(g0 + sub, topk)):
                grp = jnp.where(row == b - g0, v2[b], grp)
            v2g.append(grp)
        cands = []
        for a in range(topk):
            bmax = topk // (a + 1)
            for gi, grp in enumerate(v2g):
                if gi * sub < bmax:
                    cands.append(jnp.where(row < bmax - gi * sub, v1[a] + grp, NEG_INF))
        cur = cands
        tau = None
        for i in range(topk):
            tau = functools.reduce(jnp.maximum, [jnp.max(c, axis=0, keepdims=True) for c in cur])
            if i + 1 < topk:
                cur = [jnp.where(c == tau, NEG_INF, c) for c in cur]
        m1, m2 = v1[0], v2[0]
        mx = m1 + m2
        z = functools.reduce(
            jnp.add, [jnp.sum(jnp.where(c >= tau, jnp.exp(c - mx), 0.0), axis=0, keepdims=True) for c in cands])
        s1_ref[hh] = s1
        s2_ref[hh] = s2
        rows_ref[ROW_TAU, hh:hh + 1, :] = tau
        rows_ref[ROW_M1, hh:hh + 1, :] = m1
        rows_ref[ROW_M2, hh:hh + 1, :] = m2
        rows_ref[ROW_INVZ, hh:hh + 1, :] = 1.0 / z


def peer_route(q, keys, cfg):
    n = q.shape[0]
    heads, nkeys, half = cfg.peer_heads, cfg.peer_n_keys, cfg.peer_key_dim // 2
    tm = _pick(n, (256, 128))
    tab = jax.ShapeDtypeStruct((heads, nkeys, n), F32)
    tab_spec = pl.BlockSpec((heads, nkeys, tm), lambda i: (0, 0, i))
    return pl.pallas_call(
        functools.partial(_peer_route_body, heads=heads, nkeys=nkeys, half=half, topk=cfg.peer_topk),
        grid=(n // tm,),
        in_specs=[pl.BlockSpec((tm, q.shape[1]), lambda i: (i, 0)),
                  pl.BlockSpec(keys.shape, lambda i: (0, 0, 0))],
        out_specs=[tab_spec, tab_spec, pl.BlockSpec((4, heads, tm), lambda i: (0, 0, i))],
        out_shape=[tab, tab, jax.ShapeDtypeStruct((4, heads, n), F32)],
        compiler_params=_params("parallel"),
        name="peer_route",
    )(q, keys)


def _gelu(x):
    return 0.5 * x * (1.0 + lax.erf(x * (1.0 / math.sqrt(2.0))))


SUBLANES = 8
GATE_LANES = 512


def _peer_row_tables(bc_ref, jt, s1_ref, rows_ref, *, heads, nsub):
    tm = bc_ref.shape[-1]
    for r in range(nsub):
        i1 = jt * nsub + r
        for hh in range(heads):
            s1row = s1_ref[hh, pl.ds(i1, 1), :]
            crow = jnp.exp(s1row - rows_ref[ROW_M1, hh:hh + 1, :]) * rows_ref[ROW_INVZ, hh:hh + 1, :]
            bc_ref[hh, r, 0] = jnp.broadcast_to(s1row, (SUBLANES, tm))
            bc_ref[hh, r, 1] = jnp.broadcast_to(crow, (SUBLANES, tm))


def _peer_gate_passes(dst_ref, s2_ref, e2_ref, taub_ref, bc_ref, *, heads, nkeys, nsub):
    tm = dst_ref.shape[-1]
    tw = min(tm, GATE_LANES)

    def one(c, l):
        rows, cols = slice(c, c + SUBLANES), slice(l, l + tw)
        ws = [None] * nsub
        for hh in range(heads):
            s2t, e2t, tb = s2_ref[hh, rows, cols], e2_ref[hh, rows, cols], taub_ref[hh, :, cols]
            for r in range(nsub):
                hit = (s2t + bc_ref[hh, r, 0, :, cols]) >= tb
                term = jnp.where(hit, e2t, 0.0) * bc_ref[hh, r, 1, :, cols]
                ws[r] = term if ws[r] is None else ws[r] + term
        for r in range(nsub):
            dst_ref[r * nkeys + c:r * nkeys + c + SUBLANES, cols] = ws[r]

    return [functools.partial(one, c, l) for c in range(0, nkeys, SUBLANES) for l in range(0, tm, tw)]


def _peer_dense_body(xt_ref, u_ref, v_ref, s1_ref, s2_ref, rows_ref, o_ref, e2_ref, taub_ref, bc_ref,
                     gate_a_ref, gate_b_ref, ht_ref, *, heads, nkeys, nsub):
    j = pl.program_id(1)
    last = pl.num_programs(1) - 1
    row_tables = functools.partial(_peer_row_tables, bc_ref, s1_ref=s1_ref, rows_ref=rows_ref, heads=heads, nsub=nsub)
    gate_passes = functools.partial(_peer_gate_passes, s2_ref=s2_ref, e2_ref=e2_ref, taub_ref=taub_ref, bc_ref=bc_ref,
                                    heads=heads, nkeys=nkeys, nsub=nsub)
    tm = xt_ref.shape[1]
    d = o_ref.shape[1]

    @pl.when(j == 0)
    def _():
        o_ref[...] = jnp.zeros_like(o_ref)
        for hh in range(heads):
            e2_ref[hh] = jnp.exp(s2_ref[hh] - rows_ref[ROW_M2, hh:hh + 1, :])
            taub_ref[hh] = jnp.broadcast_to(rows_ref[ROW_TAU, hh:hh + 1, :], (SUBLANES, tm))
        row_tables(0)
        for run in gate_passes(gate_a_ref):
            run()

    def step(cur_ref, nxt_ref):
        row_tables(jnp.minimum(j + 1, last))
        passes = gate_passes(nxt_ref)
        tw = min(tm, MXU_COLS)
        n_tok, n_out = tm // tw, d // MXU_COLS
        n_first = n_tok
        first = len(passes) // 2
        quota = ([first // n_first + (i < first % n_first) for i in range(n_first)]
                 + [(len(passes) - first) // n_out + (i < (len(passes) - first) % n_out) for i in range(n_out)])
        it = iter(passes)
        for c in range(n_tok):
            cols = slice(c * tw, (c + 1) * tw)
            act = _gelu(jnp.dot(u_ref[...], xt_ref[:, cols], preferred_element_type=F32))
            ht_ref[cols, :] = (cur_ref[:, cols] * act).T.astype(BF16)
            for _ in range(quota[c]):
                next(it)()
        for p in range(n_out):
            cols = slice(p * MXU_COLS, (p + 1) * MXU_COLS)
            o_ref[:, cols] += jnp.dot(ht_ref[...], v_ref[:, cols], preferred_element_type=F32)
            for _ in range(quota[n_first + p]):
                next(it)()

    @pl.when(j % 2 == 0)
    def _():
        step(gate_a_ref, gate_b_ref)

    @pl.when(j % 2 == 1)
    def _():
        step(gate_b_ref, gate_a_ref)


def peer_dense(xt, u, v, s1, s2, rows, cfg):
    d, n = xt.shape
    e = u.shape[0]
    heads, nkeys = cfg.peer_heads, cfg.peer_n_keys
    tm = _pick(n, (1024, 512, 256, 128))
    nsub = 2
    te = nsub * nkeys
    once = dict(pipeline_mode=pl.Buffered(1))
    tab_spec = pl.BlockSpec((heads, nkeys, tm), lambda i, j: (0, 0, i), **once)
    return pl.pallas_call(
        functools.partial(_peer_dense_body, heads=heads, nkeys=nkeys, nsub=nsub),
        grid=(n // tm, e // te),
        in_specs=[
            pl.BlockSpec((d, tm), lambda i, j: (0, i), **once),
            pl.BlockSpec((te, d), lambda i, j: (j, 0)),
            pl.BlockSpec((te, d), lambda i, j: (j, 0)),
            tab_spec, tab_spec,
            pl.BlockSpec((4, heads, tm), lambda i, j: (0, 0, i), **once),
        ],
        out_specs=pl.BlockSpec((tm, d), lambda i, j: (i, 0), **once),
        out_shape=jax.ShapeDtypeStruct((n, d), F32),
        scratch_shapes=[
            pltpu.VMEM((heads, nkeys, tm), F32),
            pltpu.VMEM((heads, SUBLANES, tm), F32),
            pltpu.VMEM((heads, nsub, 2, SUBLANES, tm), F32),
            pltpu.VMEM((te, tm), F32),
            pltpu.VMEM((te, tm), F32),
            pltpu.VMEM((tm, te), BF16),
        ],
        compiler_params=_params("parallel", "arbitrary"),
        name="peer_dense",
    )(xt, u, v, s1, s2, rows)


def peer_block(h, g_ffn, w_query, keys, u, v, cfg):
    n = h.shape[0]
    npad = -(-n // LANES) * LANES
    hp = h if npad == n else jnp.pad(h, ((0, npad - n), (0, 0)))
    xn, xt = rmsnorm_cast(hp, g_ffn[None, :], transposed=True)
    q = matmul(xn, w_query, name="peer_query")
    s1, s2, rows = peer_route(q, keys, cfg)
    return h + peer_dense(xt, u, v, s1, s2, rows, cfg)[:n]


def _moba_prompt_body(q_ref, k_ref, v_ref, o_ref, means_ref, sel_ref, m_ref, l_ref, acc_ref,
                      *, blk, nblk, group, hd, topk, scale):
    qb = pl.program_id(2)
    nt = (((1,), (1,)), ((), ()))
    tn = (((0,), (0,)), ((), ()))
    rows = group * blk

    @pl.when(qb == 0)
    def _():
        for n in range(nblk):
            means_ref[n:n + 1, :] = jnp.mean(k_ref[0, n * blk:(n + 1) * blk, :], axis=0, keepdims=True)

    q4 = jnp.concatenate([q_ref[0, :, g * hd:(g + 1) * hd] for g in range(group)], axis=0)

    gate = lax.dot_general(means_ref[...].astype(BF16), q4, nt, preferred_element_type=F32)
    bidx = lax.broadcasted_iota(jnp.int32, gate.shape, 0)
    cand = bidx < qb
    gate = jnp.where(cand, gate, NEG_INF)
    rank = jnp.zeros(gate.shape, F32)
    for mm in range(nblk):
        gm = gate[mm:mm + 1, :]
        beats = (gm > gate) | ((gm == gate) & (mm < bidx))
        rank = rank + jnp.where(beats, 1.0, 0.0)
    sel_ref[...] = jnp.where((rank < topk) & cand, 1.0, 0.0)

    pw = min(rows, 2 * blk)
    parts = [slice(i, i + pw) for i in range(0, rows, pw)]
    qs = [q4[c, :] for c in parts]

    kpos = lax.broadcasted_iota(jnp.int32, (blk, pw), 0)
    qpos = lax.broadcasted_iota(jnp.int32, (blk, pw), 1) % blk
    k_own = k_ref[0, pl.ds(qb * blk, blk), :].astype(BF16)
    v_own = v_ref[0, pl.ds(qb * blk, blk), :].astype(BF16)
    ss = [lax.dot_general(k_own, qp, nt, preferred_element_type=F32) * scale for qp in qs]
    for c, s in zip(parts, ss):
        s = jnp.where(kpos <= qpos, s, NEG_INF)
        m0 = jnp.max(s, axis=0, keepdims=True)
        p = jnp.exp(s - m0)
        m_ref[:, c] = m0
        l_ref[:, c] = jnp.sum(p, axis=0, keepdims=True)
        acc_ref[:, c] = lax.dot_general(v_own, p.astype(BF16), tn, preferred_element_type=F32)

    def past(n, carry):
        kb = k_ref[0, pl.ds(n * blk, blk), :].astype(BF16)
        vb = v_ref[0, pl.ds(n * blk, blk), :].astype(BF16)
        sbs = [lax.dot_general(kb, qp, nt, preferred_element_type=F32) * scale for qp in qs]
        for c, sb in zip(parts, sbs):
            on = sel_ref[pl.ds(n, 1), c] > 0.5
            sb = jnp.where(on, sb, NEG_INF)
            m_old = m_ref[:, c]
            m_new = jnp.maximum(m_old, jnp.max(sb, axis=0, keepdims=True))
            pb = jnp.where(on, jnp.exp(sb - m_new), 0.0)
            alpha = jnp.exp(m_old - m_new)
            m_ref[:, c] = m_new
            l_ref[:, c] = alpha * l_ref[:, c] + jnp.sum(pb, axis=0, keepdims=True)
            acc_ref[:, c] = alpha * acc_ref[:, c] + lax.dot_general(vb, pb.astype(BF16), tn,
                                                                      preferred_element_type=F32)
        return carry

    lax.fori_loop(0, qb, past, 0)

    out = (acc_ref[...] / l_ref[...]).T
    for g in range(group):
        o_ref[0, :, g * hd:(g + 1) * hd] = out[g * blk:(g + 1) * blk, :].astype(o_ref.dtype)


def moba_prompt(q, k, v, cfg):
    bsz, seq, _ = q.shape
    blk, hd, group, kvh = cfg.moba_block, cfg.head_dim, cfg.group, cfg.n_kv_heads
    assert seq % blk == 0
    nblk = seq // blk
    rows = group * blk
    nsel = -(-nblk // 8) * 8
    body = functools.partial(_moba_prompt_body, blk=blk, nblk=nblk, group=group, hd=hd, topk=cfg.moba_topk,
                             scale=hd ** -0.5)
    return pl.pallas_call(
        body,
        grid=(bsz, kvh, nblk),
        in_specs=[
            pl.BlockSpec((1, blk, group * hd), lambda b, kh, i: (b, i, kh)),
            pl.BlockSpec((1, seq, hd), lambda b, kh, i: (b, 0, kh)),
            pl.BlockSpec((1, seq, hd), lambda b, kh, i: (b, 0, kh)),
        ],
        out_specs=pl.BlockSpec((1, blk, group * hd), lambda b, kh, i: (b, i, kh)),
        out_shape=jax.ShapeDtypeStruct(q.shape, BF16),
        scratch_shapes=[
            pltpu.VMEM((nblk, hd), F32),
            pltpu.VMEM((nblk, rows), F32),
            pltpu.VMEM((1, rows), F32),
            pltpu.VMEM((1, rows), F32),
            pltpu.VMEM((hd, rows), F32),
        ],
        compiler_params=_params("parallel", "parallel", "arbitrary"),
        name="moba_prompt",
    )(q, k, v)


def _block_means_body(pt_ref, *refs, inv_rows):
    k_refs, o_ref = refs[:-1], refs[-1]
    o_ref[0, 0] = functools.reduce(jnp.add, [jnp.sum(k_ref[0], axis=0) for k_ref in k_refs]) * inv_rows


def _page_specs(ppb, ps, kvh, hd):
    return [pl.BlockSpec((1, ps, kvh, hd), functools.partial(lambda p, b, n, pt: (pt[b, n * ppb + p], 0, 0, 0), p))
            for p in range(ppb)]


def block_means(cache_k, page_table, n_full, ppb, cfg):
    db = page_table.shape[0]
    _, ps, kvh, hd = cache_k.shape
    return pl.pallas_call(
        functools.partial(_block_means_body, inv_rows=1.0 / (ps * ppb)),
        grid_spec=pltpu.PrefetchScalarGridSpec(
            num_scalar_prefetch=1,
            grid=(db, n_full),
            in_specs=_page_specs(ppb, ps, kvh, hd),
            out_specs=pl.BlockSpec((1, 1, kvh, hd), lambda b, n, pt: (b, n, 0, 0)),
        ),
        out_shape=jax.ShapeDtypeStruct((db, n_full, kvh, hd), F32),
        compiler_params=_params("parallel", "parallel"),
        name="block_means",
    )(page_table, *([cache_k] * ppb))


def _moba_select_body(q_ref, means_ref, sel_ref, *, topk, kvh):
    nt = (((1,), (1,)), ((), ()))
    gate = jnp.concatenate(
        [lax.dot_general(q_ref[0, kh], means_ref[0, :, kh, :].astype(BF16), nt, preferred_element_type=F32)
         for kh in range(kvh)], axis=0)
    lane = lax.broadcasted_iota(jnp.int32, gate.shape, 1)
    nb = gate.shape[1]
    sel = jnp.zeros(gate.shape, F32)
    for _ in range(topk):
        m = jnp.max(gate, axis=1, keepdims=True)
        first = jnp.min(jnp.where(gate == m, lane, nb), axis=1, keepdims=True)
        pick = lane == first
        sel = jnp.where(pick, 1.0, sel)
        gate = jnp.where(pick, -3.0e38, gate)
    sel_ref[0] = sel


def moba_select(q4, means, cfg):
    db, kvh, rpk, hd = q4.shape
    n_full = means.shape[1]
    rows = kvh * rpk
    return pl.pallas_call(
        functools.partial(_moba_select_body, topk=min(cfg.moba_topk, n_full), kvh=kvh),
        grid=(db,),
        in_specs=[pl.BlockSpec((1, kvh, rpk, hd), lambda b: (b, 0, 0, 0)),
                  pl.BlockSpec((1, n_full, kvh, hd), lambda b: (b, 0, 0, 0))],
        out_specs=pl.BlockSpec((1, rows, n_full), lambda b: (b, 0, 0)),
        out_shape=jax.ShapeDtypeStruct((db, rows, n_full), F32),
        compiler_params=_params("parallel"),
        name="moba_select",
    )(q4, means)


def _moba_sample_body(pt_ref, *refs, ppb, scale, t_new, kvh, rpk):
    q_ref, sel_ref, kn_ref, vn_ref = refs[0], refs[1 + 2 * ppb], refs[2 + 2 * ppb], refs[3 + 2 * ppb]
    k_refs, v_refs = refs[1:1 + ppb], refs[1 + ppb:1 + 2 * ppb]
    o_ref, m_ref, l_ref, acc_ref = refs[4 + 2 * ppb:]
    n = pl.program_id(1)
    nt = (((1,), (1,)), ((), ()))
    rows = q_ref.shape[1]
    ps = kn_ref.shape[1] // kvh

    @pl.when(n == 0)
    def _():
        m_ref[...] = jnp.full(m_ref.shape, NEG_INF, F32)
        l_ref[...] = jnp.zeros(l_ref.shape, F32)
        acc_ref[...] = jnp.zeros(acc_ref.shape, F32)

    def head_rows(page_ref, kh):
        return page_ref[0, pl.ds(kh, ps, stride=kvh), :].astype(BF16)

    def absorb(k_pages, v_pages, on):
        s = jnp.concatenate(
            [jnp.concatenate(
                [lax.dot_general(q_ref[0, kh * rpk:(kh + 1) * rpk, :], head_rows(kp, kh), nt,
                                 preferred_element_type=F32) for kp in k_pages], axis=1)
             for kh in range(kvh)], axis=0) * scale
        s = jnp.where(on, s, NEG_INF)
        m_old = m_ref[...]
        m_new = jnp.maximum(m_old, jnp.max(s, axis=1, keepdims=True))
        p = jnp.where(on, jnp.exp(s - m_new), 0.0)
        alpha = jnp.exp(m_old - m_new)
        m_ref[...] = m_new
        l_ref[...] = alpha * l_ref[...] + jnp.sum(p, axis=1, keepdims=True)
        pb = p.astype(BF16)
        pv = jnp.concatenate(
            [functools.reduce(jnp.add, [
                jnp.dot(pb[kh * rpk:(kh + 1) * rpk, i * ps:(i + 1) * ps], head_rows(vp, kh),
                        preferred_element_type=F32) for i, vp in enumerate(v_pages)])
             for kh in range(kvh)], axis=0)
        acc_ref[...] = alpha * acc_ref[...] + pv

    absorb(k_refs, v_refs, sel_ref[0, 0] > 0.5)

    @pl.when(n == pl.num_programs(1) - 1)
    def _():
        tq = lax.broadcasted_iota(jnp.int32, (rows, ps), 0) % t_new
        tk = lax.broadcasted_iota(jnp.int32, (rows, ps), 1)
        absorb([kn_ref], [vn_ref], tk <= tq)
        o_ref[0] = acc_ref[...] / l_ref[...]


def moba_sample(q, cache_k, cache_v, page_table, sel, k_new, v_new, t_new, ppb, cfg):
    db, rows, hd = q.shape
    kvh = cfg.n_kv_heads
    n_full = sel.shape[1]
    page_rows = cache_k.shape[1]
    body = functools.partial(_moba_sample_body, ppb=ppb, scale=hd ** -0.5, t_new=t_new, kvh=kvh, rpk=rows // kvh)
    page_specs = [
        pl.BlockSpec((1, page_rows, hd), functools.partial(lambda p, b, n, pt: (pt[b, n * ppb + p], 0, 0), p))
        for p in range(ppb)]
    new_spec = pl.BlockSpec((1, page_rows, hd), lambda b, n, pt: (b, 0, 0))
    return pl.pallas_call(
        body,
        grid_spec=pltpu.PrefetchScalarGridSpec(
            num_scalar_prefetch=1,
            grid=(db, n_full),
            in_specs=([pl.BlockSpec((1, rows, hd), lambda b, n, pt: (b, 0, 0))] + page_specs + page_specs
                      + [pl.BlockSpec((1, 1, rows, 1), lambda b, n, pt: (b, n, 0, 0)), new_spec, new_spec]),
            out_specs=pl.BlockSpec((1, rows, hd), lambda b, n, pt: (b, 0, 0)),
            scratch_shapes=[
                pltpu.VMEM((rows, 1), F32),
                pltpu.VMEM((rows, 1), F32),
                pltpu.VMEM((rows, hd), F32),
            ],
        ),
        out_shape=jax.ShapeDtypeStruct((db, rows, hd), F32),
        compiler_params=_params("parallel", "arbitrary"),
        name="moba_sample",
    )(page_table, q, *([cache_k] * ppb), *([cache_v] * ppb), sel, k_new, v_new)


def _cast_body(x_ref, o_ref):
    o_ref[...] = x_ref[0].astype(o_ref.dtype)


def cast_layer(w, layer, cols=None):
    _, r, c = w.shape
    cols = c if cols is None else cols
    tr = next(t for t in (512, 256, 128, 64, 32, 16) if r % t == 0 and t * cols * 4 <= 8 * 1024 * 1024)
    return pl.pallas_call(
        _cast_body,
        grid=(r // tr,),
        in_specs=[pl.BlockSpec((1, tr, cols), lambda i: (layer, i, 0))],
        out_specs=pl.BlockSpec((tr, cols), lambda i: (i, 0)),
        out_shape=jax.ShapeDtypeStruct((r, cols), BF16),
        compiler_params=_params("parallel"),
        name="cast_bf16",
    )(w)


def _cast_tail_body(x_ref, o_ref, *, valid):
    lane = lax.broadcasted_iota(jnp.int32, o_ref.shape, 1)
    o_ref[...] = jnp.where(lane < valid, x_ref[0], 0.0).astype(o_ref.dtype)


def cast_tail(w, layer, col0):
    _, r, c = w.shape
    assert col0 % LANES == 0 and 0 < c - col0 <= LANES
    tr = _pick(r, (512, 256, 128, 64))
    return pl.pallas_call(
        functools.partial(_cast_tail_body, valid=c - col0),
        grid=(r // tr,),
        in_specs=[pl.BlockSpec((1, tr, LANES), lambda i: (layer, i, col0 // LANES))],
        out_specs=pl.BlockSpec((tr, LANES), lambda i: (i, 0)),
        out_shape=jax.ShapeDtypeStruct((r, LANES), BF16),
        compiler_params=_params("parallel"),
        name="cast_tail_bf16",
    )(w)


class Weights(NamedTuple):
    norm_mix: jax.Array
    norm_ffn: jax.Array
    w_in: jax.Array
    w_z: jax.Array
    w_gate2: jax.Array
    b_gate: jax.Array
    gla_norm: jax.Array
    w_out: jax.Array
    norm_kv: jax.Array
    w_kv: jax.Array
    k_norm: jax.Array
    w_q: jax.Array
    q_norm: jax.Array
    w_o: jax.Array
    peer_w_query: tuple
    peer_keys: tuple
    peer_u: tuple
    peer_v: tuple


def _prep_weights(norm_mix, norm_ffn, gla_w_in, gla_w_gate2, gla_b_gate, gla_norm, gla_w_out, norm_kv, w_kv,
                  k_norm, w_q, q_norm, w_o, peer_w_query, peer_sub_keys, peer_u, peer_v, cfg):
    main = 2 * cfg.gla_dk_tot + 2 * cfg.d_model
    rank = cfg.gla_gate_rank
    depth = norm_mix.shape[0]
    return Weights(
        norm_mix=norm_mix,
        norm_ffn=norm_ffn,
        w_in=cast_layer(gla_w_in, 0, cols=main),
        w_z=cast_tail(gla_w_in, 0, main),
        w_gate2=jnp.pad(gla_w_gate2[0], ((0, LANES - rank), (0, 0))).astype(BF16),
        b_gate=gla_b_gate[0][None, :],
        gla_norm=gla_norm[0].reshape(1, -1),
        w_out=gla_w_out[0].astype(BF16),
        norm_kv=norm_kv,
        w_kv=w_kv.astype(BF16),
        k_norm=k_norm[None, :],
        w_q=w_q[0].astype(BF16),
        q_norm=q_norm[0][None, :],
        w_o=w_o[0].astype(BF16),
        peer_w_query=tuple(peer_w_query[l].astype(BF16) for l in range(depth)),
        peer_keys=tuple(peer_sub_keys[l].reshape(2 * cfg.peer_heads, cfg.peer_n_keys, -1).astype(BF16)
                        for l in range(depth)),
        peer_u=tuple(cast_layer(peer_u, l) for l in range(depth)),
        peer_v=tuple(cast_layer(peer_v, l) for l in range(depth)),
    )


def _trunk(x, pos0, s0, attend, w, cfg):
    bsz, seq, d = x.shape
    n = bsz * seq
    hd = cfg.head_dim
    xf = x.reshape(n, d)

    (hn,) = rmsnorm_cast(xf, w.norm_mix[0:1])
    proj = matmul(hn, w.w_in, name="gla_in")
    z = matmul(hn, w.w_z, name="gla_gate_in")
    og, s_fin = gla(proj.reshape(bsz, seq, -1), z.reshape(bsz, seq, -1), w.w_gate2, w.b_gate, w.gla_norm, s0, cfg)
    h = matmul(og.reshape(n, -1), w.w_out, residual=xf, name="gla_out")
    h = peer_block(h, w.norm_ffn[0], w.peer_w_query[0], w.peer_keys[0], w.peer_u[0], w.peer_v[0], cfg)

    kvn, hn1 = rmsnorm_cast(h, jnp.stack([w.norm_kv, w.norm_mix[1]]))
    kv = matmul(kvn, w.w_kv, name="kv_proj")
    cos, sin = _rope_tables(pos0, seq, hd)
    (k,) = head_norm_rope(kv[:, :cfg.kv_dim], w.k_norm, cos, sin, seq, (F32,), hd)
    v = kv[:, cfg.kv_dim:]
    (q,) = head_norm_rope(matmul(hn1, w.w_q, name="q_proj"), w.q_norm, cos, sin, seq, (BF16,), hd)

    o = attend(q, k, v)
    h = matmul(o, w.w_o, residual=h, name="attn_out")
    h = peer_block(h, w.norm_ffn[1], w.peer_w_query[1], w.peer_keys[1], w.peer_u[1], w.peer_v[1], cfg)
    return (h.reshape(bsz, seq, d), s_fin[None],
            k.reshape(bsz, seq, cfg.n_kv_heads, hd), v.reshape(bsz, seq, cfg.n_kv_heads, hd))


def _attend_prompt(bsz, seq, cfg):
    def attend(q, k, v):
        n = q.shape[0]
        o = moba_prompt(q.reshape(bsz, seq, -1), k.reshape(bsz, seq, -1), v.reshape(bsz, seq, -1), cfg)
        return o.reshape(n, -1)
    return attend


def _attend_sample(db, t_new, cache_k, cache_v, page_table, past_len, cfg):
    hd, kvh, group, nh = cfg.head_dim, cfg.n_kv_heads, cfg.group, cfg.n_heads
    blk, ps = cfg.moba_block, cfg.page_size
    ppb = blk // ps
    n_full = past_len // blk
    assert past_len % blk == 0 and n_full > 0 and t_new <= ps
    pt = page_table[:, :n_full * ppb]
    ck = cache_k.reshape(cache_k.shape[0], ps * kvh, hd)
    cv = cache_v.reshape(cache_v.shape[0], ps * kvh, hd)

    def attend(q, k, v):
        qr = q.reshape(db, t_new, nh, hd).transpose(0, 2, 1, 3).reshape(db, nh * t_new, hd)
        means = block_means(cache_k, pt, n_full, ppb, cfg)
        sel = moba_select(qr.reshape(db, kvh, group * t_new, hd), means, cfg)
        sel = sel.transpose(0, 2, 1)[..., None]
        pad = ((0, 0), (0, ps - t_new), (0, 0), (0, 0))
        k_new = jnp.pad(k.reshape(db, t_new, kvh, hd), pad).reshape(db, ps * kvh, hd)
        v_new = jnp.pad(v.reshape(db, t_new, kvh, hd), pad).reshape(db, ps * kvh, hd)
        o = moba_sample(qr, ck, cv, pt, sel, k_new, v_new, t_new, ppb, cfg)
        o = o.reshape(db, nh, t_new, hd).transpose(0, 2, 1, 3).reshape(db * t_new, nh * hd)
        return o.astype(BF16)
    return attend


def _forward(x_prompt, x_sample, cache_k, cache_v, state_gla, page_table, weights, past_len, cfg):
    w = _prep_weights(*weights, cfg)
    bsz, seq, _ = x_prompt.shape
    db, t_new, _ = x_sample.shape
    s0_p = jnp.zeros((bsz, cfg.gla_heads, cfg.gla_dk, cfg.gla_dv), state_gla.dtype)
    y_p, st_p, k_p, v_p = _trunk(x_prompt, 0, s0_p, _attend_prompt(bsz, seq, cfg), w, cfg)
    attend_s = _attend_sample(db, t_new, cache_k, cache_v, page_table, past_len, cfg)
    y_s, st_s, k_s, v_s = _trunk(x_sample, past_len, state_gla[0], attend_s, w, cfg)
    return (y_p, y_s, st_p, st_s, k_p, v_p, k_s, v_s)


def kernel(x_prompt, x_sample, cache_k, cache_v, state_gla, page_table, norm_mix, norm_ffn, gla_w_in, gla_w_gate2, gla_b_gate, gla_norm, gla_w_out, norm_kv, w_kv, k_norm, w_q, q_norm, w_o, peer_w_query, peer_sub_keys, peer_u, peer_v):
    weights = (norm_mix, norm_ffn, gla_w_in, gla_w_gate2, gla_b_gate, gla_norm, gla_w_out, norm_kv, w_kv, k_norm,
               w_q, q_norm, w_o, peer_w_query, peer_sub_keys, peer_u, peer_v)
    past_len = page_table.shape[1] * CFG.page_size
    return _forward(x_prompt, x_sample, cache_k, cache_v, state_gla, page_table, weights, past_len, CFG)
```

```python
import functools
import math
from typing import NamedTuple

import numpy as np
import jax
import jax.numpy as jnp
from jax import lax
from jax.experimental import pallas as pl
from jax.experimental.pallas import tpu as pltpu

F32 = jnp.float32
BF16 = jnp.bfloat16

LANES = 128
MXU_COLS = 256
RMS_EPS = 1e-6
NEG_INF = -1e30
ROPE_THETA = 10000.0
VMEM_LIMIT_BYTES = 56 * 1024 * 1024


class Cfg(NamedTuple):
    d_model: int = 4096
    gla_heads: int = 4
    gla_gate_rank: int = 16
    gla_gate_tau: float = 16.0
    gla_chunk: int = 64
    head_dim: int = 128
    n_kv_heads: int = 8
    moba_block: int = 256
    moba_topk: int = 3
    page_size: int = 128
    peer_heads: int = 8
    peer_n_keys: int = 128
    peer_topk: int = 16
    peer_key_dim: int = 256

    @property
    def gla_dk_tot(self):
        return self.d_model // 2

    @property
    def gla_dk(self):
        return self.gla_dk_tot // self.gla_heads

    @property
    def gla_dv(self):
        return self.d_model // self.gla_heads

    @property
    def n_heads(self):
        return self.d_model // self.head_dim

    @property
    def group(self):
        return self.n_heads // self.n_kv_heads

    @property
    def kv_dim(self):
        return self.n_kv_heads * self.head_dim


CFG = Cfg()


def _params(*sem, flags=None):
    return pltpu.CompilerParams(dimension_semantics=sem, vmem_limit_bytes=VMEM_LIMIT_BYTES, flags=flags)


def _pick(n, prefs):
    for p in prefs:
        if n % p == 0:
            return p
    return n


def _rmsnorm_body(x_ref, g_ref, *o_refs, transposed):
    x = x_ref[...]
    y = x * lax.rsqrt(jnp.mean(x * x, axis=-1, keepdims=True) + RMS_EPS)
    ng = g_ref.shape[0]
    for i in range(ng):
        o_refs[i][...] = (y * g_ref[i:i + 1, :]).astype(o_refs[i].dtype)
    if transposed:
        o_refs[ng][...] = (y * g_ref[0:1, :]).T.astype(o_refs[ng].dtype)


def rmsnorm_cast(x, gains, transposed=False):
    m, d = x.shape
    g = gains.shape[0]
    tm = _pick(m, (256, 128, 64))
    out_specs = [pl.BlockSpec((tm, d), lambda i: (i, 0))] * g
    out_shape = [jax.ShapeDtypeStruct((m, d), BF16)] * g
    if transposed:
        out_specs = out_specs + [pl.BlockSpec((d, tm), lambda i: (0, i))]
        out_shape = out_shape + [jax.ShapeDtypeStruct((d, m), BF16)]
    return pl.pallas_call(
        functools.partial(_rmsnorm_body, transposed=transposed),
        grid=(m // tm,),
        in_specs=[pl.BlockSpec((tm, d), lambda i: (i, 0)), pl.BlockSpec((g, d), lambda i: (0, 0))],
        out_specs=out_specs,
        out_shape=out_shape,
        compiler_params=_params("parallel"),
        name="rmsnorm_cast",
    )(x, gains)


def _mm_body(x_ref, w_ref, o_ref):
    o_ref[...] = jnp.dot(x_ref[...], w_ref[...], preferred_element_type=F32).astype(o_ref.dtype)


def _mm_res_body(x_ref, w_ref, r_ref, o_ref):
    o_ref[...] = (r_ref[...] + jnp.dot(x_ref[...], w_ref[...], preferred_element_type=F32)).astype(o_ref.dtype)


def matmul(x, w, residual=None, out_dtype=F32, name="matmul"):
    m, k = x.shape
    n = w.shape[1]
    tm = _pick(m, (1024, 512, 256, 128, 64))
    tn = _pick(n, (1024, 512, 256, 128))
    in_specs = [pl.BlockSpec((tm, k), lambda j, i: (i, 0)), pl.BlockSpec((k, tn), lambda j, i: (0, j))]
    args = [x, w]
    body = _mm_body
    if residual is not None:
        in_specs.append(pl.BlockSpec((tm, tn), lambda j, i: (i, j)))
        args.append(residual)
        body = _mm_res_body
    return pl.pallas_call(
        body,
        grid=(n // tn, m // tm),
        in_specs=in_specs,
        out_specs=pl.BlockSpec((tm, tn), lambda j, i: (i, j)),
        out_shape=jax.ShapeDtypeStruct((m, n), out_dtype),
        compiler_params=_params("parallel", "parallel"),
        name=name,
    )(*args)


def _cumsum_rows(g):
    c = g.shape[0]
    row = lax.broadcasted_iota(jnp.int32, g.shape, 0)
    b = g
    s = 1
    while s < c:
        b = b + jnp.where(row >= s, pltpu.roll(b, s, axis=0), 0.0)
        s *= 2
    return b


def _log_sigmoid(x):
    return -(jnp.maximum(-x, 0.0) + jnp.log1p(jnp.exp(-jnp.abs(x))))


def _gla_body(q_ref, k_ref, v_ref, r_ref, z_ref, wg_ref, bg_ref, gn_ref, s0_ref, og_ref, sf_ref, st_ref,
              *, chunk, nsub, q_scale, inv_tau):
    t = pl.program_id(2)

    @pl.when(t == 0)
    def _():
        st_ref[...] = s0_ref[0, 0].T

    za = jnp.dot(z_ref[0].astype(BF16), wg_ref[...], preferred_element_type=F32) + bg_ref[...]
    log_a = _log_sigmoid(za) * inv_tau
    tril = (lax.broadcasted_iota(jnp.int32, (chunk, chunk), 0) >= lax.broadcasted_iota(jnp.int32, (chunk, chunk), 1))
    nt = (((1,), (1,)), ((), ()))
    tn = (((0,), (0,)), ((), ()))
    for i in range(nsub):
        sl = slice(i * chunk, (i + 1) * chunk)
        b = _cumsum_rows(log_a[sl])
        b_last = b[chunk - 1:chunk, :]
        q = q_ref[0, sl, :] * q_scale
        k = k_ref[0, sl, :]
        v = v_ref[0, sl, :].astype(BF16)
        qe = (q * jnp.exp(b)).astype(BF16)
        ke = (k * jnp.exp(-b)).astype(BF16)
        kd = (k * jnp.exp(b_last - b)).astype(BF16)
        att = lax.dot_general(qe, ke, nt, preferred_element_type=F32)
        att = jnp.where(tril, att, 0.0).astype(BF16)
        st = st_ref[...]
        o = lax.dot_general(qe, st.astype(BF16), nt, preferred_element_type=F32)
        o = o + jnp.dot(att, v, preferred_element_type=F32)
        st_ref[...] = st * jnp.exp(b_last) + lax.dot_general(v, kd, tn, preferred_element_type=F32)
        on = o * lax.rsqrt(jnp.mean(o * o, axis=-1, keepdims=True) + RMS_EPS) * gn_ref[...]
        r = r_ref[0, sl, :]
        og_ref[0, sl, :] = (on * (r * jax.nn.sigmoid(r))).astype(og_ref.dtype)

    @pl.when(t == pl.num_programs(2) - 1)
    def _():
        sf_ref[0, 0] = st_ref[...].T


def gla(proj, z, wg, bg, gn, s0, cfg):
    bsz, seq, _ = proj.shape
    h, dk, dv = cfg.gla_heads, cfg.gla_dk, cfg.gla_dv
    chunk = math.gcd(seq, cfg.gla_chunk)
    rows = _pick(seq, (4 * chunk, 2 * chunk, chunk))
    nsub = rows // chunk
    kq, kk, kv, kr = 0, h, (2 * h * dk) // dv, (2 * h * dk) // dv + h
    body = functools.partial(_gla_body, chunk=chunk, nsub=nsub, q_scale=dk ** -0.5, inv_tau=1.0 / cfg.gla_gate_tau)
    return pl.pallas_call(
        body,
        grid=(bsz, h, seq // rows),
        in_specs=[
            pl.BlockSpec((1, rows, dk), lambda b, hh, t: (b, t, kq + hh)),
            pl.BlockSpec((1, rows, dk), lambda b, hh, t: (b, t, kk + hh)),
            pl.BlockSpec((1, rows, dv), lambda b, hh, t: (b, t, kv + hh)),
            pl.BlockSpec((1, rows, dv), lambda b, hh, t: (b, t, kr + hh)),
            pl.BlockSpec((1, rows, LANES), lambda b, hh, t: (b, t, 0)),
            pl.BlockSpec((LANES, dk), lambda b, hh, t: (0, hh)),
            pl.BlockSpec((1, dk), lambda b, hh, t: (0, hh)),
            pl.BlockSpec((1, dv), lambda b, hh, t: (0, hh)),
            pl.BlockSpec((1, 1, dk, dv), lambda b, hh, t: (b, hh, 0, 0)),
        ],
        out_specs=[
            pl.BlockSpec((1, rows, dv), lambda b, hh, t: (b, t, hh)),
            pl.BlockSpec((1, 1, dk, dv), lambda b, hh, t: (b, hh, 0, 0)),
        ],
        out_shape=[
            jax.ShapeDtypeStruct((bsz, seq, h * dv), BF16),
            jax.ShapeDtypeStruct((bsz, h, dk, dv), F32),
        ],
        scratch_shapes=[pltpu.VMEM((dv, dk), F32)],
        compiler_params=_params("parallel", "parallel", "arbitrary"),
        name="gla",
    )(proj, proj, proj, proj, z, wg, bg, gn, s0)


def _headrope_body(x_ref, g_ref, cos_ref, sin_ref, *o_refs, nh, hd):
    for hh in range(nh):
        x = x_ref[:, hh * hd:(hh + 1) * hd]
        y = x * lax.rsqrt(jnp.mean(x * x, axis=-1, keepdims=True) + RMS_EPS) * g_ref[...]
        out = y * cos_ref[...] + pltpu.roll(y, hd // 2, axis=1) * sin_ref[...]
        for o_ref in o_refs:
            o_ref[:, hh * hd:(hh + 1) * hd] = out.astype(o_ref.dtype)


def head_norm_rope(x, gain, cos, sin, seq, out_dtypes, hd):
    n, width = x.shape
    nh = width // hd
    tm = _pick(seq, (256, 128, 64, 32, 16, 8))
    per = seq // tm
    return pl.pallas_call(
        functools.partial(_headrope_body, nh=nh, hd=hd),
        grid=(n // tm,),
        in_specs=[
            pl.BlockSpec((tm, width), lambda i: (i, 0)),
            pl.BlockSpec((1, hd), lambda i: (0, 0)),
            pl.BlockSpec((tm, hd), lambda i: (i % per, 0)),
            pl.BlockSpec((tm, hd), lambda i: (i % per, 0)),
        ],
        out_specs=[pl.BlockSpec((tm, width), lambda i: (i, 0)) for _ in out_dtypes],
        out_shape=[jax.ShapeDtypeStruct((n, width), dt) for dt in out_dtypes],
        compiler_params=_params("parallel"),
        name="head_norm_rope",
    )(x, gain, cos, sin)


def _rope_tables(pos0, seq, hd):
    half = hd // 2
    inv = ROPE_THETA ** (-np.arange(half, dtype=np.float64) / half)
    ang = (pos0 + np.arange(seq, dtype=np.float64))[:, None] * inv[None, :]
    cos, sin = np.cos(ang), np.sin(ang)
    return (jnp.asarray(np.concatenate([cos, cos], axis=1), F32),
            jnp.asarray(np.concatenate([-sin, sin], axis=1), F32))


def _top_values(s, k):
    vals = []
    cur = s
    for i in range(k):
        m = jnp.max(cur, axis=0, keepdims=True)
        vals.append(m)
        if i + 1 < k:
            cur = jnp.where(cur == m, NEG_INF, cur)
    return vals


ROW_TAU, ROW_M1, ROW_M2, ROW_INVZ = range(4)


def _peer_route_body(q_ref, keys_ref, s1_ref, s2_ref, rows_ref, *, heads, nkeys, half, topk):
    nt = (((1,), (1,)), ((), ()))
    tm = q_ref.shape[0]
    sub = 8
    row = lax.broadcasted_iota(jnp.int32, (sub, tm), 0)
    for hh in range(heads):
        st = []
        for p in range(2):
            g = 2 * hh + p
            qg = q_ref[:, g * half:(g + 1) * half].astype(BF16)
            st.append(lax.dot_general(keys_ref[g], qg, nt, preferred_element_type=F32))
        s1, s2 = st
        v1 = _top_values(s1, topk)
        v2 = _top_values(s2, topk)
        v2g = []
        for g0 in range(0, topk, sub):
            grp = jnp.full((sub, tm), NEG_INF, F32)
            for b in range(g0, min(g0 + sub, topk)):
                grp = jnp.where(row == b - g0, v2[b], grp)
            v2g.append(grp)
        cands = []
        for a in range(topk):
            bmax = topk // (a + 1)
            for gi, grp in enumerate(v2g):
                if gi * sub < bmax:
                    cands.append(jnp.where(row < bmax - gi * sub, v1[a] + grp, NEG_INF))
        cur = cands
        tau = None
        for i in range(topk):
            tau = functools.reduce(jnp.maximum, [jnp.max(c, axis=0, keepdims=True) for c in cur])
            if i + 1 < topk:
                cur = [jnp.where(c == tau, NEG_INF, c) for c in cur]
        m1, m2 = v1[0], v2[0]
        mx = m1 + m2
        z = functools.reduce(
            jnp.add, [jnp.sum(jnp.where(c >= tau, jnp.exp(c - mx), 0.0), axis=0, keepdims=True) for c in cands])
        s1_ref[hh] = s1
        s2_ref[hh] = s2
        rows_ref[ROW_TAU, hh:hh + 1, :] = tau
        rows_ref[ROW_M1, hh:hh + 1, :] = m1
        rows_ref[ROW_M2, hh:hh + 1, :] = m2
        rows_ref[ROW_INVZ, hh:hh + 1, :] = 1.0 / z


def peer_route(q, keys, cfg):
    n = q.shape[0]
    heads, nkeys, half = cfg.peer_heads, cfg.peer_n_keys, cfg.peer_key_dim // 2
    tm = _pick(n, (256, 128))
    tab = jax.ShapeDtypeStruct((heads, nkeys, n), F32)
    tab_spec = pl.BlockSpec((heads, nkeys, tm), lambda i: (0, 0, i))
    return pl.pallas_call(
        functools.partial(_peer_route_body, heads=heads, nkeys=nkeys, half=half, topk=cfg.peer_topk),
        grid=(n // tm,),
        in_specs=[pl.BlockSpec((tm, q.shape[1]), lambda i: (i, 0)),
                  pl.BlockSpec(keys.shape, lambda i: (0, 0, 0))],
        out_specs=[tab_spec, tab_spec, pl.BlockSpec((4, heads, tm), lambda i: (0, 0, i))],
        out_shape=[tab, tab, jax.ShapeDtypeStruct((4, heads, n), F32)],
        compiler_params=_params("parallel"),
        name="peer_route",
    )(q, keys)


def _gelu(x):
    return 0.5 * x * (1.0 + lax.erf(x * (1.0 / math.sqrt(2.0))))


SUBLANES = 8
GATE_LANES = 512


def _peer_row_tables(bc_ref, jt, s1_ref, rows_ref, *, heads, nsub):
    tm = bc_ref.shape[-1]
    for r in range(nsub):
        i1 = jt * nsub + r
        for hh in range(heads):
            s1row = s1_ref[hh, pl.ds(i1, 1), :]
            crow = jnp.exp(s1row - rows_ref[ROW_M1, hh:hh + 1, :]) * rows_ref[ROW_INVZ, hh:hh + 1, :]
            bc_ref[hh, r, 0] = jnp.broadcast_to(s1row, (SUBLANES, tm))
            bc_ref[hh, r, 1] = jnp.broadcast_to(crow, (SUBLANES, tm))


def _peer_gate_passes(dst_ref, s2_ref, e2_ref, taub_ref, bc_ref, *, heads, nkeys, nsub):
    tm = dst_ref.shape[-1]
    tw = min(tm, GATE_LANES)

    def one(c, l):
        rows, cols = slice(c, c + SUBLANES), slice(l, l + tw)
        ws = [None] * nsub
        for hh in range(heads):
            s2t, e2t, tb = s2_ref[hh, rows, cols], e2_ref[hh, rows, cols], taub_ref[hh, :, cols]
            for r in range(nsub):
                hit = (s2t + bc_ref[hh, r, 0, :, cols]) >= tb
                term = jnp.where(hit, e2t, 0.0) * bc_ref[hh, r, 1, :, cols]
                ws[r] = term if ws[r] is None else ws[r] + term
        for r in range(nsub):
            dst_ref[r * nkeys + c:r * nkeys + c + SUBLANES, cols] = ws[r]

    return [functools.partial(one, c, l) for c in range(0, nkeys, SUBLANES) for l in range(0, tm, tw)]


def _peer_dense_body(xt_ref, u_ref, v_ref, s1_ref, s2_ref, rows_ref, o_ref, e2_ref, taub_ref, bc_ref,
                     gate_a_ref, gate_b_ref, ht_ref, *, heads, nkeys, nsub):
    j = pl.program_id(1)
    last = pl.num_programs(1) - 1
    row_tables = functools.partial(_peer_row_tables, bc_ref, s1_ref=s1_ref, rows_ref=rows_ref, heads=heads, nsub=nsub)
    gate_passes = functools.partial(_peer_gate_passes, s2_ref=s2_ref, e2_ref=e2_ref, taub_ref=taub_ref, bc_ref=bc_ref,
                                    heads=heads, nkeys=nkeys, nsub=nsub)
    tm = xt_ref.shape[1]
    d = o_ref.shape[1]

    @pl.when(j == 0)
    def _():
        o_ref[...] = jnp.zeros_like(o_ref)
        for hh in range(heads):
            e2_ref[hh] = jnp.exp(s2_ref[hh] - rows_ref[ROW_M2, hh:hh + 1, :])
            taub_ref[hh] = jnp.broadcast_to(rows_ref[ROW_TAU, hh:hh + 1, :], (SUBLANES, tm))
        row_tables(0)
        for run in gate_passes(gate_a_ref):
            run()

    def step(cur_ref, nxt_ref):
        row_tables(jnp.minimum(j + 1, last))
        passes = gate_passes(nxt_ref)
        tw = min(tm, MXU_COLS)
        n_tok, n_out = tm // tw, d // MXU_COLS
        n_first = n_tok
        first = len(passes) // 2
        quota = ([first // n_first + (i < first % n_first) for i in range(n_first)]
                 + [(len(passes) - first) // n_out + (i < (len(passes) - first) % n_out) for i in range(n_out)])
        it = iter(passes)
        for c in range(n_tok):
            cols = slice(c * tw, (c + 1) * tw)
            act = _gelu(jnp.dot(u_ref[...], xt_ref[:, cols], preferred_element_type=F32))
            ht_ref[cols, :] = (cur_ref[:, cols] * act).T.astype(BF16)
            for _ in range(quota[c]):
                next(it)()
        for p in range(n_out):
            cols = slice(p * MXU_COLS, (p + 1) * MXU_COLS)
            o_ref[:, cols] += jnp.dot(ht_ref[...], v_ref[:, cols], preferred_element_type=F32)
            for _ in range(quota[n_first + p]):
                next(it)()

    @pl.when(j % 2 == 0)
    def _():
        step(gate_a_ref, gate_b_ref)

    @pl.when(j % 2 == 1)
    def _():
        step(gate_b_ref, gate_a_ref)


def peer_dense(xt, u, v, s1, s2, rows, cfg):
    d, n = xt.shape
    e = u.shape[0]
    heads, nkeys = cfg.peer_heads, cfg.peer_n_keys
    tm = _pick(n, (1024, 512, 256, 128))
    nsub = 2
    te = nsub * nkeys
    once = dict(pipeline_mode=pl.Buffered(1))
    tab_spec = pl.BlockSpec((heads, nkeys, tm), lambda i, j: (0, 0, i), **once)
    return pl.pallas_call(
        functools.partial(_peer_dense_body, heads=heads, nkeys=nkeys, nsub=nsub),
        grid=(n // tm, e // te),
        in_specs=[
            pl.BlockSpec((d, tm), lambda i, j: (0, i), **once),
            pl.BlockSpec((te, d), lambda i, j: (j, 0)),
            pl.BlockSpec((te, d), lambda i, j: (j, 0)),
            tab_spec, tab_spec,
            pl.BlockSpec((4, heads, tm), lambda i, j: (0, 0, i), **once),
        ],
        out_specs=pl.BlockSpec((tm, d), lambda i, j: (i, 0), **once),
        out_shape=jax.ShapeDtypeStruct((n, d), F32),
        scratch_shapes=[
            pltpu.VMEM((heads, nkeys, tm), F32),
            pltpu.VMEM((heads, SUBLANES, tm), F32),
            pltpu.VMEM((heads, nsub, 2, SUBLANES, tm), F32),
            pltpu.VMEM((te, tm), F32),
            pltpu.VMEM((te, tm), F32),
            pltpu.VMEM((tm, te), BF16),
        ],
        compiler_params=_params("parallel", "arbitrary"),
        name="peer_dense",
    )(xt, u, v, s1, s2, rows)


def peer_block(h, g_ffn, w_query, keys, u, v, cfg):
    n = h.shape[0]
    npad = -(-n // LANES) * LANES
    hp = h if npad == n else jnp.pad(h, ((0, npad - n), (0, 0)))
    xn, xt = rmsnorm_cast(hp, g_ffn[None, :], transposed=True)
    q = matmul(xn, w_query, name="peer_query")
    s1, s2, rows = peer_route(q, keys, cfg)
    return h + peer_dense(xt, u, v, s1, s2, rows, cfg)[:n]


def _moba_prompt_body(q_ref, k_ref, v_ref, o_ref, means_ref, sel_ref, m_ref, l_ref, acc_ref,
                      *, blk, nblk, group, hd, topk, scale):
    qb = pl.program_id(2)
    nt = (((1,), (1,)), ((), ()))
    tn = (((0,), (0,)), ((), ()))
    rows = group * blk

    @pl.when(qb == 0)
    def _():
        for n in range(nblk):
            means_ref[n:n + 1, :] = jnp.mean(k_ref[0, n * blk:(n + 1) * blk, :], axis=0, keepdims=True)

    q4 = jnp.concatenate([q_ref[0, :, g * hd:(g + 1) * hd] for g in range(group)], axis=0)

    gate = lax.dot_general(means_ref[...].astype(BF16), q4, nt, preferred_element_type=F32)
    bidx = lax.broadcasted_iota(jnp.int32, gate.shape, 0)
    cand = bidx < qb
    gate = jnp.where(cand, gate, NEG_INF)
    rank = jnp.zeros(gate.shape, F32)
    for mm in range(nblk):
        gm = gate[mm:mm + 1, :]
        beats = (gm > gate) | ((gm == gate) & (mm < bidx))
        rank = rank + jnp.where(beats, 1.0, 0.0)
    sel_ref[...] = jnp.where((rank < topk) & cand, 1.0, 0.0)

    pw = min(rows, 2 * blk)
    parts = [slice(i, i + pw) for i in range(0, rows, pw)]
    qs = [q4[c, :] for c in parts]

    kpos = lax.broadcasted_iota(jnp.int32, (blk, pw), 0)
    qpos = lax.broadcasted_iota(jnp.int32, (blk, pw), 1) % blk
    k_own = k_ref[0, pl.ds(qb * blk, blk), :].astype(BF16)
    v_own = v_ref[0, pl.ds(qb * blk, blk), :].astype(BF16)
    ss = [lax.dot_general(k_own, qp, nt, preferred_element_type=F32) * scale for qp in qs]
    for c, s in zip(parts, ss):
        s = jnp.where(kpos <= qpos, s, NEG_INF)
        m0 = jnp.max(s, axis=0, keepdims=True)
        p = jnp.exp(s - m0)
        m_ref[:, c] = m0
        l_ref[:, c] = jnp.sum(p, axis=0, keepdims=True)
        acc_ref[:, c] = lax.dot_general(v_own, p.astype(BF16), tn, preferred_element_type=F32)

    def past(n, carry):
        kb = k_ref[0, pl.ds(n * blk, blk), :].astype(BF16)
        vb = v_ref[0, pl.ds(n * blk, blk), :].astype(BF16)
        sbs = [lax.dot_general(kb, qp, nt, preferred_element_type=F32) * scale for qp in qs]
        for c, sb in zip(parts, sbs):
            on = sel_ref[pl.ds(n, 1), c] > 0.5
            m_old = m_ref[:, c]
            m_new = jnp.where(on, jnp.maximum(m_old, jnp.max(sb, axis=0, keepdims=True)), m_old)
            pb = jnp.exp(sb - m_new)
            alpha = jnp.exp(m_old - m_new)
            m_ref[:, c] = m_new
            l_ref[:, c] = alpha * l_ref[:, c] + jnp.where(on, jnp.sum(pb, axis=0, keepdims=True), 0.0)
            pv = lax.dot_general(vb, pb.astype(BF16), tn, preferred_element_type=F32)
            acc_ref[:, c] = alpha * acc_ref[:, c] + jnp.where(on, pv, 0.0)
        return carry

    lax.fori_loop(0, qb, past, 0)

    out = (acc_ref[...] / l_ref[...]).T
    for g in range(group):
        o_ref[0, :, g * hd:(g + 1) * hd] = out[g * blk:(g + 1) * blk, :].astype(o_ref.dtype)


def moba_prompt(q, k, v, cfg):
    bsz, seq, _ = q.shape
    blk, hd, group, kvh = cfg.moba_block, cfg.head_dim, cfg.group, cfg.n_kv_heads
    assert seq % blk == 0
    nblk = seq // blk
    rows = group * blk
    nsel = -(-nblk // 8) * 8
    body = functools.partial(_moba_prompt_body, blk=blk, nblk=nblk, group=group, hd=hd, topk=cfg.moba_topk,
                             scale=hd ** -0.5)
    return pl.pallas_call(
        body,
        grid=(bsz, kvh, nblk),
        in_specs=[
            pl.BlockSpec((1, blk, group * hd), lambda b, kh, i: (b, i, kh)),
            pl.BlockSpec((1, seq, hd), lambda b, kh, i: (b, 0, kh)),
            pl.BlockSpec((1, seq, hd), lambda b, kh, i: (b, 0, kh)),
        ],
        out_specs=pl.BlockSpec((1, blk, group * hd), lambda b, kh, i: (b, i, kh)),
        out_shape=jax.ShapeDtypeStruct(q.shape, BF16),
        scratch_shapes=[
            pltpu.VMEM((nblk, hd), F32),
            pltpu.VMEM((nblk, rows), F32),
            pltpu.VMEM((1, rows), F32),
            pltpu.VMEM((1, rows), F32),
            pltpu.VMEM((hd, rows), F32),
        ],
        compiler_params=_params("parallel", "parallel", "arbitrary"),
        name="moba_prompt",
    )(q, k, v)


def _block_means_body(pt_ref, *refs, inv_rows, ppb):
    k_refs, o_ref = refs[:-1], refs[-1]
    for i in range(len(k_refs) // ppb):
        pages = k_refs[i * ppb:(i + 1) * ppb]
        o_ref[0, i] = functools.reduce(jnp.add, [jnp.sum(k_ref[0], axis=0) for k_ref in pages]) * inv_rows


def _page_specs(ppb, ps, kvh, hd):
    return [pl.BlockSpec((1, ps, kvh, hd), functools.partial(lambda p, b, n, pt: (pt[b, n * ppb + p], 0, 0, 0), p))
            for p in range(ppb)]


def block_means(cache_k, page_table, n_full, ppb, cfg):
    db = page_table.shape[0]
    _, ps, kvh, hd = cache_k.shape
    bps = 2 if n_full % 2 == 0 else 1
    return pl.pallas_call(
        functools.partial(_block_means_body, inv_rows=1.0 / (ps * ppb), ppb=ppb),
        grid_spec=pltpu.PrefetchScalarGridSpec(
            num_scalar_prefetch=1,
            grid=(db, n_full // bps),
            in_specs=_page_specs(bps * ppb, ps, kvh, hd),
            out_specs=pl.BlockSpec((1, bps, kvh, hd), lambda b, n, pt: (b, n, 0, 0)),
        ),
        out_shape=jax.ShapeDtypeStruct((db, n_full, kvh, hd), F32),
        compiler_params=_params("parallel", "parallel"),
        name="block_means",
    )(page_table, *([cache_k] * (bps * ppb)))


def _moba_select_body(q_ref, means_ref, sel_ref, *, topk, kvh):
    nt = (((1,), (1,)), ((), ()))
    gate = jnp.concatenate(
        [lax.dot_general(q_ref[0, kh], means_ref[0, :, kh, :].astype(BF16), nt, preferred_element_type=F32)
         for kh in range(kvh)], axis=0)
    lane = lax.broadcasted_iota(jnp.int32, gate.shape, 1)
    nb = gate.shape[1]
    sel = jnp.zeros(gate.shape, F32)
    for _ in range(topk):
        m = jnp.max(gate, axis=1, keepdims=True)
        first = jnp.min(jnp.where(gate == m, lane, nb), axis=1, keepdims=True)
        pick = lane == first
        sel = jnp.where(pick, 1.0, sel)
        gate = jnp.where(pick, -3.0e38, gate)
    sel_ref[0] = sel


def moba_select(q4, means, cfg):
    db, kvh, rpk, hd = q4.shape
    n_full = means.shape[1]
    rows = kvh * rpk
    return pl.pallas_call(
        functools.partial(_moba_select_body, topk=min(cfg.moba_topk, n_full), kvh=kvh),
        grid=(db,),
        in_specs=[pl.BlockSpec((1, kvh, rpk, hd), lambda b: (b, 0, 0, 0)),
                  pl.BlockSpec((1, n_full, kvh, hd), lambda b: (b, 0, 0, 0))],
        out_specs=pl.BlockSpec((1, rows, n_full), lambda b: (b, 0, 0)),
        out_shape=jax.ShapeDtypeStruct((db, rows, n_full), F32),
        compiler_params=_params("parallel"),
        name="moba_select",
    )(q4, means)


def _moba_sample_body(pt_ref, *refs, ppb, scale, t_new, kvh, rpk):
    q_ref, sel_ref, kn_ref, vn_ref = refs[0], refs[1 + 2 * ppb], refs[2 + 2 * ppb], refs[3 + 2 * ppb]
    k_refs, v_refs = refs[1:1 + ppb], refs[1 + ppb:1 + 2 * ppb]
    o_ref, m_ref, l_ref, acc_ref = refs[4 + 2 * ppb:]
    n = pl.program_id(1)
    nt = (((1,), (1,)), ((), ()))
    rows = q_ref.shape[1]
    ps = kn_ref.shape[1] // kvh

    @pl.when(n == 0)
    def _():
        m_ref[...] = jnp.full(m_ref.shape, NEG_INF, F32)
        l_ref[...] = jnp.zeros(l_ref.shape, F32)
        acc_ref[...] = jnp.zeros(acc_ref.shape, F32)

    def head_rows(page_ref, kh):
        return page_ref[0, pl.ds(kh, ps, stride=kvh), :].astype(BF16)

    def absorb(k_pages, v_pages, on):
        s = jnp.concatenate(
            [jnp.concatenate(
                [lax.dot_general(q_ref[0, kh * rpk:(kh + 1) * rpk, :], head_rows(kp, kh), nt,
                                 preferred_element_type=F32) for kp in k_pages], axis=1)
             for kh in range(kvh)], axis=0) * scale
        s = jnp.where(on, s, NEG_INF)
        m_old = m_ref[...]
        m_new = jnp.maximum(m_old, jnp.max(s, axis=1, keepdims=True))
        p = jnp.where(on, jnp.exp(s - m_new), 0.0)
        alpha = jnp.exp(m_old - m_new)
        m_ref[...] = m_new
        l_ref[...] = alpha * l_ref[...] + jnp.sum(p, axis=1, keepdims=True)
        pb = p.astype(BF16)
        pv = jnp.concatenate(
            [functools.reduce(jnp.add, [
                jnp.dot(pb[kh * rpk:(kh + 1) * rpk, i * ps:(i + 1) * ps], head_rows(vp, kh),
                        preferred_element_type=F32) for i, vp in enumerate(v_pages)])
             for kh in range(kvh)], axis=0)
        acc_ref[...] = alpha * acc_ref[...] + pv

    absorb(k_refs, v_refs, sel_ref[0, 0] > 0.5)

    @pl.when(n == pl.num_programs(1) - 1)
    def _():
        tq = lax.broadcasted_iota(jnp.int32, (rows, ps), 0) % t_new
        tk = lax.broadcasted_iota(jnp.int32, (rows, ps), 1)
        absorb([kn_ref], [vn_ref], tk <= tq)
        o_ref[0] = acc_ref[...] / l_ref[...]


def moba_sample(q, cache_k, cache_v, page_table, sel, k_new, v_new, t_new, ppb, cfg):
    db, rows, hd = q.shape
    kvh = cfg.n_kv_heads
    n_full = sel.shape[1]
    page_rows = cache_k.shape[1]
    body = functools.partial(_moba_sample_body, ppb=ppb, scale=hd ** -0.5, t_new=t_new, kvh=kvh, rpk=rows // kvh)
    page_specs = [
        pl.BlockSpec((1, page_rows, hd), functools.partial(lambda p, b, n, pt: (pt[b, n * ppb + p], 0, 0), p))
        for p in range(ppb)]
    new_spec = pl.BlockSpec((1, page_rows, hd), lambda b, n, pt: (b, 0, 0))
    return pl.pallas_call(
        body,
        grid_spec=pltpu.PrefetchScalarGridSpec(
            num_scalar_prefetch=1,
            grid=(db, n_full),
            in_specs=([pl.BlockSpec((1, rows, hd), lambda b, n, pt: (b, 0, 0))] + page_specs + page_specs
                      + [pl.BlockSpec((1, 1, rows, 1), lambda b, n, pt: (b, n, 0, 0)), new_spec, new_spec]),
            out_specs=pl.BlockSpec((1, rows, hd), lambda b, n, pt: (b, 0, 0)),
            scratch_shapes=[
                pltpu.VMEM((rows, 1), F32),
                pltpu.VMEM((rows, 1), F32),
                pltpu.VMEM((rows, hd), F32),
            ],
        ),
        out_shape=jax.ShapeDtypeStruct((db, rows, hd), F32),
        compiler_params=_params("parallel", "arbitrary"),
        name="moba_sample",
    )(page_table, q, *([cache_k] * ppb), *([cache_v] * ppb), sel, k_new, v_new)


def _cast_body(x_ref, o_ref):
    o_ref[...] = x_ref[0].astype(o_ref.dtype)


def cast_layer(w, layer, cols=None):
    _, r, c = w.shape
    cols = c if cols is None else cols
    tr = next(t for t in (512, 256, 128, 64, 32, 16) if r % t == 0 and t * cols * 4 <= 8 * 1024 * 1024)
    return pl.pallas_call(
        _cast_body,
        grid=(r // tr,),
        in_specs=[pl.BlockSpec((1, tr, cols), lambda i: (layer, i, 0))],
        out_specs=pl.BlockSpec((tr, cols), lambda i: (i, 0)),
        out_shape=jax.ShapeDtypeStruct((r, cols), BF16),
        compiler_params=_params("parallel"),
        name="cast_bf16",
    )(w)


def _cast_tail_body(x_ref, o_ref, *, valid):
    lane = lax.broadcasted_iota(jnp.int32, o_ref.shape, 1)
    o_ref[...] = jnp.where(lane < valid, x_ref[0], 0.0).astype(o_ref.dtype)


def cast_tail(w, layer, col0):
    _, r, c = w.shape
    assert col0 % LANES == 0 and 0 < c - col0 <= LANES
    tr = _pick(r, (512, 256, 128, 64))
    return pl.pallas_call(
        functools.partial(_cast_tail_body, valid=c - col0),
        grid=(r // tr,),
        in_specs=[pl.BlockSpec((1, tr, LANES), lambda i: (layer, i, col0 // LANES))],
        out_specs=pl.BlockSpec((tr, LANES), lambda i: (i, 0)),
        out_shape=jax.ShapeDtypeStruct((r, LANES), BF16),
        compiler_params=_params("parallel"),
        name="cast_tail_bf16",
    )(w)


class Weights(NamedTuple):
    norm_mix: jax.Array
    norm_ffn: jax.Array
    w_in: jax.Array
    w_z: jax.Array
    w_gate2: jax.Array
    b_gate: jax.Array
    gla_norm: jax.Array
    w_out: jax.Array
    norm_kv: jax.Array
    w_kv: jax.Array
    k_norm: jax.Array
    w_q: jax.Array
    q_norm: jax.Array
    w_o: jax.Array
    peer_w_query: tuple
    peer_keys: tuple
    peer_u: tuple
    peer_v: tuple


def _prep_weights(norm_mix, norm_ffn, gla_w_in, gla_w_gate2, gla_b_gate, gla_norm, gla_w_out, norm_kv, w_kv,
                  k_norm, w_q, q_norm, w_o, peer_w_query, peer_sub_keys, peer_u, peer_v, cfg):
    main = 2 * cfg.gla_dk_tot + 2 * cfg.d_model
    rank = cfg.gla_gate_rank
    depth = norm_mix.shape[0]
    return Weights(
        norm_mix=norm_mix,
        norm_ffn=norm_ffn,
        w_in=cast_layer(gla_w_in, 0, cols=main),
        w_z=cast_tail(gla_w_in, 0, main),
        w_gate2=jnp.pad(gla_w_gate2[0], ((0, LANES - rank), (0, 0))).astype(BF16),
        b_gate=gla_b_gate[0][None, :],
        gla_norm=gla_norm[0].reshape(1, -1),
        w_out=gla_w_out[0].astype(BF16),
        norm_kv=norm_kv,
        w_kv=w_kv.astype(BF16),
        k_norm=k_norm[None, :],
        w_q=w_q[0].astype(BF16),
        q_norm=q_norm[0][None, :],
        w_o=w_o[0].astype(BF16),
        peer_w_query=tuple(peer_w_query[l].astype(BF16) for l in range(depth)),
        peer_keys=tuple(peer_sub_keys[l].reshape(2 * cfg.peer_heads, cfg.peer_n_keys, -1).astype(BF16)
                        for l in range(depth)),
        peer_u=tuple(cast_layer(peer_u, l) for l in range(depth)),
        peer_v=tuple(cast_layer(peer_v, l) for l in range(depth)),
    )


def _trunk(x, pos0, s0, attend, w, cfg):
    bsz, seq, d = x.shape
    n = bsz * seq
    hd = cfg.head_dim
    xf = x.reshape(n, d)

    (hn,) = rmsnorm_cast(xf, w.norm_mix[0:1])
    proj = matmul(hn, w.w_in, name="gla_in")
    z = matmul(hn, w.w_z, name="gla_gate_in")
    og, s_fin = gla(proj.reshape(bsz, seq, -1), z.reshape(bsz, seq, -1), w.w_gate2, w.b_gate, w.gla_norm, s0, cfg)
    h = matmul(og.reshape(n, -1), w.w_out, residual=xf, name="gla_out")
    h = peer_block(h, w.norm_ffn[0], w.peer_w_query[0], w.peer_keys[0], w.peer_u[0], w.peer_v[0], cfg)

    kvn, hn1 = rmsnorm_cast(h, jnp.stack([w.norm_kv, w.norm_mix[1]]))
    kv = matmul(kvn, w.w_kv, name="kv_proj")
    cos, sin = _rope_tables(pos0, seq, hd)
    (k,) = head_norm_rope(kv[:, :cfg.kv_dim], w.k_norm, cos, sin, seq, (F32,), hd)
    v = kv[:, cfg.kv_dim:]
    (q,) = head_norm_rope(matmul(hn1, w.w_q, name="q_proj"), w.q_norm, cos, sin, seq, (BF16,), hd)

    o = attend(q, k, v)
    h = matmul(o, w.w_o, residual=h, name="attn_out")
    h = peer_block(h, w.norm_ffn[1], w.peer_w_query[1], w.peer_keys[1], w.peer_u[1], w.peer_v[1], cfg)
    return (h.reshape(bsz, seq, d), s_fin[None],
            k.reshape(bsz, seq, cfg.n_kv_heads, hd), v.reshape(bsz, seq, cfg.n_kv_heads, hd))


def _attend_prompt(bsz, seq, cfg):
    def attend(q, k, v):
        n = q.shape[0]
        o = moba_prompt(q.reshape(bsz, seq, -1), k.reshape(bsz, seq, -1), v.reshape(bsz, seq, -1), cfg)
        return o.reshape(n, -1)
    return attend


def _attend_sample(db, t_new, cache_k, cache_v, page_table, past_len, cfg):
    hd, kvh, group, nh = cfg.head_dim, cfg.n_kv_heads, cfg.group, cfg.n_heads
    blk, ps = cfg.moba_block, cfg.page_size
    ppb = blk // ps
    n_full = past_len // blk
    assert past_len % blk == 0 and n_full > 0 and t_new <= ps
    pt = page_table[:, :n_full * ppb]
    ck = cache_k.reshape(cache_k.shape[0], ps * kvh, hd)
    cv = cache_v.reshape(cache_v.shape[0], ps * kvh, hd)

    def attend(q, k, v):
        qr = q.reshape(db, t_new, nh, hd).transpose(0, 2, 1, 3).reshape(db, nh * t_new, hd)
        means = block_means(cache_k, pt, n_full, ppb, cfg)
        sel = moba_select(qr.reshape(db, kvh, group * t_new, hd), means, cfg)
        sel = sel.transpose(0, 2, 1)[..., None]
        pad = ((0, 0), (0, ps - t_new), (0, 0), (0, 0))
        k_new = jnp.pad(k.reshape(db, t_new, kvh, hd), pad).reshape(db, ps * kvh, hd)
        v_new = jnp.pad(v.reshape(db, t_new, kvh, hd), pad).reshape(db, ps * kvh, hd)
        o = moba_sample(qr, ck, cv, pt, sel, k_new, v_new, t_new, ppb, cfg)
        o = o.reshape(db, nh, t_new, hd).transpose(0, 2, 1, 3).reshape(db * t_new, nh * hd)
        return o.astype(BF16)
    return attend


def _forward(x_prompt, x_sample, cache_k, cache_v, state_gla, page_table, weights, past_len, cfg):
    w = _prep_weights(*weights, cfg)
    bsz, seq, _ = x_prompt.shape
    db, t_new, _ = x_sample.shape
    s0_p = jnp.zeros((bsz, cfg.gla_heads, cfg.gla_dk, cfg.gla_dv), state_gla.dtype)
    y_p, st_p, k_p, v_p = _trunk(x_prompt, 0, s0_p, _attend_prompt(bsz, seq, cfg), w, cfg)
    attend_s = _attend_sample(db, t_new, cache_k, cache_v, page_table, past_len, cfg)
    y_s, st_s, k_s, v_s = _trunk(x_sample, past_len, state_gla[0], attend_s, w, cfg)
    return (y_p, y_s, st_p, st_s, k_p, v_p, k_s, v_s)


def kernel(x_prompt, x_sample, cache_k, cache_v, state_gla, page_table, norm_mix, norm_ffn, gla_w_in, gla_w_gate2, gla_b_gate, gla_norm, gla_w_out, norm_kv, w_kv, k_norm, w_q, q_norm, w_o, peer_w_query, peer_sub_keys, peer_u, peer_v):
    weights = (norm_mix, norm_ffn, gla_w_in, gla_w_gate2, gla_b_gate, gla_norm, gla_w_out, norm_kv, w_kv, k_norm,
               w_q, q_norm, w_o, peer_w_query, peer_sub_keys, peer_u, peer_v)
    past_len = page_table.shape[1] * CFG.page_size
    return _forward(x_prompt, x_sample, cache_k, cache_v, state_gla, page_table, weights, past_len, CFG)
```

```python
import functools
import math
from typing import NamedTuple

import numpy as np
import jax
import jax.numpy as jnp
from jax import lax
from jax.experimental import pallas as pl
from jax.experimental.pallas import tpu as pltpu

F32 = jnp.float32
BF16 = jnp.bfloat16

LANES = 128
MXU_COLS = 256
RMS_EPS = 1e-6
NEG_INF = -1e30
ROPE_THETA = 10000.0
VMEM_LIMIT_BYTES = 56 * 1024 * 1024


class Cfg(NamedTuple):
    d_model: int = 4096
    gla_heads: int = 4
    gla_gate_rank: int = 16
    gla_gate_tau: float = 16.0
    gla_chunk: int = 64
    head_dim: int = 128
    n_kv_heads: int = 8
    moba_block: int = 256
    moba_topk: int = 3
    page_size: int = 128
    peer_heads: int = 8
    peer_n_keys: int = 128
    peer_topk: int = 16
    peer_key_dim: int = 256

    @property
    def gla_dk_tot(self):
        return self.d_model // 2

    @property
    def gla_dk(self):
        return self.gla_dk_tot // self.gla_heads

    @property
    def gla_dv(self):
        return self.d_model // self.gla_heads

    @property
    def n_heads(self):
        return self.d_model // self.head_dim

    @property
    def group(self):
        return self.n_heads // self.n_kv_heads

    @property
    def kv_dim(self):
        return self.n_kv_heads * self.head_dim


CFG = Cfg()


def _params(*sem, flags=None):
    return pltpu.CompilerParams(dimension_semantics=sem, vmem_limit_bytes=VMEM_LIMIT_BYTES, flags=flags)


def _pick(n, prefs):
    for p in prefs:
        if n % p == 0:
            return p
    return n


def _rmsnorm_body(x_ref, g_ref, *o_refs, transposed):
    x = x_ref[...]
    y = x * lax.rsqrt(jnp.mean(x * x, axis=-1, keepdims=True) + RMS_EPS)
    ng = g_ref.shape[0]
    for i in range(ng):
        o_refs[i][...] = (y * g_ref[i:i + 1, :]).astype(o_refs[i].dtype)
    if transposed:
        o_refs[ng][...] = (y * g_ref[0:1, :]).T.astype(o_refs[ng].dtype)


def rmsnorm_cast(x, gains, transposed=False):
    m, d = x.shape
    g = gains.shape[0]
    tm = _pick(m, (256, 128, 64))
    out_specs = [pl.BlockSpec((tm, d), lambda i: (i, 0))] * g
    out_shape = [jax.ShapeDtypeStruct((m, d), BF16)] * g
    if transposed:
        out_specs = out_specs + [pl.BlockSpec((d, tm), lambda i: (0, i))]
        out_shape = out_shape + [jax.ShapeDtypeStruct((d, m), BF16)]
    return pl.pallas_call(
        functools.partial(_rmsnorm_body, transposed=transposed),
        grid=(m // tm,),
        in_specs=[pl.BlockSpec((tm, d), lambda i: (i, 0)), pl.BlockSpec((g, d), lambda i: (0, 0))],
        out_specs=out_specs,
        out_shape=out_shape,
        compiler_params=_params("parallel"),
        name="rmsnorm_cast",
    )(x, gains)


def _mm_body(x_ref, w_ref, o_ref):
    o_ref[...] = jnp.dot(x_ref[...], w_ref[...], preferred_element_type=F32).astype(o_ref.dtype)


def _mm_res_body(x_ref, w_ref, r_ref, o_ref):
    o_ref[...] = (r_ref[...] + jnp.dot(x_ref[...], w_ref[...], preferred_element_type=F32)).astype(o_ref.dtype)


def matmul(x, w, residual=None, out_dtype=F32, name="matmul"):
    m, k = x.shape
    n = w.shape[1]
    tm = _pick(m, (1024, 512, 256, 128, 64))
    tn = _pick(n, (1024, 512, 256, 128))
    in_specs = [pl.BlockSpec((tm, k), lambda j, i: (i, 0)), pl.BlockSpec((k, tn), lambda j, i: (0, j))]
    args = [x, w]
    body = _mm_body
    if residual is not None:
        in_specs.append(pl.BlockSpec((tm, tn), lambda j, i: (i, j)))
        args.append(residual)
        body = _mm_res_body
    return pl.pallas_call(
        body,
        grid=(n // tn, m // tm),
        in_specs=in_specs,
        out_specs=pl.BlockSpec((tm, tn), lambda j, i: (i, j)),
        out_shape=jax.ShapeDtypeStruct((m, n), out_dtype),
        compiler_params=_params("parallel", "parallel"),
        name=name,
    )(*args)


def _cumsum_rows(g):
    c = g.shape[0]
    row = lax.broadcasted_iota(jnp.int32, g.shape, 0)
    b = g
    s = 1
    while s < c:
        b = b + jnp.where(row >= s, pltpu.roll(b, s, axis=0), 0.0)
        s *= 2
    return b


def _log_sigmoid(x):
    return -(jnp.maximum(-x, 0.0) + jnp.log1p(jnp.exp(-jnp.abs(x))))


def _gla_body(q_ref, k_ref, v_ref, r_ref, z_ref, wg_ref, bg_ref, gn_ref, s0_ref, og_ref, sf_ref, st_ref,
              *, chunk, nsub, q_scale, inv_tau):
    t = pl.program_id(2)

    @pl.when(t == 0)
    def _():
        st_ref[...] = s0_ref[0, 0].T

    za = jnp.dot(z_ref[0].astype(BF16), wg_ref[...], preferred_element_type=F32) + bg_ref[...]
    log_a = _log_sigmoid(za) * inv_tau
    tril = (lax.broadcasted_iota(jnp.int32, (chunk, chunk), 0) >= lax.broadcasted_iota(jnp.int32, (chunk, chunk), 1))
    nt = (((1,), (1,)), ((), ()))
    tn = (((0,), (0,)), ((), ()))
    for i in range(nsub):
        sl = slice(i * chunk, (i + 1) * chunk)
        b = _cumsum_rows(log_a[sl])
        b_last = b[chunk - 1:chunk, :]
        q = q_ref[0, sl, :] * q_scale
        k = k_ref[0, sl, :]
        v = v_ref[0, sl, :].astype(BF16)
        qe = (q * jnp.exp(b)).astype(BF16)
        ke = (k * jnp.exp(-b)).astype(BF16)
        kd = (k * jnp.exp(b_last - b)).astype(BF16)
        att = lax.dot_general(qe, ke, nt, preferred_element_type=F32)
        att = jnp.where(tril, att, 0.0).astype(BF16)
        st = st_ref[...]
        o = lax.dot_general(qe, st.astype(BF16), nt, preferred_element_type=F32)
        o = o + jnp.dot(att, v, preferred_element_type=F32)
        st_ref[...] = st * jnp.exp(b_last) + lax.dot_general(v, kd, tn, preferred_element_type=F32)
        on = o * lax.rsqrt(jnp.mean(o * o, axis=-1, keepdims=True) + RMS_EPS) * gn_ref[...]
        r = r_ref[0, sl, :]
        og_ref[0, sl, :] = (on * (r * jax.nn.sigmoid(r))).astype(og_ref.dtype)

    @pl.when(t == pl.num_programs(2) - 1)
    def _():
        sf_ref[0, 0] = st_ref[...].T


def gla(proj, z, wg, bg, gn, s0, cfg):
    bsz, seq, _ = proj.shape
    h, dk, dv = cfg.gla_heads, cfg.gla_dk, cfg.gla_dv
    chunk = math.gcd(seq, cfg.gla_chunk)
    rows = _pick(seq, (4 * chunk, 2 * chunk, chunk))
    nsub = rows // chunk
    kq, kk, kv, kr = 0, h, (2 * h * dk) // dv, (2 * h * dk) // dv + h
    body = functools.partial(_gla_body, chunk=chunk, nsub=nsub, q_scale=dk ** -0.5, inv_tau=1.0 / cfg.gla_gate_tau)
    return pl.pallas_call(
        body,
        grid=(bsz, h, seq // rows),
        in_specs=[
            pl.BlockSpec((1, rows, dk), lambda b, hh, t: (b, t, kq + hh)),
            pl.BlockSpec((1, rows, dk), lambda b, hh, t: (b, t, kk + hh)),
            pl.BlockSpec((1, rows, dv), lambda b, hh, t: (b, t, kv + hh)),
            pl.BlockSpec((1, rows, dv), lambda b, hh, t: (b, t, kr + hh)),
            pl.BlockSpec((1, rows, LANES), lambda b, hh, t: (b, t, 0)),
            pl.BlockSpec((LANES, dk), lambda b, hh, t: (0, hh)),
            pl.BlockSpec((1, dk), lambda b, hh, t: (0, hh)),
            pl.BlockSpec((1, dv), lambda b, hh, t: (0, hh)),
            pl.BlockSpec((1, 1, dk, dv), lambda b, hh, t: (b, hh, 0, 0)),
        ],
        out_specs=[
            pl.BlockSpec((1, rows, dv), lambda b, hh, t: (b, t, hh)),
            pl.BlockSpec((1, 1, dk, dv), lambda b, hh, t: (b, hh, 0, 0)),
        ],
        out_shape=[
            jax.ShapeDtypeStruct((bsz, seq, h * dv), BF16),
            jax.ShapeDtypeStruct((bsz, h, dk, dv), F32),
        ],
        scratch_shapes=[pltpu.VMEM((dv, dk), F32)],
        compiler_params=_params("parallel", "parallel", "arbitrary"),
        name="gla",
    )(proj, proj, proj, proj, z, wg, bg, gn, s0)


def _headrope_body(x_ref, g_ref, cos_ref, sin_ref, *o_refs, nh, hd):
    for hh in range(nh):
        x = x_ref[:, hh * hd:(hh + 1) * hd]
        y = x * lax.rsqrt(jnp.mean(x * x, axis=-1, keepdims=True) + RMS_EPS) * g_ref[...]
        out = y * cos_ref[...] + pltpu.roll(y, hd // 2, axis=1) * sin_ref[...]
        for o_ref in o_refs:
            o_ref[:, hh * hd:(hh + 1) * hd] = out.astype(o_ref.dtype)


def head_norm_rope(x, gain, cos, sin, seq, out_dtypes, hd):
    n, width = x.shape
    nh = width // hd
    tm = _pick(seq, (256, 128, 64, 32, 16, 8))
    per = seq // tm
    return pl.pallas_call(
        functools.partial(_headrope_body, nh=nh, hd=hd),
        grid=(n // tm,),
        in_specs=[
            pl.BlockSpec((tm, width), lambda i: (i, 0)),
            pl.BlockSpec((1, hd), lambda i: (0, 0)),
            pl.BlockSpec((tm, hd), lambda i: (i % per, 0)),
            pl.BlockSpec((tm, hd), lambda i: (i % per, 0)),
        ],
        out_specs=[pl.BlockSpec((tm, width), lambda i: (i, 0)) for _ in out_dtypes],
        out_shape=[jax.ShapeDtypeStruct((n, width), dt) for dt in out_dtypes],
        compiler_params=_params("parallel"),
        name="head_norm_rope",
    )(x, gain, cos, sin)


def _rope_tables(pos0, seq, hd):
    half = hd // 2
    inv = ROPE_THETA ** (-np.arange(half, dtype=np.float64) / half)
    ang = (pos0 + np.arange(seq, dtype=np.float64))[:, None] * inv[None, :]
    cos, sin = np.cos(ang), np.sin(ang)
    return (jnp.asarray(np.concatenate([cos, cos], axis=1), F32),
            jnp.asarray(np.concatenate([-sin, sin], axis=1), F32))


def _top_values(s, k):
    vals = []
    cur = s
    for i in range(k):
        m = jnp.max(cur, axis=0, keepdims=True)
        vals.append(m)
        if i + 1 < k:
            cur = jnp.where(cur == m, NEG_INF, cur)
    return vals


ROW_TAU, ROW_M1, ROW_M2, ROW_INVZ = range(4)


def _peer_route_body(q_ref, keys_ref, s1_ref, s2_ref, rows_ref, *, heads, nkeys, half, topk):
    nt = (((1,), (1,)), ((), ()))
    tm = q_ref.shape[0]
    sub = 8
    row = lax.broadcasted_iota(jnp.int32, (sub, tm), 0)
    for hh in range(heads):
        st = []
        for p in range(2):
            g = 2 * hh + p
            qg = q_ref[:, g * half:(g + 1) * half].astype(BF16)
            st.append(lax.dot_general(keys_ref[g], qg, nt, preferred_element_type=F32))
        s1, s2 = st
        v1 = _top_values(s1, topk)
        v2 = _top_values(s2, topk)
        v2g = []
        for g0 in range(0, topk, sub):
            grp = jnp.full((sub, tm), NEG_INF, F32)
            for b in range(g0, min(g0 + sub, topk)):
                grp = jnp.where(row == b - g0, v2[b], grp)
            v2g.append(grp)
        cands = []
        for a in range(topk):
            bmax = topk // (a + 1)
            for gi, grp in enumerate(v2g):
                if gi * sub < bmax:
                    cands.append(jnp.where(row < bmax - gi * sub, v1[a] + grp, NEG_INF))
        cur = cands
        tau = None
        for i in range(topk):
            tau = functools.reduce(jnp.maximum, [jnp.max(c, axis=0, keepdims=True) for c in cur])
            if i + 1 < topk:
                cur = [jnp.where(c == tau, NEG_INF, c) for c in cur]
        m1, m2 = v1[0], v2[0]
        mx = m1 + m2
        z = functools.reduce(
            jnp.add, [jnp.sum(jnp.where(c >= tau, jnp.exp(c - mx), 0.0), axis=0, keepdims=True) for c in cands])
        s1_ref[hh] = s1
        s2_ref[hh] = s2
        rows_ref[ROW_TAU, hh:hh + 1, :] = tau
        rows_ref[ROW_M1, hh:hh + 1, :] = m1
        rows_ref[ROW_M2, hh:hh + 1, :] = m2
        rows_ref[ROW_INVZ, hh:hh + 1, :] = 1.0 / z


def peer_route(q, keys, cfg):
    n = q.shape[0]
    heads, nkeys, half = cfg.peer_heads, cfg.peer_n_keys, cfg.peer_key_dim // 2
    tm = _pick(n, (256, 128))
    tab = jax.ShapeDtypeStruct((heads, nkeys, n), F32)
    tab_spec = pl.BlockSpec((heads, nkeys, tm), lambda i: (0, 0, i))
    return pl.pallas_call(
        functools.partial(_peer_route_body, heads=heads, nkeys=nkeys, half=half, topk=cfg.peer_topk),
        grid=(n // tm,),
        in_specs=[pl.BlockSpec((tm, q.shape[1]), lambda i: (i, 0)),
                  pl.BlockSpec(keys.shape, lambda i: (0, 0, 0))],
        out_specs=[tab_spec, tab_spec, pl.BlockSpec((4, heads, tm), lambda i: (0, 0, i))],
        out_shape=[tab, tab, jax.ShapeDtypeStruct((4, heads, n), F32)],
        compiler_params=_params("parallel"),
        name="peer_route",
    )(q, keys)


def _gelu(x):
    return 0.5 * x * (1.0 + lax.erf(x * (1.0 / math.sqrt(2.0))))


SUBLANES = 8
GATE_LANES = 512


def _peer_row_tables(bc_ref, jt, s1_ref, rows_ref, *, heads, nsub):
    tm = bc_ref.shape[-1]
    for r in range(nsub):
        i1 = jt * nsub + r
        for hh in range(heads):
            s1row = s1_ref[hh, pl.ds(i1, 1), :]
            crow = jnp.exp(s1row - rows_ref[ROW_M1, hh:hh + 1, :]) * rows_ref[ROW_INVZ, hh:hh + 1, :]
            bc_ref[hh, r, 0] = jnp.broadcast_to(s1row, (SUBLANES, tm))
            bc_ref[hh, r, 1] = jnp.broadcast_to(crow, (SUBLANES, tm))


def _peer_gate_passes(dst_ref, s2_ref, e2_ref, taub_ref, bc_ref, *, heads, nkeys, nsub):
    tm = dst_ref.shape[-1]
    tw = min(tm, GATE_LANES)

    def one(c, l):
        rows, cols = slice(c, c + SUBLANES), slice(l, l + tw)
        ws = [None] * nsub
        for hh in range(heads):
            s2t, e2t, tb = s2_ref[hh, rows, cols], e2_ref[hh, rows, cols], taub_ref[hh, :, cols]
            for r in range(nsub):
                hit = (s2t + bc_ref[hh, r, 0, :, cols]) >= tb
                term = jnp.where(hit, e2t, 0.0) * bc_ref[hh, r, 1, :, cols]
                ws[r] = term if ws[r] is None else ws[r] + term
        for r in range(nsub):
            dst_ref[r * nkeys + c:r * nkeys + c + SUBLANES, cols] = ws[r]

    return [functools.partial(one, c, l) for c in range(0, nkeys, SUBLANES) for l in range(0, tm, tw)]


def _peer_dense_body(xt_ref, u_ref, v_ref, s1_ref, s2_ref, rows_ref, o_ref, e2_ref, taub_ref, bc_ref,
                     gate_a_ref, gate_b_ref, ht_ref, *, heads, nkeys, nsub):
    j = pl.program_id(1)
    last = pl.num_programs(1) - 1
    row_tables = functools.partial(_peer_row_tables, bc_ref, s1_ref=s1_ref, rows_ref=rows_ref, heads=heads, nsub=nsub)
    gate_passes = functools.partial(_peer_gate_passes, s2_ref=s2_ref, e2_ref=e2_ref, taub_ref=taub_ref, bc_ref=bc_ref,
                                    heads=heads, nkeys=nkeys, nsub=nsub)
    tm = xt_ref.shape[1]
    d = o_ref.shape[1]

    @pl.when(j == 0)
    def _():
        o_ref[...] = jnp.zeros_like(o_ref)
        for hh in range(heads):
            e2_ref[hh] = jnp.exp(s2_ref[hh] - rows_ref[ROW_M2, hh:hh + 1, :])
            taub_ref[hh] = jnp.broadcast_to(rows_ref[ROW_TAU, hh:hh + 1, :], (SUBLANES, tm))
        row_tables(0)
        for run in gate_passes(gate_a_ref):
            run()

    def step(cur_ref, nxt_ref):
        row_tables(jnp.minimum(j + 1, last))
        passes = gate_passes(nxt_ref)
        tw = min(tm, MXU_COLS)
        n_tok, n_out = tm // tw, d // MXU_COLS
        n_first = n_tok
        first = len(passes) // 2
        quota = ([first // n_first + (i < first % n_first) for i in range(n_first)]
                 + [(len(passes) - first) // n_out + (i < (len(passes) - first) % n_out) for i in range(n_out)])
        it = iter(passes)
        for c in range(n_tok):
            cols = slice(c * tw, (c + 1) * tw)
            act = _gelu(jnp.dot(u_ref[...], xt_ref[:, cols], preferred_element_type=F32))
            ht_ref[cols, :] = (cur_ref[:, cols] * act).T.astype(BF16)
            for _ in range(quota[c]):
                next(it)()
        for p in range(n_out):
            cols = slice(p * MXU_COLS, (p + 1) * MXU_COLS)
            o_ref[:, cols] += jnp.dot(ht_ref[...], v_ref[:, cols], preferred_element_type=F32)
            for _ in range(quota[n_first + p]):
                next(it)()

    @pl.when(j % 2 == 0)
    def _():
        step(gate_a_ref, gate_b_ref)

    @pl.when(j % 2 == 1)
    def _():
        step(gate_b_ref, gate_a_ref)


def peer_dense(xt, u, v, s1, s2, rows, cfg):
    d, n = xt.shape
    e = u.shape[0]
    heads, nkeys = cfg.peer_heads, cfg.peer_n_keys
    tm = _pick(n, (1024, 512, 256, 128))
    nsub = 2
    te = nsub * nkeys
    once = dict(pipeline_mode=pl.Buffered(1))
    tab_spec = pl.BlockSpec((heads, nkeys, tm), lambda i, j: (0, 0, i), **once)
    return pl.pallas_call(
        functools.partial(_peer_dense_body, heads=heads, nkeys=nkeys, nsub=nsub),
        grid=(n // tm, e // te),
        in_specs=[
            pl.BlockSpec((d, tm), lambda i, j: (0, i), **once),
            pl.BlockSpec((te, d), lambda i, j: (j, 0)),
            pl.BlockSpec((te, d), lambda i, j: (j, 0)),
            tab_spec, tab_spec,
            pl.BlockSpec((4, heads, tm), lambda i, j: (0, 0, i), **once),
        ],
        out_specs=pl.BlockSpec((tm, d), lambda i, j: (i, 0), **once),
        out_shape=jax.ShapeDtypeStruct((n, d), F32),
        scratch_shapes=[
            pltpu.VMEM((heads, nkeys, tm), F32),
            pltpu.VMEM((heads, SUBLANES, tm), F32),
            pltpu.VMEM((heads, nsub, 2, SUBLANES, tm), F32),
            pltpu.VMEM((te, tm), F32),
            pltpu.VMEM((te, tm), F32),
            pltpu.VMEM((tm, te), BF16),
        ],
        compiler_params=_params("parallel", "arbitrary"),
        name="peer_dense",
    )(xt, u, v, s1, s2, rows)


def peer_block(h, g_ffn, w_query, keys, u, v, cfg):
    n = h.shape[0]
    npad = -(-n // LANES) * LANES
    hp = h if npad == n else jnp.pad(h, ((0, npad - n), (0, 0)))
    xn, xt = rmsnorm_cast(hp, g_ffn[None, :], transposed=True)
    q = matmul(xn, w_query, name="peer_query")
    s1, s2, rows = peer_route(q, keys, cfg)
    return h + peer_dense(xt, u, v, s1, s2, rows, cfg)[:n]


def _moba_prompt_body(q_ref, k_ref, v_ref, o_ref, means_ref, sel_ref, m_ref, l_ref, acc_ref,
                      *, blk, nblk, group, hd, topk, scale):
    qb = pl.program_id(2)
    nt = (((1,), (1,)), ((), ()))
    tn = (((0,), (0,)), ((), ()))
    rows = group * blk

    @pl.when(qb == 0)
    def _():
        for n in range(nblk):
            means_ref[n:n + 1, :] = jnp.mean(k_ref[0, n * blk:(n + 1) * blk, :], axis=0, keepdims=True)

    q4 = jnp.concatenate([q_ref[0, :, g * hd:(g + 1) * hd] for g in range(group)], axis=0)

    gate = lax.dot_general(means_ref[...].astype(BF16), q4, nt, preferred_element_type=F32)
    bidx = lax.broadcasted_iota(jnp.int32, gate.shape, 0)
    cand = bidx < qb
    gate = jnp.where(cand, gate, NEG_INF)
    rank = jnp.zeros(gate.shape, F32)
    for mm in range(nblk):
        gm = gate[mm:mm + 1, :]
        beats = (gm > gate) | ((gm == gate) & (mm < bidx))
        rank = rank + jnp.where(beats, 1.0, 0.0)
    sel_ref[...] = jnp.where((rank < topk) & cand, 1.0, 0.0)

    pw = min(rows, 2 * blk)
    parts = [slice(i, i + pw) for i in range(0, rows, pw)]
    qs = [q4[c, :] for c in parts]

    kpos = lax.broadcasted_iota(jnp.int32, (blk, pw), 0)
    qpos = lax.broadcasted_iota(jnp.int32, (blk, pw), 1) % blk
    k_own = k_ref[0, pl.ds(qb * blk, blk), :].astype(BF16)
    v_own = v_ref[0, pl.ds(qb * blk, blk), :].astype(BF16)
    ss = [lax.dot_general(k_own, qp, nt, preferred_element_type=F32) * scale for qp in qs]
    for c, s in zip(parts, ss):
        s = jnp.where(kpos <= qpos, s, NEG_INF)
        m0 = jnp.max(s, axis=0, keepdims=True)
        p = jnp.exp(s - m0)
        m_ref[:, c] = m0
        l_ref[:, c] = jnp.sum(p, axis=0, keepdims=True)
        acc_ref[:, c] = lax.dot_general(v_own, p.astype(BF16), tn, preferred_element_type=F32)

    def past(n, carry):
        kb = k_ref[0, pl.ds(n * blk, blk), :].astype(BF16)
        vb = v_ref[0, pl.ds(n * blk, blk), :].astype(BF16)
        sbs = [lax.dot_general(kb, qp, nt, preferred_element_type=F32) * scale for qp in qs]
        for c, sb in zip(parts, sbs):
            on = sel_ref[pl.ds(n, 1), c] > 0.5
            m_old = m_ref[:, c]
            m_new = jnp.where(on, jnp.maximum(m_old, jnp.max(sb, axis=0, keepdims=True)), m_old)
            pb = jnp.exp(sb - m_new)
            alpha = jnp.exp(m_old - m_new)
            m_ref[:, c] = m_new
            l_ref[:, c] = alpha * l_ref[:, c] + jnp.where(on, jnp.sum(pb, axis=0, keepdims=True), 0.0)
            pv = lax.dot_general(vb, pb.astype(BF16), tn, preferred_element_type=F32)
            acc_ref[:, c] = alpha * acc_ref[:, c] + jnp.where(on, pv, 0.0)
        return carry

    lax.fori_loop(0, qb, past, 0)

    out = (acc_ref[...] / l_ref[...]).T
    for g in range(group):
        o_ref[0, :, g * hd:(g + 1) * hd] = out[g * blk:(g + 1) * blk, :].astype(o_ref.dtype)


def moba_prompt(q, k, v, cfg):
    bsz, seq, _ = q.shape
    blk, hd, group, kvh = cfg.moba_block, cfg.head_dim, cfg.group, cfg.n_kv_heads
    assert seq % blk == 0
    nblk = seq // blk
    rows = group * blk
    nsel = -(-nblk // 8) * 8
    body = functools.partial(_moba_prompt_body, blk=blk, nblk=nblk, group=group, hd=hd, topk=cfg.moba_topk,
                             scale=hd ** -0.5)
    return pl.pallas_call(
        body,
        grid=(bsz, kvh, nblk),
        in_specs=[
            pl.BlockSpec((1, blk, group * hd), lambda b, kh, i: (b, i, kh)),
            pl.BlockSpec((1, seq, hd), lambda b, kh, i: (b, 0, kh)),
            pl.BlockSpec((1, seq, hd), lambda b, kh, i: (b, 0, kh)),
        ],
        out_specs=pl.BlockSpec((1, blk, group * hd), lambda b, kh, i: (b, i, kh)),
        out_shape=jax.ShapeDtypeStruct(q.shape, BF16),
        scratch_shapes=[
            pltpu.VMEM((nblk, hd), F32),
            pltpu.VMEM((nblk, rows), F32),
            pltpu.VMEM((1, rows), F32),
            pltpu.VMEM((1, rows), F32),
            pltpu.VMEM((hd, rows), F32),
        ],
        compiler_params=_params("parallel", "parallel", "arbitrary"),
        name="moba_prompt",
    )(q, k, v)


def _block_means_body(pt_ref, *refs, inv_rows, ppb):
    k_refs, o_ref = refs[:-1], refs[-1]
    for i in range(len(k_refs) // ppb):
        pages = k_refs[i * ppb:(i + 1) * ppb]
        o_ref[0, i] = functools.reduce(jnp.add, [jnp.sum(k_ref[0], axis=0) for k_ref in pages]) * inv_rows


def _page_specs(ppb, ps, kvh, hd):
    return [pl.BlockSpec((1, ps, kvh, hd), functools.partial(lambda p, b, n, pt: (pt[b, n * ppb + p], 0, 0, 0), p))
            for p in range(ppb)]


def block_means(cache_k, page_table, n_full, ppb, cfg):
    db = page_table.shape[0]
    _, ps, kvh, hd = cache_k.shape
    bps = 2 if n_full % 2 == 0 else 1
    return pl.pallas_call(
        functools.partial(_block_means_body, inv_rows=1.0 / (ps * ppb), ppb=ppb),
        grid_spec=pltpu.PrefetchScalarGridSpec(
            num_scalar_prefetch=1,
            grid=(db, n_full // bps),
            in_specs=_page_specs(bps * ppb, ps, kvh, hd),
            out_specs=pl.BlockSpec((1, bps, kvh, hd), lambda b, n, pt: (b, n, 0, 0)),
        ),
        out_shape=jax.ShapeDtypeStruct((db, n_full, kvh, hd), F32),
        compiler_params=_params("parallel", "parallel"),
        name="block_means",
    )(page_table, *([cache_k] * (bps * ppb)))


def _moba_select_body(q_ref, means_ref, sel_ref, *, topk, kvh):
    nt = (((1,), (1,)), ((), ()))
    gate = jnp.concatenate(
        [lax.dot_general(q_ref[0, kh], means_ref[0, :, kh, :].astype(BF16), nt, preferred_element_type=F32)
         for kh in range(kvh)], axis=0)
    lane = lax.broadcasted_iota(jnp.int32, gate.shape, 1)
    nb = gate.shape[1]
    sel = jnp.zeros(gate.shape, F32)
    for _ in range(topk):
        m = jnp.max(gate, axis=1, keepdims=True)
        first = jnp.min(jnp.where(gate == m, lane, nb), axis=1, keepdims=True)
        pick = lane == first
        sel = jnp.where(pick, 1.0, sel)
        gate = jnp.where(pick, -3.0e38, gate)
    sel_ref[0] = sel


def moba_select(q4, means, cfg):
    db, kvh, rpk, hd = q4.shape
    n_full = means.shape[1]
    rows = kvh * rpk
    return pl.pallas_call(
        functools.partial(_moba_select_body, topk=min(cfg.moba_topk, n_full), kvh=kvh),
        grid=(db,),
        in_specs=[pl.BlockSpec((1, kvh, rpk, hd), lambda b: (b, 0, 0, 0)),
                  pl.BlockSpec((1, n_full, kvh, hd), lambda b: (b, 0, 0, 0))],
        out_specs=pl.BlockSpec((1, rows, n_full), lambda b: (b, 0, 0)),
        out_shape=jax.ShapeDtypeStruct((db, rows, n_full), F32),
        compiler_params=_params("parallel"),
        name="moba_select",
    )(q4, means)


def _moba_sample_body(pt_ref, *refs, ppb, scale, t_new, kvh, rpk):
    q_ref, sel_ref, kn_ref, vn_ref = refs[0], refs[1 + 2 * ppb], refs[2 + 2 * ppb], refs[3 + 2 * ppb]
    k_refs, v_refs = refs[1:1 + ppb], refs[1 + ppb:1 + 2 * ppb]
    o_ref, m_ref, l_ref, acc_ref = refs[4 + 2 * ppb:]
    n = pl.program_id(1)
    nt = (((1,), (1,)), ((), ()))
    rows = q_ref.shape[1]
    ps = kn_ref.shape[1] // kvh

    @pl.when(n == 0)
    def _():
        m_ref[...] = jnp.full(m_ref.shape, NEG_INF, F32)
        l_ref[...] = jnp.zeros(l_ref.shape, F32)
        acc_ref[...] = jnp.zeros(acc_ref.shape, F32)

    def head_rows(page_ref, kh):
        return page_ref[0, pl.ds(kh, ps, stride=kvh), :].astype(BF16)

    def absorb(k_pages, v_pages, on):
        s = jnp.concatenate(
            [jnp.concatenate(
                [lax.dot_general(q_ref[0, kh * rpk:(kh + 1) * rpk, :], head_rows(kp, kh), nt,
                                 preferred_element_type=F32) for kp in k_pages], axis=1)
             for kh in range(kvh)], axis=0) * scale
        s = jnp.where(on, s, NEG_INF)
        m_old = m_ref[...]
        m_new = jnp.maximum(m_old, jnp.max(s, axis=1, keepdims=True))
        p = jnp.where(on, jnp.exp(s - m_new), 0.0)
        alpha = jnp.exp(m_old - m_new)
        m_ref[...] = m_new
        l_ref[...] = alpha * l_ref[...] + jnp.sum(p, axis=1, keepdims=True)
        pb = p.astype(BF16)
        pv = jnp.concatenate(
            [functools.reduce(jnp.add, [
                jnp.dot(pb[kh * rpk:(kh + 1) * rpk, i * ps:(i + 1) * ps], head_rows(vp, kh),
                        preferred_element_type=F32) for i, vp in enumerate(v_pages)])
             for kh in range(kvh)], axis=0)
        acc_ref[...] = alpha * acc_ref[...] + pv

    absorb(k_refs, v_refs, sel_ref[0, 0] > 0.5)

    @pl.when(n == pl.num_programs(1) - 1)
    def _():
        tq = lax.broadcasted_iota(jnp.int32, (rows, ps), 0) % t_new
        tk = lax.broadcasted_iota(jnp.int32, (rows, ps), 1)
        absorb([kn_ref], [vn_ref], tk <= tq)
        o_ref[0] = acc_ref[...] / l_ref[...]


def moba_sample(q, cache_k, cache_v, page_table, sel, k_new, v_new, t_new, ppb, cfg):
    db, rows, hd = q.shape
    kvh = cfg.n_kv_heads
    n_full = sel.shape[1]
    page_rows = cache_k.shape[1]
    body = functools.partial(_moba_sample_body, ppb=ppb, scale=hd ** -0.5, t_new=t_new, kvh=kvh, rpk=rows // kvh)
    page_specs = [
        pl.BlockSpec((1, page_rows, hd), functools.partial(lambda p, b, n, pt: (pt[b, n * ppb + p], 0, 0), p))
        for p in range(ppb)]
    new_spec = pl.BlockSpec((1, page_rows, hd), lambda b, n, pt: (b, 0, 0))
    return pl.pallas_call(
        body,
        grid_spec=pltpu.PrefetchScalarGridSpec(
            num_scalar_prefetch=1,
            grid=(db, n_full),
            in_specs=([pl.BlockSpec((1, rows, hd), lambda b, n, pt: (b, 0, 0))] + page_specs + page_specs
                      + [pl.BlockSpec((1, 1, rows, 1), lambda b, n, pt: (b, n, 0, 0)), new_spec, new_spec]),
            out_specs=pl.BlockSpec((1, rows, hd), lambda b, n, pt: (b, 0, 0)),
            scratch_shapes=[
                pltpu.VMEM((rows, 1), F32),
                pltpu.VMEM((rows, 1), F32),
                pltpu.VMEM((rows, hd), F32),
            ],
        ),
        out_shape=jax.ShapeDtypeStruct((db, rows, hd), F32),
        compiler_params=_params("parallel", "arbitrary"),
        name="moba_sample",
    )(page_table, q, *([cache_k] * ppb), *([cache_v] * ppb), sel, k_new, v_new)


def _cast_body(x_ref, o_ref):
    o_ref[...] = x_ref[0].astype(o_ref.dtype)


def cast_layer(w, layer, cols=None):
    _, r, c = w.shape
    cols = c if cols is None else cols
    tr = next(t for t in (512, 256, 128, 64, 32, 16) if r % t == 0 and t * cols * 4 <= 8 * 1024 * 1024)
    return pl.pallas_call(
        _cast_body,
        grid=(r // tr,),
        in_specs=[pl.BlockSpec((1, tr, cols), lambda i: (layer, i, 0))],
        out_specs=pl.BlockSpec((tr, cols), lambda i: (i, 0)),
        out_shape=jax.ShapeDtypeStruct((r, cols), BF16),
        compiler_params=_params("parallel"),
        name="cast_bf16",
    )(w)


class Weights(NamedTuple):
    norm_mix: jax.Array
    norm_ffn: jax.Array
    w_in: jax.Array
    w_z: jax.Array
    w_gate2: jax.Array
    b_gate: jax.Array
    gla_norm: jax.Array
    w_out: jax.Array
    norm_kv: jax.Array
    w_kv: jax.Array
    k_norm: jax.Array
    w_q: jax.Array
    q_norm: jax.Array
    w_o: jax.Array
    peer_w_query: tuple
    peer_keys: tuple
    peer_u: tuple
    peer_v: tuple


def _prep_weights(norm_mix, norm_ffn, gla_w_in, gla_w_gate2, gla_b_gate, gla_norm, gla_w_out, norm_kv, w_kv,
                  k_norm, w_q, q_norm, w_o, peer_w_query, peer_sub_keys, peer_u, peer_v, cfg):
    main = 2 * cfg.gla_dk_tot + 2 * cfg.d_model
    rank = cfg.gla_gate_rank
    depth = norm_mix.shape[0]
    return Weights(
        norm_mix=norm_mix,
        norm_ffn=norm_ffn,
        w_in=gla_w_in[0][:, :main].astype(BF16),
        w_z=jnp.pad(gla_w_in[0][:, main:], ((0, 0), (0, LANES - rank))).astype(BF16),
        w_gate2=jnp.pad(gla_w_gate2[0], ((0, LANES - rank), (0, 0))).astype(BF16),
        b_gate=gla_b_gate[0][None, :],
        gla_norm=gla_norm[0].reshape(1, -1),
        w_out=gla_w_out[0].astype(BF16),
        norm_kv=norm_kv,
        w_kv=w_kv.astype(BF16),
        k_norm=k_norm[None, :],
        w_q=w_q[0].astype(BF16),
        q_norm=q_norm[0][None, :],
        w_o=w_o[0].astype(BF16),
        peer_w_query=tuple(peer_w_query[l].astype(BF16) for l in range(depth)),
        peer_keys=tuple(peer_sub_keys[l].reshape(2 * cfg.peer_heads, cfg.peer_n_keys, -1).astype(BF16)
                        for l in range(depth)),
        peer_u=tuple(cast_layer(peer_u, l) for l in range(depth)),
        peer_v=tuple(cast_layer(peer_v, l) for l in range(depth)),
    )


def _trunk(x, pos0, s0, attend, w, cfg):
    bsz, seq, d = x.shape
    n = bsz * seq
    hd = cfg.head_dim
    xf = x.reshape(n, d)

    (hn,) = rmsnorm_cast(xf, w.norm_mix[0:1])
    proj = matmul(hn, w.w_in, name="gla_in")
    z = matmul(hn, w.w_z, name="gla_gate_in")
    og, s_fin = gla(proj.reshape(bsz, seq, -1), z.reshape(bsz, seq, -1), w.w_gate2, w.b_gate, w.gla_norm, s0, cfg)
    h = matmul(og.reshape(n, -1), w.w_out, residual=xf, name="gla_out")
    h = peer_block(h, w.norm_ffn[0], w.peer_w_query[0], w.peer_keys[0], w.peer_u[0], w.peer_v[0], cfg)

    kvn, hn1 = rmsnorm_cast(h, jnp.stack([w.norm_kv, w.norm_mix[1]]))
    kv = matmul(kvn, w.w_kv, name="kv_proj")
    cos, sin = _rope_tables(pos0, seq, hd)
    (k,) = head_norm_rope(kv[:, :cfg.kv_dim], w.k_norm, cos, sin, seq, (F32,), hd)
    v = kv[:, cfg.kv_dim:]
    (q,) = head_norm_rope(matmul(hn1, w.w_q, name="q_proj"), w.q_norm, cos, sin, seq, (BF16,), hd)

    o = attend(q, k, v)
    h = matmul(o, w.w_o, residual=h, name="attn_out")
    h = peer_block(h, w.norm_ffn[1], w.peer_w_query[1], w.peer_keys[1], w.peer_u[1], w.peer_v[1], cfg)
    return (h.reshape(bsz, seq, d), s_fin[None],
            k.reshape(bsz, seq, cfg.n_kv_heads, hd), v.reshape(bsz, seq, cfg.n_kv_heads, hd))


def _attend_prompt(bsz, seq, cfg):
    def attend(q, k, v):
        n = q.shape[0]
        o = moba_prompt(q.reshape(bsz, seq, -1), k.reshape(bsz, seq, -1), v.reshape(bsz, seq, -1), cfg)
        return o.reshape(n, -1)
    return attend


def _attend_sample(db, t_new, cache_k, cache_v, page_table, past_len, cfg):
    hd, kvh, group, nh = cfg.head_dim, cfg.n_kv_heads, cfg.group, cfg.n_heads
    blk, ps = cfg.moba_block, cfg.page_size
    ppb = blk // ps
    n_full = past_len // blk
    assert past_len % blk == 0 and n_full > 0 and t_new <= ps
    pt = page_table[:, :n_full * ppb]
    ck = cache_k.reshape(cache_k.shape[0], ps * kvh, hd)
    cv = cache_v.reshape(cache_v.shape[0], ps * kvh, hd)

    def attend(q, k, v):
        qr = q.reshape(db, t_new, nh, hd).transpose(0, 2, 1, 3).reshape(db, nh * t_new, hd)
        means = block_means(cache_k, pt, n_full, ppb, cfg)
        sel = moba_select(qr.reshape(db, kvh, group * t_new, hd), means, cfg)
        sel = sel.transpose(0, 2, 1)[..., None]
        pad = ((0, 0), (0, ps - t_new), (0, 0), (0, 0))
        k_new = jnp.pad(k.reshape(db, t_new, kvh, hd), pad).reshape(db, ps * kvh, hd)
        v_new = jnp.pad(v.reshape(db, t_new, kvh, hd), pad).reshape(db, ps * kvh, hd)
        o = moba_sample(qr, ck, cv, pt, sel, k_new, v_new, t_new, ppb, cfg)
        o = o.reshape(db, nh, t_new, hd).transpose(0, 2, 1, 3).reshape(db * t_new, nh * hd)
        return o.astype(BF16)
    return attend


def _forward(x_prompt, x_sample, cache_k, cache_v, state_gla, page_table, weights, past_len, cfg):
    w = _prep_weights(*weights, cfg)
    bsz, seq, _ = x_prompt.shape
    db, t_new, _ = x_sample.shape
    s0_p = jnp.zeros((bsz, cfg.gla_heads, cfg.gla_dk, cfg.gla_dv), state_gla.dtype)
    y_p, st_p, k_p, v_p = _trunk(x_prompt, 0, s0_p, _attend_prompt(bsz, seq, cfg), w, cfg)
    attend_s = _attend_sample(db, t_new, cache_k, cache_v, page_table, past_len, cfg)
    y_s, st_s, k_s, v_s = _trunk(x_sample, past_len, state_gla[0], attend_s, w, cfg)
    return (y_p, y_s, st_p, st_s, k_p, v_p, k_s, v_s)


def kernel(x_prompt, x_sample, cache_k, cache_v, state_gla, page_table, norm_mix, norm_ffn, gla_w_in, gla_w_gate2, gla_b_gate, gla_norm, gla_w_out, norm_kv, w_kv, k_norm, w_q, q_norm, w_o, peer_w_query, peer_sub_keys, peer_u, peer_v):
    weights = (norm_mix, norm_ffn, gla_w_in, gla_w_gate2, gla_b_gate, gla_norm, gla_w_out, norm_kv, w_kv, k_norm,
               w_q, q_norm, w_o, peer_w_query, peer_sub_keys, peer_u, peer_v)
    past_len = page_table.shape[1] * CFG.page_size
    return _forward(x_prompt, x_sample, cache_k, cache_v, state_gla, page_table, weights, past_len, CFG)
```

```python
import functools
import math
from typing import NamedTuple

import numpy as np
import jax
import jax.numpy as jnp
from jax import lax
from jax.experimental import pallas as pl
from jax.experimental.pallas import tpu as pltpu

F32 = jnp.float32
BF16 = jnp.bfloat16

LANES = 128
MXU_COLS = 256
RMS_EPS = 1e-6
NEG_INF = -1e30
ROPE_THETA = 10000.0
VMEM_LIMIT_BYTES = 56 * 1024 * 1024


class Cfg(NamedTuple):
    d_model: int = 4096
    gla_heads: int = 4
    gla_gate_rank: int = 16
    gla_gate_tau: float = 16.0
    gla_chunk: int = 64
    head_dim: int = 128
    n_kv_heads: int = 8
    moba_block: int = 256
    moba_topk: int = 3
    page_size: int = 128
    peer_heads: int = 8
    peer_n_keys: int = 128
    peer_topk: int = 16
    peer_key_dim: int = 256

    @property
    def gla_dk_tot(self):
        return self.d_model // 2

    @property
    def gla_dk(self):
        return self.gla_dk_tot // self.gla_heads

    @property
    def gla_dv(self):
        return self.d_model // self.gla_heads

    @property
    def n_heads(self):
        return self.d_model // self.head_dim

    @property
    def group(self):
        return self.n_heads // self.n_kv_heads

    @property
    def kv_dim(self):
        return self.n_kv_heads * self.head_dim


CFG = Cfg()


def _params(*sem, flags=None):
    return pltpu.CompilerParams(dimension_semantics=sem, vmem_limit_bytes=VMEM_LIMIT_BYTES, flags=flags)


def _pick(n, prefs):
    for p in prefs:
        if n % p == 0:
            return p
    return n


def _rmsnorm_body(x_ref, g_ref, *o_refs, transposed):
    x = x_ref[...]
    y = x * lax.rsqrt(jnp.mean(x * x, axis=-1, keepdims=True) + RMS_EPS)
    ng = g_ref.shape[0]
    for i in range(ng):
        o_refs[i][...] = (y * g_ref[i:i + 1, :]).astype(o_refs[i].dtype)
    if transposed:
        o_refs[ng][...] = (y * g_ref[0:1, :]).T.astype(o_refs[ng].dtype)


def rmsnorm_cast(x, gains, transposed=False):
    m, d = x.shape
    g = gains.shape[0]
    tm = _pick(m, (256, 128, 64))
    out_specs = [pl.BlockSpec((tm, d), lambda i: (i, 0))] * g
    out_shape = [jax.ShapeDtypeStruct((m, d), BF16)] * g
    if transposed:
        out_specs = out_specs + [pl.BlockSpec((d, tm), lambda i: (0, i))]
        out_shape = out_shape + [jax.ShapeDtypeStruct((d, m), BF16)]
    return pl.pallas_call(
        functools.partial(_rmsnorm_body, transposed=transposed),
        grid=(m // tm,),
        in_specs=[pl.BlockSpec((tm, d), lambda i: (i, 0)), pl.BlockSpec((g, d), lambda i: (0, 0))],
        out_specs=out_specs,
        out_shape=out_shape,
        compiler_params=_params("parallel"),
        name="rmsnorm_cast",
    )(x, gains)


def _mm_body(x_ref, w_ref, o_ref):
    o_ref[...] = jnp.dot(x_ref[...], w_ref[...], preferred_element_type=F32).astype(o_ref.dtype)


def _mm_res_body(x_ref, w_ref, r_ref, o_ref):
    o_ref[...] = (r_ref[...] + jnp.dot(x_ref[...], w_ref[...], preferred_element_type=F32)).astype(o_ref.dtype)


def matmul(x, w, residual=None, out_dtype=F32, name="matmul"):
    m, k = x.shape
    n = w.shape[1]
    tm = _pick(m, (1024, 512, 256, 128, 64))
    tn = _pick(n, (1024, 512, 256, 128))
    in_specs = [pl.BlockSpec((tm, k), lambda j, i: (i, 0)), pl.BlockSpec((k, tn), lambda j, i: (0, j))]
    args = [x, w]
    body = _mm_body
    if residual is not None:
        in_specs.append(pl.BlockSpec((tm, tn), lambda j, i: (i, j)))
        args.append(residual)
        body = _mm_res_body
    return pl.pallas_call(
        body,
        grid=(n // tn, m // tm),
        in_specs=in_specs,
        out_specs=pl.BlockSpec((tm, tn), lambda j, i: (i, j)),
        out_shape=jax.ShapeDtypeStruct((m, n), out_dtype),
        compiler_params=_params("parallel", "parallel"),
        name=name,
    )(*args)


def _cumsum_rows(g):
    c = g.shape[0]
    row = lax.broadcasted_iota(jnp.int32, g.shape, 0)
    b = g
    s = 1
    while s < c:
        b = b + jnp.where(row >= s, pltpu.roll(b, s, axis=0), 0.0)
        s *= 2
    return b


def _log_sigmoid(x):
    return -(jnp.maximum(-x, 0.0) + jnp.log1p(jnp.exp(-jnp.abs(x))))


def _gla_body(q_ref, k_ref, v_ref, r_ref, z_ref, wg_ref, bg_ref, gn_ref, s0_ref, og_ref, sf_ref, st_ref,
              *, chunk, nsub, q_scale, inv_tau):
    t = pl.program_id(2)

    @pl.when(t == 0)
    def _():
        st_ref[...] = s0_ref[0, 0].T

    za = jnp.dot(z_ref[0].astype(BF16), wg_ref[...], preferred_element_type=F32) + bg_ref[...]
    log_a = _log_sigmoid(za) * inv_tau
    tril = (lax.broadcasted_iota(jnp.int32, (chunk, chunk), 0) >= lax.broadcasted_iota(jnp.int32, (chunk, chunk), 1))
    nt = (((1,), (1,)), ((), ()))
    tn = (((0,), (0,)), ((), ()))
    for i in range(nsub):
        sl = slice(i * chunk, (i + 1) * chunk)
        b = _cumsum_rows(log_a[sl])
        b_last = b[chunk - 1:chunk, :]
        q = q_ref[0, sl, :] * q_scale
        k = k_ref[0, sl, :]
        v = v_ref[0, sl, :].astype(BF16)
        qe = (q * jnp.exp(b)).astype(BF16)
        ke = (k * jnp.exp(-b)).astype(BF16)
        kd = (k * jnp.exp(b_last - b)).astype(BF16)
        att = lax.dot_general(qe, ke, nt, preferred_element_type=F32)
        att = jnp.where(tril, att, 0.0).astype(BF16)
        st = st_ref[...]
        o = lax.dot_general(qe, st.astype(BF16), nt, preferred_element_type=F32)
        o = o + jnp.dot(att, v, preferred_element_type=F32)
        st_ref[...] = st * jnp.exp(b_last) + lax.dot_general(v, kd, tn, preferred_element_type=F32)
        on = o * lax.rsqrt(jnp.mean(o * o, axis=-1, keepdims=True) + RMS_EPS) * gn_ref[...]
        r = r_ref[0, sl, :]
        og_ref[0, sl, :] = (on * (r * jax.nn.sigmoid(r))).astype(og_ref.dtype)

    @pl.when(t == pl.num_programs(2) - 1)
    def _():
        sf_ref[0, 0] = st_ref[...].T


def gla(proj, z, wg, bg, gn, s0, cfg):
    bsz, seq, _ = proj.shape
    h, dk, dv = cfg.gla_heads, cfg.gla_dk, cfg.gla_dv
    chunk = math.gcd(seq, cfg.gla_chunk)
    rows = _pick(seq, (8 * chunk, 4 * chunk, 2 * chunk, chunk))
    nsub = rows // chunk
    kq, kk, kv, kr = 0, h, (2 * h * dk) // dv, (2 * h * dk) // dv + h
    body = functools.partial(_gla_body, chunk=chunk, nsub=nsub, q_scale=dk ** -0.5, inv_tau=1.0 / cfg.gla_gate_tau)
    return pl.pallas_call(
        body,
        grid=(bsz, h, seq // rows),
        in_specs=[
            pl.BlockSpec((1, rows, dk), lambda b, hh, t: (b, t, kq + hh)),
            pl.BlockSpec((1, rows, dk), lambda b, hh, t: (b, t, kk + hh)),
            pl.BlockSpec((1, rows, dv), lambda b, hh, t: (b, t, kv + hh)),
            pl.BlockSpec((1, rows, dv), lambda b, hh, t: (b, t, kr + hh)),
            pl.BlockSpec((1, rows, LANES), lambda b, hh, t: (b, t, 0)),
            pl.BlockSpec((LANES, dk), lambda b, hh, t: (0, hh)),
            pl.BlockSpec((1, dk), lambda b, hh, t: (0, hh)),
            pl.BlockSpec((1, dv), lambda b, hh, t: (0, hh)),
            pl.BlockSpec((1, 1, dk, dv), lambda b, hh, t: (b, hh, 0, 0)),
        ],
        out_specs=[
            pl.BlockSpec((1, rows, dv), lambda b, hh, t: (b, t, hh)),
            pl.BlockSpec((1, 1, dk, dv), lambda b, hh, t: (b, hh, 0, 0)),
        ],
        out_shape=[
            jax.ShapeDtypeStruct((bsz, seq, h * dv), BF16),
            jax.ShapeDtypeStruct((bsz, h, dk, dv), F32),
        ],
        scratch_shapes=[pltpu.VMEM((dv, dk), F32)],
        compiler_params=_params("parallel", "parallel", "arbitrary"),
        name="gla",
    )(proj, proj, proj, proj, z, wg, bg, gn, s0)


def _headrope_body(x_ref, g_ref, cos_ref, sin_ref, *o_refs, nh, hd):
    for hh in range(nh):
        x = x_ref[:, hh * hd:(hh + 1) * hd]
        y = x * lax.rsqrt(jnp.mean(x * x, axis=-1, keepdims=True) + RMS_EPS) * g_ref[...]
        out = y * cos_ref[...] + pltpu.roll(y, hd // 2, axis=1) * sin_ref[...]
        for o_ref in o_refs:
            o_ref[:, hh * hd:(hh + 1) * hd] = out.astype(o_ref.dtype)


def head_norm_rope(x, gain, cos, sin, seq, out_dtypes, hd):
    n, width = x.shape
    nh = width // hd
    tm = _pick(seq, (256, 128, 64, 32, 16, 8))
    per = seq // tm
    return pl.pallas_call(
        functools.partial(_headrope_body, nh=nh, hd=hd),
        grid=(n // tm,),
        in_specs=[
            pl.BlockSpec((tm, width), lambda i: (i, 0)),
            pl.BlockSpec((1, hd), lambda i: (0, 0)),
            pl.BlockSpec((tm, hd), lambda i: (i % per, 0)),
            pl.BlockSpec((tm, hd), lambda i: (i % per, 0)),
        ],
        out_specs=[pl.BlockSpec((tm, width), lambda i: (i, 0)) for _ in out_dtypes],
        out_shape=[jax.ShapeDtypeStruct((n, width), dt) for dt in out_dtypes],
        compiler_params=_params("parallel"),
        name="head_norm_rope",
    )(x, gain, cos, sin)


def _rope_tables(pos0, seq, hd):
    half = hd // 2
    inv = ROPE_THETA ** (-np.arange(half, dtype=np.float64) / half)
    ang = (pos0 + np.arange(seq, dtype=np.float64))[:, None] * inv[None, :]
    cos, sin = np.cos(ang), np.sin(ang)
    return (jnp.asarray(np.concatenate([cos, cos], axis=1), F32),
            jnp.asarray(np.concatenate([-sin, sin], axis=1), F32))


def _top_values(s, k):
    vals = []
    cur = s
    for i in range(k):
        m = jnp.max(cur, axis=0, keepdims=True)
        vals.append(m)
        if i + 1 < k:
            cur = jnp.where(cur == m, NEG_INF, cur)
    return vals


ROW_TAU, ROW_M1, ROW_M2, ROW_INVZ = range(4)


def _peer_route_body(q_ref, keys_ref, s1_ref, s2_ref, rows_ref, *, heads, nkeys, half, topk):
    nt = (((1,), (1,)), ((), ()))
    tm = q_ref.shape[0]
    sub = 8
    row = lax.broadcasted_iota(jnp.int32, (sub, tm), 0)
    for hh in range(heads):
        st = []
        for p in range(2):
            g = 2 * hh + p
            qg = q_ref[:, g * half:(g + 1) * half].astype(BF16)
            st.append(lax.dot_general(keys_ref[g], qg, nt, preferred_element_type=F32))
        s1, s2 = st
        v1 = _top_values(s1, topk)
        v2 = _top_values(s2, topk)
        v2g = []
        for g0 in range(0, topk, sub):
            grp = jnp.full((sub, tm), NEG_INF, F32)
            for b in range(g0, min(g0 + sub, topk)):
                grp = jnp.where(row == b - g0, v2[b], grp)
            v2g.append(grp)
        cands = []
        for a in range(topk):
            bmax = topk // (a + 1)
            for gi, grp in enumerate(v2g):
                if gi * sub < bmax:
                    cands.append(jnp.where(row < bmax - gi * sub, v1[a] + grp, NEG_INF))
        cur = cands
        tau = None
        for i in range(topk):
            tau = functools.reduce(jnp.maximum, [jnp.max(c, axis=0, keepdims=True) for c in cur])
            if i + 1 < topk:
                cur = [jnp.where(c == tau, NEG_INF, c) for c in cur]
        m1, m2 = v1[0], v2[0]
        mx = m1 + m2
        z = functools.reduce(
            jnp.add, [jnp.sum(jnp.where(c >= tau, jnp.exp(c - mx), 0.0), axis=0, keepdims=True) for c in cands])
        s1_ref[hh] = s1
        s2_ref[hh] = s2
        rows_ref[ROW_TAU, hh:hh + 1, :] = tau
        rows_ref[ROW_M1, hh:hh + 1, :] = m1
        rows_ref[ROW_M2, hh:hh + 1, :] = m2
        rows_ref[ROW_INVZ, hh:hh + 1, :] = 1.0 / z


def peer_route(q, keys, cfg):
    n = q.shape[0]
    heads, nkeys, half = cfg.peer_heads, cfg.peer_n_keys, cfg.peer_key_dim // 2
    tm = _pick(n, (256, 128))
    tab = jax.ShapeDtypeStruct((heads, nkeys, n), F32)
    tab_spec = pl.BlockSpec((heads, nkeys, tm), lambda i: (0, 0, i))
    return pl.pallas_call(
        functools.partial(_peer_route_body, heads=heads, nkeys=nkeys, half=half, topk=cfg.peer_topk),
        grid=(n // tm,),
        in_specs=[pl.BlockSpec((tm, q.shape[1]), lambda i: (i, 0)),
                  pl.BlockSpec(keys.shape, lambda i: (0, 0, 0))],
        out_specs=[tab_spec, tab_spec, pl.BlockSpec((4, heads, tm), lambda i: (0, 0, i))],
        out_shape=[tab, tab, jax.ShapeDtypeStruct((4, heads, n), F32)],
        compiler_params=_params("parallel"),
        name="peer_route",
    )(q, keys)


def _gelu(x):
    return 0.5 * x * (1.0 + lax.erf(x * (1.0 / math.sqrt(2.0))))


SUBLANES = 8
GATE_LANES = 512


def _peer_row_tables(bc_ref, jt, s1_ref, rows_ref, *, heads, nsub):
    tm = bc_ref.shape[-1]
    for r in range(nsub):
        i1 = jt * nsub + r
        for hh in range(heads):
            s1row = s1_ref[hh, pl.ds(i1, 1), :]
            crow = jnp.exp(s1row - rows_ref[ROW_M1, hh:hh + 1, :]) * rows_ref[ROW_INVZ, hh:hh + 1, :]
            bc_ref[hh, r, 0] = jnp.broadcast_to(s1row, (SUBLANES, tm))
            bc_ref[hh, r, 1] = jnp.broadcast_to(crow, (SUBLANES, tm))


def _peer_gate_passes(dst_ref, s2_ref, e2_ref, taub_ref, bc_ref, *, heads, nkeys, nsub):
    tm = dst_ref.shape[-1]
    tw = min(tm, GATE_LANES)

    def one(c, l):
        rows, cols = slice(c, c + SUBLANES), slice(l, l + tw)
        ws = [None] * nsub
        for hh in range(heads):
            s2t, e2t, tb = s2_ref[hh, rows, cols], e2_ref[hh, rows, cols], taub_ref[hh, :, cols]
            for r in range(nsub):
                hit = (s2t + bc_ref[hh, r, 0, :, cols]) >= tb
                term = jnp.where(hit, e2t, 0.0) * bc_ref[hh, r, 1, :, cols]
                ws[r] = term if ws[r] is None else ws[r] + term
        for r in range(nsub):
            dst_ref[r * nkeys + c:r * nkeys + c + SUBLANES, cols] = ws[r]

    return [functools.partial(one, c, l) for c in range(0, nkeys, SUBLANES) for l in range(0, tm, tw)]


def _peer_dense_body(xt_ref, u_ref, v_ref, s1_ref, s2_ref, rows_ref, o_ref, e2_ref, taub_ref, bc_ref,
                     gate_a_ref, gate_b_ref, ht_ref, *, heads, nkeys, nsub):
    j = pl.program_id(1)
    last = pl.num_programs(1) - 1
    row_tables = functools.partial(_peer_row_tables, bc_ref, s1_ref=s1_ref, rows_ref=rows_ref, heads=heads, nsub=nsub)
    gate_passes = functools.partial(_peer_gate_passes, s2_ref=s2_ref, e2_ref=e2_ref, taub_ref=taub_ref, bc_ref=bc_ref,
                                    heads=heads, nkeys=nkeys, nsub=nsub)
    tm = xt_ref.shape[1]
    d = o_ref.shape[1]

    @pl.when(j == 0)
    def _():
        o_ref[...] = jnp.zeros_like(o_ref)
        for hh in range(heads):
            e2_ref[hh] = jnp.exp(s2_ref[hh] - rows_ref[ROW_M2, hh:hh + 1, :])
            taub_ref[hh] = jnp.broadcast_to(rows_ref[ROW_TAU, hh:hh + 1, :], (SUBLANES, tm))
        row_tables(0)
        for run in gate_passes(gate_a_ref):
            run()

    def step(cur_ref, nxt_ref):
        row_tables(jnp.minimum(j + 1, last))
        passes = gate_passes(nxt_ref)
        tw = min(tm, MXU_COLS)
        n_tok, n_out = tm // tw, d // MXU_COLS
        n_first = n_tok
        first = len(passes) // 2
        quota = ([first // n_first + (i < first % n_first) for i in range(n_first)]
                 + [(len(passes) - first) // n_out + (i < (len(passes) - first) % n_out) for i in range(n_out)])
        it = iter(passes)
        for c in range(n_tok):
            cols = slice(c * tw, (c + 1) * tw)
            act = _gelu(jnp.dot(u_ref[...], xt_ref[:, cols], preferred_element_type=F32))
            ht_ref[cols, :] = (cur_ref[:, cols] * act).T.astype(BF16)
            for _ in range(quota[c]):
                next(it)()
        for p in range(n_out):
            cols = slice(p * MXU_COLS, (p + 1) * MXU_COLS)
            o_ref[:, cols] += jnp.dot(ht_ref[...], v_ref[:, cols], preferred_element_type=F32)
            for _ in range(quota[n_first + p]):
                next(it)()

    @pl.when(j % 2 == 0)
    def _():
        step(gate_a_ref, gate_b_ref)

    @pl.when(j % 2 == 1)
    def _():
        step(gate_b_ref, gate_a_ref)


def peer_dense(xt, u, v, s1, s2, rows, cfg):
    d, n = xt.shape
    e = u.shape[0]
    heads, nkeys = cfg.peer_heads, cfg.peer_n_keys
    tm = _pick(n, (1024, 512, 256, 128))
    nsub = 2
    te = nsub * nkeys
    once = dict(pipeline_mode=pl.Buffered(1))
    tab_spec = pl.BlockSpec((heads, nkeys, tm), lambda i, j: (0, 0, i), **once)
    return pl.pallas_call(
        functools.partial(_peer_dense_body, heads=heads, nkeys=nkeys, nsub=nsub),
        grid=(n // tm, e // te),
        in_specs=[
            pl.BlockSpec((d, tm), lambda i, j: (0, i), **once),
            pl.BlockSpec((te, d), lambda i, j: (j, 0)),
            pl.BlockSpec((te, d), lambda i, j: (j, 0)),
            tab_spec, tab_spec,
            pl.BlockSpec((4, heads, tm), lambda i, j: (0, 0, i), **once),
        ],
        out_specs=pl.BlockSpec((tm, d), lambda i, j: (i, 0), **once),
        out_shape=jax.ShapeDtypeStruct((n, d), F32),
        scratch_shapes=[
            pltpu.VMEM((heads, nkeys, tm), F32),
            pltpu.VMEM((heads, SUBLANES, tm), F32),
            pltpu.VMEM((heads, nsub, 2, SUBLANES, tm), F32),
            pltpu.VMEM((te, tm), F32),
            pltpu.VMEM((te, tm), F32),
            pltpu.VMEM((tm, te), BF16),
        ],
        compiler_params=_params("parallel", "arbitrary"),
        name="peer_dense",
    )(xt, u, v, s1, s2, rows)


def peer_block(h, g_ffn, w_query, keys, u, v, cfg):
    n = h.shape[0]
    npad = -(-n // LANES) * LANES
    hp = h if npad == n else jnp.pad(h, ((0, npad - n), (0, 0)))
    xn, xt = rmsnorm_cast(hp, g_ffn[None, :], transposed=True)
    q = matmul(xn, w_query, name="peer_query")
    s1, s2, rows = peer_route(q, keys, cfg)
    return h + peer_dense(xt, u, v, s1, s2, rows, cfg)[:n]


def _moba_prompt_body(q_ref, k_ref, v_ref, o_ref, means_ref, sel_ref, m_ref, l_ref, acc_ref,
                      *, blk, nblk, group, hd, topk, scale):
    qb = pl.program_id(2)
    nt = (((1,), (1,)), ((), ()))
    tn = (((0,), (0,)), ((), ()))
    rows = group * blk

    @pl.when(qb == 0)
    def _():
        for n in range(nblk):
            means_ref[n:n + 1, :] = jnp.mean(k_ref[0, n * blk:(n + 1) * blk, :], axis=0, keepdims=True)

    q4 = jnp.concatenate([q_ref[0, :, g * hd:(g + 1) * hd] for g in range(group)], axis=0)

    gate = lax.dot_general(means_ref[...].astype(BF16), q4, nt, preferred_element_type=F32)
    bidx = lax.broadcasted_iota(jnp.int32, gate.shape, 0)
    cand = bidx < qb
    gate = jnp.where(cand, gate, NEG_INF)
    rank = jnp.zeros(gate.shape, F32)
    for mm in range(nblk):
        gm = gate[mm:mm + 1, :]
        beats = (gm > gate) | ((gm == gate) & (mm < bidx))
        rank = rank + jnp.where(beats, 1.0, 0.0)
    sel_ref[...] = jnp.where((rank < topk) & cand, 1.0, 0.0)

    pw = min(rows, 2 * blk)
    parts = [slice(i, i + pw) for i in range(0, rows, pw)]
    qs = [q4[c, :] for c in parts]

    kpos = lax.broadcasted_iota(jnp.int32, (blk, pw), 0)
    qpos = lax.broadcasted_iota(jnp.int32, (blk, pw), 1) % blk
    k_own = k_ref[0, pl.ds(qb * blk, blk), :].astype(BF16)
    v_own = v_ref[0, pl.ds(qb * blk, blk), :].astype(BF16)
    ss = [lax.dot_general(k_own, qp, nt, preferred_element_type=F32) * scale for qp in qs]
    for c, s in zip(parts, ss):
        s = jnp.where(kpos <= qpos, s, NEG_INF)
        m0 = jnp.max(s, axis=0, keepdims=True)
        p = jnp.exp(s - m0)
        m_ref[:, c] = m0
        l_ref[:, c] = jnp.sum(p, axis=0, keepdims=True)
        acc_ref[:, c] = lax.dot_general(v_own, p.astype(BF16), tn, preferred_element_type=F32)

    def past(n, carry):
        kb = k_ref[0, pl.ds(n * blk, blk), :].astype(BF16)
        vb = v_ref[0, pl.ds(n * blk, blk), :].astype(BF16)
        sbs = [lax.dot_general(kb, qp, nt, preferred_element_type=F32) * scale for qp in qs]
        for c, sb in zip(parts, sbs):
            on = sel_ref[pl.ds(n, 1), c] > 0.5
            m_old = m_ref[:, c]
            m_new = jnp.where(on, jnp.maximum(m_old, jnp.max(sb, axis=0, keepdims=True)), m_old)
            pb = jnp.exp(sb - m_new)
            alpha = jnp.exp(m_old - m_new)
            m_ref[:, c] = m_new
            l_ref[:, c] = alpha * l_ref[:, c] + jnp.where(on, jnp.sum(pb, axis=0, keepdims=True), 0.0)
            pv = lax.dot_general(vb, pb.astype(BF16), tn, preferred_element_type=F32)
            acc_ref[:, c] = alpha * acc_ref[:, c] + jnp.where(on, pv, 0.0)
        return carry

    lax.fori_loop(0, qb, past, 0)

    out = (acc_ref[...] / l_ref[...]).T
    for g in range(group):
        o_ref[0, :, g * hd:(g + 1) * hd] = out[g * blk:(g + 1) * blk, :].astype(o_ref.dtype)


def moba_prompt(q, k, v, cfg):
    bsz, seq, _ = q.shape
    blk, hd, group, kvh = cfg.moba_block, cfg.head_dim, cfg.group, cfg.n_kv_heads
    assert seq % blk == 0
    nblk = seq // blk
    rows = group * blk
    nsel = -(-nblk // 8) * 8
    body = functools.partial(_moba_prompt_body, blk=blk, nblk=nblk, group=group, hd=hd, topk=cfg.moba_topk,
                             scale=hd ** -0.5)
    return pl.pallas_call(
        body,
        grid=(bsz, kvh, nblk),
        in_specs=[
            pl.BlockSpec((1, blk, group * hd), lambda b, kh, i: (b, i, kh)),
            pl.BlockSpec((1, seq, hd), lambda b, kh, i: (b, 0, kh)),
            pl.BlockSpec((1, seq, hd), lambda b, kh, i: (b, 0, kh)),
        ],
        out_specs=pl.BlockSpec((1, blk, group * hd), lambda b, kh, i: (b, i, kh)),
        out_shape=jax.ShapeDtypeStruct(q.shape, BF16),
        scratch_shapes=[
            pltpu.VMEM((nblk, hd), F32),
            pltpu.VMEM((nblk, rows), F32),
            pltpu.VMEM((1, rows), F32),
            pltpu.VMEM((1, rows), F32),
            pltpu.VMEM((hd, rows), F32),
        ],
        compiler_params=_params("parallel", "parallel", "arbitrary"),
        name="moba_prompt",
    )(q, k, v)


def _block_means_body(pt_ref, *refs, inv_rows, ppb):
    k_refs, o_ref = refs[:-1], refs[-1]
    for i in range(len(k_refs) // ppb):
        pages = k_refs[i * ppb:(i + 1) * ppb]
        o_ref[0, i] = functools.reduce(jnp.add, [jnp.sum(k_ref[0], axis=0) for k_ref in pages]) * inv_rows


def _page_specs(ppb, ps, kvh, hd):
    return [pl.BlockSpec((1, ps, kvh, hd), functools.partial(lambda p, b, n, pt: (pt[b, n * ppb + p], 0, 0, 0), p))
            for p in range(ppb)]


def block_means(cache_k, page_table, n_full, ppb, cfg):
    db = page_table.shape[0]
    _, ps, kvh, hd = cache_k.shape
    bps = 2 if n_full % 2 == 0 else 1
    return pl.pallas_call(
        functools.partial(_block_means_body, inv_rows=1.0 / (ps * ppb), ppb=ppb),
        grid_spec=pltpu.PrefetchScalarGridSpec(
            num_scalar_prefetch=1,
            grid=(db, n_full // bps),
            in_specs=_page_specs(bps * ppb, ps, kvh, hd),
            out_specs=pl.BlockSpec((1, bps, kvh, hd), lambda b, n, pt: (b, n, 0, 0)),
        ),
        out_shape=jax.ShapeDtypeStruct((db, n_full, kvh, hd), F32),
        compiler_params=_params("parallel", "parallel"),
        name="block_means",
    )(page_table, *([cache_k] * (bps * ppb)))


def _moba_select_body(q_ref, means_ref, sel_ref, *, topk, kvh):
    nt = (((1,), (1,)), ((), ()))
    gate = jnp.concatenate(
        [lax.dot_general(q_ref[0, kh], means_ref[0, :, kh, :].astype(BF16), nt, preferred_element_type=F32)
         for kh in range(kvh)], axis=0)
    lane = lax.broadcasted_iota(jnp.int32, gate.shape, 1)
    nb = gate.shape[1]
    sel = jnp.zeros(gate.shape, F32)
    for _ in range(topk):
        m = jnp.max(gate, axis=1, keepdims=True)
        first = jnp.min(jnp.where(gate == m, lane, nb), axis=1, keepdims=True)
        pick = lane == first
        sel = jnp.where(pick, 1.0, sel)
        gate = jnp.where(pick, -3.0e38, gate)
    sel_ref[0] = sel


def moba_select(q4, means, cfg):
    db, kvh, rpk, hd = q4.shape
    n_full = means.shape[1]
    rows = kvh * rpk
    return pl.pallas_call(
        functools.partial(_moba_select_body, topk=min(cfg.moba_topk, n_full), kvh=kvh),
        grid=(db,),
        in_specs=[pl.BlockSpec((1, kvh, rpk, hd), lambda b: (b, 0, 0, 0)),
                  pl.BlockSpec((1, n_full, kvh, hd), lambda b: (b, 0, 0, 0))],
        out_specs=pl.BlockSpec((1, rows, n_full), lambda b: (b, 0, 0)),
        out_shape=jax.ShapeDtypeStruct((db, rows, n_full), F32),
        compiler_params=_params("parallel"),
        name="moba_select",
    )(q4, means)


def _moba_sample_body(pt_ref, *refs, ppb, scale, t_new, kvh, rpk):
    q_ref, sel_ref, kn_ref, vn_ref = refs[0], refs[1 + 2 * ppb], refs[2 + 2 * ppb], refs[3 + 2 * ppb]
    k_refs, v_refs = refs[1:1 + ppb], refs[1 + ppb:1 + 2 * ppb]
    o_ref, m_ref, l_ref, acc_ref = refs[4 + 2 * ppb:]
    n = pl.program_id(1)
    nt = (((1,), (1,)), ((), ()))
    rows = q_ref.shape[1]
    ps = kn_ref.shape[1] // kvh

    @pl.when(n == 0)
    def _():
        m_ref[...] = jnp.full(m_ref.shape, NEG_INF, F32)
        l_ref[...] = jnp.zeros(l_ref.shape, F32)
        acc_ref[...] = jnp.zeros(acc_ref.shape, F32)

    def head_rows(page_ref, kh):
        return page_ref[0, pl.ds(kh, ps, stride=kvh), :].astype(BF16)

    def absorb(k_pages, v_pages, on):
        s = jnp.concatenate(
            [jnp.concatenate(
                [lax.dot_general(q_ref[0, kh * rpk:(kh + 1) * rpk, :], head_rows(kp, kh), nt,
                                 preferred_element_type=F32) for kp in k_pages], axis=1)
             for kh in range(kvh)], axis=0) * scale
        s = jnp.where(on, s, NEG_INF)
        m_old = m_ref[...]
        m_new = jnp.maximum(m_old, jnp.max(s, axis=1, keepdims=True))
        p = jnp.where(on, jnp.exp(s - m_new), 0.0)
        alpha = jnp.exp(m_old - m_new)
        m_ref[...] = m_new
        l_ref[...] = alpha * l_ref[...] + jnp.sum(p, axis=1, keepdims=True)
        pb = p.astype(BF16)
        pv = jnp.concatenate(
            [functools.reduce(jnp.add, [
                jnp.dot(pb[kh * rpk:(kh + 1) * rpk, i * ps:(i + 1) * ps], head_rows(vp, kh),
                        preferred_element_type=F32) for i, vp in enumerate(v_pages)])
             for kh in range(kvh)], axis=0)
        acc_ref[...] = alpha * acc_ref[...] + pv

    absorb(k_refs, v_refs, sel_ref[0, 0] > 0.5)

    @pl.when(n == pl.num_programs(1) - 1)
    def _():
        tq = lax.broadcasted_iota(jnp.int32, (rows, ps), 0) % t_new
        tk = lax.broadcasted_iota(jnp.int32, (rows, ps), 1)
        absorb([kn_ref], [vn_ref], tk <= tq)
        o_ref[0] = acc_ref[...] / l_ref[...]


def moba_sample(q, cache_k, cache_v, page_table, sel, k_new, v_new, t_new, ppb, cfg):
    db, rows, hd = q.shape
    kvh = cfg.n_kv_heads
    n_full = sel.shape[1]
    page_rows = cache_k.shape[1]
    body = functools.partial(_moba_sample_body, ppb=ppb, scale=hd ** -0.5, t_new=t_new, kvh=kvh, rpk=rows // kvh)
    page_specs = [
        pl.BlockSpec((1, page_rows, hd), functools.partial(lambda p, b, n, pt: (pt[b, n * ppb + p], 0, 0), p))
        for p in range(ppb)]
    new_spec = pl.BlockSpec((1, page_rows, hd), lambda b, n, pt: (b, 0, 0))
    return pl.pallas_call(
        body,
        grid_spec=pltpu.PrefetchScalarGridSpec(
            num_scalar_prefetch=1,
            grid=(db, n_full),
            in_specs=([pl.BlockSpec((1, rows, hd), lambda b, n, pt: (b, 0, 0))] + page_specs + page_specs
                      + [pl.BlockSpec((1, 1, rows, 1), lambda b, n, pt: (b, n, 0, 0)), new_spec, new_spec]),
            out_specs=pl.BlockSpec((1, rows, hd), lambda b, n, pt: (b, 0, 0)),
            scratch_shapes=[
                pltpu.VMEM((rows, 1), F32),
                pltpu.VMEM((rows, 1), F32),
                pltpu.VMEM((rows, hd), F32),
            ],
        ),
        out_shape=jax.ShapeDtypeStruct((db, rows, hd), F32),
        compiler_params=_params("parallel", "arbitrary"),
        name="moba_sample",
    )(page_table, q, *([cache_k] * ppb), *([cache_v] * ppb), sel, k_new, v_new)


def _cast_body(x_ref, o_ref):
    o_ref[...] = x_ref[0].astype(o_ref.dtype)


def cast_layer(w, layer, cols=None):
    _, r, c = w.shape
    cols = c if cols is None else cols
    tr = next(t for t in (512, 256, 128, 64, 32, 16) if r % t == 0 and t * cols * 4 <= 8 * 1024 * 1024)
    return pl.pallas_call(
        _cast_body,
        grid=(r // tr,),
        in_specs=[pl.BlockSpec((1, tr, cols), lambda i: (layer, i, 0))],
        out_specs=pl.BlockSpec((tr, cols), lambda i: (i, 0)),
        out_shape=jax.ShapeDtypeStruct((r, cols), BF16),
        compiler_params=_params("parallel"),
        name="cast_bf16",
    )(w)


class Weights(NamedTuple):
    norm_mix: jax.Array
    norm_ffn: jax.Array
    w_in: jax.Array
    w_z: jax.Array
    w_gate2: jax.Array
    b_gate: jax.Array
    gla_norm: jax.Array
    w_out: jax.Array
    norm_kv: jax.Array
    w_kv: jax.Array
    k_norm: jax.Array
    w_q: jax.Array
    q_norm: jax.Array
    w_o: jax.Array
    peer_w_query: tuple
    peer_keys: tuple
    peer_u: tuple
    peer_v: tuple


def _prep_weights(norm_mix, norm_ffn, gla_w_in, gla_w_gate2, gla_b_gate, gla_norm, gla_w_out, norm_kv, w_kv,
                  k_norm, w_q, q_norm, w_o, peer_w_query, peer_sub_keys, peer_u, peer_v, cfg):
    main = 2 * cfg.gla_dk_tot + 2 * cfg.d_model
    rank = cfg.gla_gate_rank
    depth = norm_mix.shape[0]
    return Weights(
        norm_mix=norm_mix,
        norm_ffn=norm_ffn,
        w_in=gla_w_in[0][:, :main].astype(BF16),
        w_z=jnp.pad(gla_w_in[0][:, main:], ((0, 0), (0, LANES - rank))).astype(BF16),
        w_gate2=jnp.pad(gla_w_gate2[0], ((0, LANES - rank), (0, 0))).astype(BF16),
        b_gate=gla_b_gate[0][None, :],
        gla_norm=gla_norm[0].reshape(1, -1),
        w_out=gla_w_out[0].astype(BF16),
        norm_kv=norm_kv,
        w_kv=w_kv.astype(BF16),
        k_norm=k_norm[None, :],
        w_q=w_q[0].astype(BF16),
        q_norm=q_norm[0][None, :],
        w_o=w_o[0].astype(BF16),
        peer_w_query=tuple(peer_w_query[l].astype(BF16) for l in range(depth)),
        peer_keys=tuple(peer_sub_keys[l].reshape(2 * cfg.peer_heads, cfg.peer_n_keys, -1).astype(BF16)
                        for l in range(depth)),
        peer_u=tuple(cast_layer(peer_u, l) for l in range(depth)),
        peer_v=tuple(cast_layer(peer_v, l) for l in range(depth)),
    )


def _trunk(x, pos0, s0, attend, w, cfg):
    bsz, seq, d = x.shape
    n = bsz * seq
    hd = cfg.head_dim
    xf = x.reshape(n, d)

    (hn,) = rmsnorm_cast(xf, w.norm_mix[0:1])
    proj = matmul(hn, w.w_in, name="gla_in")
    z = matmul(hn, w.w_z, name="gla_gate_in")
    og, s_fin = gla(proj.reshape(bsz, seq, -1), z.reshape(bsz, seq, -1), w.w_gate2, w.b_gate, w.gla_norm, s0, cfg)
    h = matmul(og.reshape(n, -1), w.w_out, residual=xf, name="gla_out")
    h = peer_block(h, w.norm_ffn[0], w.peer_w_query[0], w.peer_keys[0], w.peer_u[0], w.peer_v[0], cfg)

    kvn, hn1 = rmsnorm_cast(h, jnp.stack([w.norm_kv, w.norm_mix[1]]))
    kv = matmul(kvn, w.w_kv, name="kv_proj")
    cos, sin = _rope_tables(pos0, seq, hd)
    (k,) = head_norm_rope(kv[:, :cfg.kv_dim], w.k_norm, cos, sin, seq, (F32,), hd)
    v = kv[:, cfg.kv_dim:]
    (q,) = head_norm_rope(matmul(hn1, w.w_q, name="q_proj"), w.q_norm, cos, sin, seq, (BF16,), hd)

    o = attend(q, k, v)
    h = matmul(o, w.w_o, residual=h, name="attn_out")
    h = peer_block(h, w.norm_ffn[1], w.peer_w_query[1], w.peer_keys[1], w.peer_u[1], w.peer_v[1], cfg)
    return (h.reshape(bsz, seq, d), s_fin[None],
            k.reshape(bsz, seq, cfg.n_kv_heads, hd), v.reshape(bsz, seq, cfg.n_kv_heads, hd))


def _attend_prompt(bsz, seq, cfg):
    def attend(q, k, v):
        n = q.shape[0]
        o = moba_prompt(q.reshape(bsz, seq, -1), k.reshape(bsz, seq, -1), v.reshape(bsz, seq, -1), cfg)
        return o.reshape(n, -1)
    return attend


def _attend_sample(db, t_new, cache_k, cache_v, page_table, past_len, cfg):
    hd, kvh, group, nh = cfg.head_dim, cfg.n_kv_heads, cfg.group, cfg.n_heads
    blk, ps = cfg.moba_block, cfg.page_size
    ppb = blk // ps
    n_full = past_len // blk
    assert past_len % blk == 0 and n_full > 0 and t_new <= ps
    pt = page_table[:, :n_full * ppb]
    ck = cache_k.reshape(cache_k.shape[0], ps * kvh, hd)
    cv = cache_v.reshape(cache_v.shape[0], ps * kvh, hd)

    def attend(q, k, v):
        qr = q.reshape(db, t_new, nh, hd).transpose(0, 2, 1, 3).reshape(db, nh * t_new, hd)
        means = block_means(cache_k, pt, n_full, ppb, cfg)
        sel = moba_select(qr.reshape(db, kvh, group * t_new, hd), means, cfg)
        sel = sel.transpose(0, 2, 1)[..., None]
        pad = ((0, 0), (0, ps - t_new), (0, 0), (0, 0))
        k_new = jnp.pad(k.reshape(db, t_new, kvh, hd), pad).reshape(db, ps * kvh, hd)
        v_new = jnp.pad(v.reshape(db, t_new, kvh, hd), pad).reshape(db, ps * kvh, hd)
        o = moba_sample(qr, ck, cv, pt, sel, k_new, v_new, t_new, ppb, cfg)
        o = o.reshape(db, nh, t_new, hd).transpose(0, 2, 1, 3).reshape(db * t_new, nh * hd)
        return o.astype(BF16)
    return attend


def _forward(x_prompt, x_sample, cache_k, cache_v, state_gla, page_table, weights, past_len, cfg):
    w = _prep_weights(*weights, cfg)
    bsz, seq, _ = x_prompt.shape
    db, t_new, _ = x_sample.shape
    s0_p = jnp.zeros((bsz, cfg.gla_heads, cfg.gla_dk, cfg.gla_dv), state_gla.dtype)
    y_p, st_p, k_p, v_p = _trunk(x_prompt, 0, s0_p, _attend_prompt(bsz, seq, cfg), w, cfg)
    attend_s = _attend_sample(db, t_new, cache_k, cache_v, page_table, past_len, cfg)
    y_s, st_s, k_s, v_s = _trunk(x_sample, past_len, state_gla[0], attend_s, w, cfg)
    return (y_p, y_s, st_p, st_s, k_p, v_p, k_s, v_s)


def kernel(x_prompt, x_sample, cache_k, cache_v, state_gla, page_table, norm_mix, norm_ffn, gla_w_in, gla_w_gate2, gla_b_gate, gla_norm, gla_w_out, norm_kv, w_kv, k_norm, w_q, q_norm, w_o, peer_w_query, peer_sub_keys, peer_u, peer_v):
    weights = (norm_mix, norm_ffn, gla_w_in, gla_w_gate2, gla_b_gate, gla_norm, gla_w_out, norm_kv, w_kv, k_norm,
               w_q, q_norm, w_o, peer_w_query, peer_sub_keys, peer_u, peer_v)
    past_len = page_table.shape[1] * CFG.page_size
    return _forward(x_prompt, x_sample, cache_k, cache_v, state_gla, page_table, weights, past_len, CFG)
```

```python
import functools
import math
from typing import NamedTuple

import numpy as np
import jax
import jax.numpy as jnp
from jax import lax
from jax.experimental import pallas as pl
from jax.experimental.pallas import tpu as pltpu

F32 = jnp.float32
BF16 = jnp.bfloat16

LANES = 128
MXU_COLS = 256
RMS_EPS = 1e-6
NEG_INF = -1e30
ROPE_THETA = 10000.0
VMEM_LIMIT_BYTES = 56 * 1024 * 1024


class Cfg(NamedTuple):
    d_model: int = 4096
    gla_heads: int = 4
    gla_gate_rank: int = 16
    gla_gate_tau: float = 16.0
    gla_chunk: int = 64
    head_dim: int = 128
    n_kv_heads: int = 8
    moba_block: int = 256
    moba_topk: int = 3
    page_size: int = 128
    peer_heads: int = 8
    peer_n_keys: int = 128
    peer_topk: int = 16
    peer_key_dim: int = 256

    @property
    def gla_dk_tot(self):
        return self.d_model // 2

    @property
    def gla_dk(self):
        return self.gla_dk_tot // self.gla_heads

    @property
    def gla_dv(self):
        return self.d_model // self.gla_heads

    @property
    def n_heads(self):
        return self.d_model // self.head_dim

    @property
    def group(self):
        return self.n_heads // self.n_kv_heads

    @property
    def kv_dim(self):
        return self.n_kv_heads * self.head_dim


CFG = Cfg()


def _params(*sem, flags=None):
    return pltpu.CompilerParams(dimension_semantics=sem, vmem_limit_bytes=VMEM_LIMIT_BYTES, flags=flags)


def _pick(n, prefs):
    for p in prefs:
        if n % p == 0:
            return p
    return n


def _rmsnorm_body(x_ref, g_ref, *o_refs, transposed):
    x = x_ref[...]
    y = x * lax.rsqrt(jnp.mean(x * x, axis=-1, keepdims=True) + RMS_EPS)
    ng = g_ref.shape[0]
    for i in range(ng):
        o_refs[i][...] = (y * g_ref[i:i + 1, :]).astype(o_refs[i].dtype)
    if transposed:
        o_refs[ng][...] = (y * g_ref[0:1, :]).T.astype(o_refs[ng].dtype)


def rmsnorm_cast(x, gains, transposed=False):
    m, d = x.shape
    g = gains.shape[0]
    tm = _pick(m, (256, 128, 64))
    out_specs = [pl.BlockSpec((tm, d), lambda i: (i, 0))] * g
    out_shape = [jax.ShapeDtypeStruct((m, d), BF16)] * g
    if transposed:
        out_specs = out_specs + [pl.BlockSpec((d, tm), lambda i: (0, i))]
        out_shape = out_shape + [jax.ShapeDtypeStruct((d, m), BF16)]
    return pl.pallas_call(
        functools.partial(_rmsnorm_body, transposed=transposed),
        grid=(m // tm,),
        in_specs=[pl.BlockSpec((tm, d), lambda i: (i, 0)), pl.BlockSpec((g, d), lambda i: (0, 0))],
        out_specs=out_specs,
        out_shape=out_shape,
        compiler_params=_params("parallel"),
        name="rmsnorm_cast",
    )(x, gains)


def _mm_body(x_ref, w_ref, o_ref):
    o_ref[...] = jnp.dot(x_ref[...], w_ref[...], preferred_element_type=F32).astype(o_ref.dtype)


def _mm_res_body(x_ref, w_ref, r_ref, o_ref):
    o_ref[...] = (r_ref[...] + jnp.dot(x_ref[...], w_ref[...], preferred_element_type=F32)).astype(o_ref.dtype)


def matmul(x, w, residual=None, out_dtype=F32, name="matmul"):
    m, k = x.shape
    n = w.shape[1]
    tm = _pick(m, (1024, 512, 256, 128, 64))
    tn = _pick(n, (1024, 512, 256, 128))
    in_specs = [pl.BlockSpec((tm, k), lambda j, i: (i, 0)), pl.BlockSpec((k, tn), lambda j, i: (0, j))]
    args = [x, w]
    body = _mm_body
    if residual is not None:
        in_specs.append(pl.BlockSpec((tm, tn), lambda j, i: (i, j)))
        args.append(residual)
        body = _mm_res_body
    return pl.pallas_call(
        body,
        grid=(n // tn, m // tm),
        in_specs=in_specs,
        out_specs=pl.BlockSpec((tm, tn), lambda j, i: (i, j)),
        out_shape=jax.ShapeDtypeStruct((m, n), out_dtype),
        compiler_params=_params("parallel", "parallel"),
        name=name,
    )(*args)


def _cumsum_rows(g):
    c = g.shape[0]
    row = lax.broadcasted_iota(jnp.int32, g.shape, 0)
    b = g
    s = 1
    while s < c:
        b = b + jnp.where(row >= s, pltpu.roll(b, s, axis=0), 0.0)
        s *= 2
    return b


def _log_sigmoid(x):
    return -(jnp.maximum(-x, 0.0) + jnp.log1p(jnp.exp(-jnp.abs(x))))


def _gla_body(q_ref, k_ref, v_ref, r_ref, z_ref, wg_ref, bg_ref, gn_ref, s0_ref, og_ref, sf_ref, st_ref,
              *, chunk, nsub, q_scale, inv_tau):
    t = pl.program_id(2)

    @pl.when(t == 0)
    def _():
        st_ref[...] = s0_ref[0, 0].T

    za = jnp.dot(z_ref[0].astype(BF16), wg_ref[...], preferred_element_type=F32) + bg_ref[...]
    log_a = _log_sigmoid(za) * inv_tau
    tril = (lax.broadcasted_iota(jnp.int32, (chunk, chunk), 0) >= lax.broadcasted_iota(jnp.int32, (chunk, chunk), 1))
    nt = (((1,), (1,)), ((), ()))
    tn = (((0,), (0,)), ((), ()))
    for i in range(nsub):
        sl = slice(i * chunk, (i + 1) * chunk)
        b = _cumsum_rows(log_a[sl])
        b_last = b[chunk - 1:chunk, :]
        q = q_ref[0, sl, :] * q_scale
        k = k_ref[0, sl, :]
        v = v_ref[0, sl, :].astype(BF16)
        qe = (q * jnp.exp(b)).astype(BF16)
        ke = (k * jnp.exp(-b)).astype(BF16)
        kd = (k * jnp.exp(b_last - b)).astype(BF16)
        att = lax.dot_general(qe, ke, nt, preferred_element_type=F32)
        att = jnp.where(tril, att, 0.0).astype(BF16)
        st = st_ref[...]
        o = lax.dot_general(qe, st.astype(BF16), nt, preferred_element_type=F32)
        o = o + jnp.dot(att, v, preferred_element_type=F32)
        st_ref[...] = st * jnp.exp(b_last) + lax.dot_general(v, kd, tn, preferred_element_type=F32)
        on = o * lax.rsqrt(jnp.mean(o * o, axis=-1, keepdims=True) + RMS_EPS) * gn_ref[...]
        r = r_ref[0, sl, :]
        og_ref[0, sl, :] = (on * (r * jax.nn.sigmoid(r))).astype(og_ref.dtype)

    @pl.when(t == pl.num_programs(2) - 1)
    def _():
        sf_ref[0, 0] = st_ref[...].T


def gla(proj, z, wg, bg, gn, s0, cfg):
    bsz, seq, _ = proj.shape
    h, dk, dv = cfg.gla_heads, cfg.gla_dk, cfg.gla_dv
    chunk = math.gcd(seq, cfg.gla_chunk)
    rows = _pick(seq, (8 * chunk, 4 * chunk, 2 * chunk, chunk))
    nsub = rows // chunk
    kq, kk, kv, kr = 0, h, (2 * h * dk) // dv, (2 * h * dk) // dv + h
    body = functools.partial(_gla_body, chunk=chunk, nsub=nsub, q_scale=dk ** -0.5, inv_tau=1.0 / cfg.gla_gate_tau)
    return pl.pallas_call(
        body,
        grid=(bsz, h, seq // rows),
        in_specs=[
            pl.BlockSpec((1, rows, dk), lambda b, hh, t: (b, t, kq + hh)),
            pl.BlockSpec((1, rows, dk), lambda b, hh, t: (b, t, kk + hh)),
            pl.BlockSpec((1, rows, dv), lambda b, hh, t: (b, t, kv + hh)),
            pl.BlockSpec((1, rows, dv), lambda b, hh, t: (b, t, kr + hh)),
            pl.BlockSpec((1, rows, LANES), lambda b, hh, t: (b, t, 0)),
            pl.BlockSpec((LANES, dk), lambda b, hh, t: (0, hh)),
            pl.BlockSpec((1, dk), lambda b, hh, t: (0, hh)),
            pl.BlockSpec((1, dv), lambda b, hh, t: (0, hh)),
            pl.BlockSpec((1, 1, dk, dv), lambda b, hh, t: (b, hh, 0, 0)),
        ],
        out_specs=[
            pl.BlockSpec((1, rows, dv), lambda b, hh, t: (b, t, hh)),
            pl.BlockSpec((1, 1, dk, dv), lambda b, hh, t: (b, hh, 0, 0)),
        ],
        out_shape=[
            jax.ShapeDtypeStruct((bsz, seq, h * dv), BF16),
            jax.ShapeDtypeStruct((bsz, h, dk, dv), F32),
        ],
        scratch_shapes=[pltpu.VMEM((dv, dk), F32)],
        compiler_params=_params("parallel", "parallel", "arbitrary"),
        name="gla",
    )(proj, proj, proj, proj, z, wg, bg, gn, s0)


def _headrope_body(x_ref, g_ref, cos_ref, sin_ref, *o_refs, nh, hd):
    for hh in range(nh):
        x = x_ref[:, hh * hd:(hh + 1) * hd]
        y = x * lax.rsqrt(jnp.mean(x * x, axis=-1, keepdims=True) + RMS_EPS) * g_ref[...]
        out = y * cos_ref[...] + pltpu.roll(y, hd // 2, axis=1) * sin_ref[...]
        for o_ref in o_refs:
            o_ref[:, hh * hd:(hh + 1) * hd] = out.astype(o_ref.dtype)


def head_norm_rope(x, gain, cos, sin, seq, out_dtypes, hd):
    n, width = x.shape
    nh = width // hd
    tm = _pick(seq, (256, 128, 64, 32, 16, 8))
    per = seq // tm
    return pl.pallas_call(
        functools.partial(_headrope_body, nh=nh, hd=hd),
        grid=(n // tm,),
        in_specs=[
            pl.BlockSpec((tm, width), lambda i: (i, 0)),
            pl.BlockSpec((1, hd), lambda i: (0, 0)),
            pl.BlockSpec((tm, hd), lambda i: (i % per, 0)),
            pl.BlockSpec((tm, hd), lambda i: (i % per, 0)),
        ],
        out_specs=[pl.BlockSpec((tm, width), lambda i: (i, 0)) for _ in out_dtypes],
        out_shape=[jax.ShapeDtypeStruct((n, width), dt) for dt in out_dtypes],
        compiler_params=_params("parallel"),
        name="head_norm_rope",
    )(x, gain, cos, sin)


def _rope_tables(pos0, seq, hd):
    half = hd // 2
    inv = ROPE_THETA ** (-np.arange(half, dtype=np.float64) / half)
    ang = (pos0 + np.arange(seq, dtype=np.float64))[:, None] * inv[None, :]
    cos, sin = np.cos(ang), np.sin(ang)
    return (jnp.asarray(np.concatenate([cos, cos], axis=1), F32),
            jnp.asarray(np.concatenate([-sin, sin], axis=1), F32))


def _top_values(s, k):
    vals = []
    cur = s
    for i in range(k):
        m = jnp.max(cur, axis=0, keepdims=True)
        vals.append(m)
        if i + 1 < k:
            cur = jnp.where(cur == m, NEG_INF, cur)
    return vals


ROW_TAU, ROW_M1, ROW_M2, ROW_INVZ = range(4)


def _peer_route_body(q_ref, keys_ref, s1_ref, s2_ref, rows_ref, *, heads, nkeys, half, topk):
    nt = (((1,), (1,)), ((), ()))
    tm = q_ref.shape[0]
    sub = 8
    row = lax.broadcasted_iota(jnp.int32, (sub, tm), 0)
    for hh in range(heads):
        st = []
        for p in range(2):
            g = 2 * hh + p
            qg = q_ref[:, g * half:(g + 1) * half].astype(BF16)
            st.append(lax.dot_general(keys_ref[g], qg, nt, preferred_element_type=F32))
        s1, s2 = st
        v1 = _top_values(s1, topk)
        v2 = _top_values(s2, topk)
        v2g = []
        for g0 in range(0, topk, sub):
            grp = jnp.full((sub, tm), NEG_INF, F32)
            for b in range(g0, min(g0 + sub, topk)):
                grp = jnp.where(row == b - g0, v2[b], grp)
            v2g.append(grp)
        cands = []
        for a in range(topk):
            bmax = topk // (a + 1)
            for gi, grp in enumerate(v2g):
                if gi * sub < bmax:
                    cands.append(jnp.where(row < bmax - gi * sub, v1[a] + grp, NEG_INF))
        cur = cands
        tau = None
        for i in range(topk):
            tau = functools.reduce(jnp.maximum, [jnp.max(c, axis=0, keepdims=True) for c in cur])
            if i + 1 < topk:
                cur = [jnp.where(c == tau, NEG_INF, c) for c in cur]
        m1, m2 = v1[0], v2[0]
        mx = m1 + m2
        z = functools.reduce(
            jnp.add, [jnp.sum(jnp.where(c >= tau, jnp.exp(c - mx), 0.0), axis=0, keepdims=True) for c in cands])
        s1_ref[hh] = s1
        s2_ref[hh] = s2
        rows_ref[ROW_TAU, hh:hh + 1, :] = tau
        rows_ref[ROW_M1, hh:hh + 1, :] = m1
        rows_ref[ROW_M2, hh:hh + 1, :] = m2
        rows_ref[ROW_INVZ, hh:hh + 1, :] = 1.0 / z


def peer_route(q, keys, cfg):
    n = q.shape[0]
    heads, nkeys, half = cfg.peer_heads, cfg.peer_n_keys, cfg.peer_key_dim // 2
    tm = _pick(n, (512, 256, 128))
    tab = jax.ShapeDtypeStruct((heads, nkeys, n), F32)
    tab_spec = pl.BlockSpec((heads, nkeys, tm), lambda i: (0, 0, i))
    return pl.pallas_call(
        functools.partial(_peer_route_body, heads=heads, nkeys=nkeys, half=half, topk=cfg.peer_topk),
        grid=(n // tm,),
        in_specs=[pl.BlockSpec((tm, q.shape[1]), lambda i: (i, 0)),
                  pl.BlockSpec(keys.shape, lambda i: (0, 0, 0))],
        out_specs=[tab_spec, tab_spec, pl.BlockSpec((4, heads, tm), lambda i: (0, 0, i))],
        out_shape=[tab, tab, jax.ShapeDtypeStruct((4, heads, n), F32)],
        compiler_params=_params("parallel"),
        name="peer_route",
    )(q, keys)


def _gelu(x):
    return 0.5 * x * (1.0 + lax.erf(x * (1.0 / math.sqrt(2.0))))


SUBLANES = 8
GATE_LANES = 512


def _peer_row_tables(bc_ref, jt, s1_ref, rows_ref, *, heads, nsub):
    tm = bc_ref.shape[-1]
    for r in range(nsub):
        i1 = jt * nsub + r
        for hh in range(heads):
            s1row = s1_ref[hh, pl.ds(i1, 1), :]
            crow = jnp.exp(s1row - rows_ref[ROW_M1, hh:hh + 1, :]) * rows_ref[ROW_INVZ, hh:hh + 1, :]
            bc_ref[hh, r, 0] = jnp.broadcast_to(s1row, (SUBLANES, tm))
            bc_ref[hh, r, 1] = jnp.broadcast_to(crow, (SUBLANES, tm))


def _peer_gate_passes(dst_ref, s2_ref, e2_ref, taub_ref, bc_ref, *, heads, nkeys, nsub):
    tm = dst_ref.shape[-1]
    tw = min(tm, GATE_LANES)

    def one(c, l):
        rows, cols = slice(c, c + SUBLANES), slice(l, l + tw)
        ws = [None] * nsub
        for hh in range(heads):
            s2t, e2t, tb = s2_ref[hh, rows, cols], e2_ref[hh, rows, cols], taub_ref[hh, :, cols]
            for r in range(nsub):
                hit = (s2t + bc_ref[hh, r, 0, :, cols]) >= tb
                term = jnp.where(hit, e2t, 0.0) * bc_ref[hh, r, 1, :, cols]
                ws[r] = term if ws[r] is None else ws[r] + term
        for r in range(nsub):
            dst_ref[r * nkeys + c:r * nkeys + c + SUBLANES, cols] = ws[r]

    return [functools.partial(one, c, l) for c in range(0, nkeys, SUBLANES) for l in range(0, tm, tw)]


def _peer_dense_body(xt_ref, u_ref, v_ref, s1_ref, s2_ref, rows_ref, o_ref, e2_ref, taub_ref, bc_ref,
                     gate_a_ref, gate_b_ref, ht_ref, *, heads, nkeys, nsub):
    j = pl.program_id(1)
    last = pl.num_programs(1) - 1
    row_tables = functools.partial(_peer_row_tables, bc_ref, s1_ref=s1_ref, rows_ref=rows_ref, heads=heads, nsub=nsub)
    gate_passes = functools.partial(_peer_gate_passes, s2_ref=s2_ref, e2_ref=e2_ref, taub_ref=taub_ref, bc_ref=bc_ref,
                                    heads=heads, nkeys=nkeys, nsub=nsub)
    tm = xt_ref.shape[1]
    d = o_ref.shape[1]

    @pl.when(j == 0)
    def _():
        o_ref[...] = jnp.zeros_like(o_ref)
        for hh in range(heads):
            e2_ref[hh] = jnp.exp(s2_ref[hh] - rows_ref[ROW_M2, hh:hh + 1, :])
            taub_ref[hh] = jnp.broadcast_to(rows_ref[ROW_TAU, hh:hh + 1, :], (SUBLANES, tm))
        row_tables(0)
        for run in gate_passes(gate_a_ref):
            run()

    def step(cur_ref, nxt_ref):
        row_tables(jnp.minimum(j + 1, last))
        passes = gate_passes(nxt_ref)
        tw = min(tm, MXU_COLS)
        n_tok, n_out = tm // tw, d // MXU_COLS
        n_first = n_tok
        first = len(passes) // 2
        quota = ([first // n_first + (i < first % n_first) for i in range(n_first)]
                 + [(len(passes) - first) // n_out + (i < (len(passes) - first) % n_out) for i in range(n_out)])
        it = iter(passes)
        for c in range(n_tok):
            cols = slice(c * tw, (c + 1) * tw)
            act = _gelu(jnp.dot(u_ref[...], xt_ref[:, cols], preferred_element_type=F32))
            ht_ref[cols, :] = (cur_ref[:, cols] * act).T.astype(BF16)
            for _ in range(quota[c]):
                next(it)()
        for p in range(n_out):
            cols = slice(p * MXU_COLS, (p + 1) * MXU_COLS)
            o_ref[:, cols] += jnp.dot(ht_ref[...], v_ref[:, cols], preferred_element_type=F32)
            for _ in range(quota[n_first + p]):
                next(it)()

    @pl.when(j % 2 == 0)
    def _():
        step(gate_a_ref, gate_b_ref)

    @pl.when(j % 2 == 1)
    def _():
        step(gate_b_ref, gate_a_ref)


def peer_dense(xt, u, v, s1, s2, rows, cfg):
    d, n = xt.shape
    e = u.shape[0]
    heads, nkeys = cfg.peer_heads, cfg.peer_n_keys
    tm = _pick(n, (1024, 512, 256, 128))
    nsub = 2
    te = nsub * nkeys
    once = dict(pipeline_mode=pl.Buffered(1))
    tab_spec = pl.BlockSpec((heads, nkeys, tm), lambda i, j: (0, 0, i), **once)
    return pl.pallas_call(
        functools.partial(_peer_dense_body, heads=heads, nkeys=nkeys, nsub=nsub),
        grid=(n // tm, e // te),
        in_specs=[
            pl.BlockSpec((d, tm), lambda i, j: (0, i), **once),
            pl.BlockSpec((te, d), lambda i, j: (j, 0)),
            pl.BlockSpec((te, d), lambda i, j: (j, 0)),
            tab_spec, tab_spec,
            pl.BlockSpec((4, heads, tm), lambda i, j: (0, 0, i), **once),
        ],
        out_specs=pl.BlockSpec((tm, d), lambda i, j: (i, 0), **once),
        out_shape=jax.ShapeDtypeStruct((n, d), F32),
        scratch_shapes=[
            pltpu.VMEM((heads, nkeys, tm), F32),
            pltpu.VMEM((heads, SUBLANES, tm), F32),
            pltpu.VMEM((heads, nsub, 2, SUBLANES, tm), F32),
            pltpu.VMEM((te, tm), F32),
            pltpu.VMEM((te, tm), F32),
            pltpu.VMEM((tm, te), BF16),
        ],
        compiler_params=_params("parallel", "arbitrary"),
        name="peer_dense",
    )(xt, u, v, s1, s2, rows)


def peer_block(h, g_ffn, w_query, keys, u, v, cfg):
    n = h.shape[0]
    npad = -(-n // LANES) * LANES
    hp = h if npad == n else jnp.pad(h, ((0, npad - n), (0, 0)))
    xn, xt = rmsnorm_cast(hp, g_ffn[None, :], transposed=True)
    q = matmul(xn, w_query, name="peer_query")
    s1, s2, rows = peer_route(q, keys, cfg)
    return h + peer_dense(xt, u, v, s1, s2, rows, cfg)[:n]


def _moba_prompt_body(q_ref, k_ref, v_ref, o_ref, means_ref, sel_ref, m_ref, l_ref, acc_ref,
                      *, blk, nblk, group, hd, topk, scale):
    qb = pl.program_id(2)
    nt = (((1,), (1,)), ((), ()))
    tn = (((0,), (0,)), ((), ()))
    rows = group * blk

    @pl.when(qb == 0)
    def _():
        for n in range(nblk):
            means_ref[n:n + 1, :] = jnp.mean(k_ref[0, n * blk:(n + 1) * blk, :], axis=0, keepdims=True)

    q4 = jnp.concatenate([q_ref[0, :, g * hd:(g + 1) * hd] for g in range(group)], axis=0)

    gate = lax.dot_general(means_ref[...].astype(BF16), q4, nt, preferred_element_type=F32)
    bidx = lax.broadcasted_iota(jnp.int32, gate.shape, 0)
    cand = bidx < qb
    gate = jnp.where(cand, gate, NEG_INF)
    rank = jnp.zeros(gate.shape, F32)
    for mm in range(nblk):
        gm = gate[mm:mm + 1, :]
        beats = (gm > gate) | ((gm == gate) & (mm < bidx))
        rank = rank + jnp.where(beats, 1.0, 0.0)
    sel_ref[...] = jnp.where((rank < topk) & cand, 1.0, 0.0)

    pw = min(rows, 2 * blk)
    parts = [slice(i, i + pw) for i in range(0, rows, pw)]
    qs = [q4[c, :] for c in parts]

    kpos = lax.broadcasted_iota(jnp.int32, (blk, pw), 0)
    qpos = lax.broadcasted_iota(jnp.int32, (blk, pw), 1) % blk
    k_own = k_ref[0, pl.ds(qb * blk, blk), :].astype(BF16)
    v_own = v_ref[0, pl.ds(qb * blk, blk), :].astype(BF16)
    ss = [lax.dot_general(k_own, qp, nt, preferred_element_type=F32) * scale for qp in qs]
    for c, s in zip(parts, ss):
        s = jnp.where(kpos <= qpos, s, NEG_INF)
        m0 = jnp.max(s, axis=0, keepdims=True)
        p = jnp.exp(s - m0)
        m_ref[:, c] = m0
        l_ref[:, c] = jnp.sum(p, axis=0, keepdims=True)
        acc_ref[:, c] = lax.dot_general(v_own, p.astype(BF16), tn, preferred_element_type=F32)

    def past(n, carry):
        kb = k_ref[0, pl.ds(n * blk, blk), :].astype(BF16)
        vb = v_ref[0, pl.ds(n * blk, blk), :].astype(BF16)
        sbs = [lax.dot_general(kb, qp, nt, preferred_element_type=F32) * scale for qp in qs]
        for c, sb in zip(parts, sbs):
            on = sel_ref[pl.ds(n, 1), c] > 0.5
            m_old = m_ref[:, c]
            m_new = jnp.where(on, jnp.maximum(m_old, jnp.max(sb, axis=0, keepdims=True)), m_old)
            pb = jnp.exp(sb - m_new)
            alpha = jnp.exp(m_old - m_new)
            m_ref[:, c] = m_new
            l_ref[:, c] = alpha * l_ref[:, c] + jnp.where(on, jnp.sum(pb, axis=0, keepdims=True), 0.0)
            pv = lax.dot_general(vb, pb.astype(BF16), tn, preferred_element_type=F32)
            acc_ref[:, c] = alpha * acc_ref[:, c] + jnp.where(on, pv, 0.0)
        return carry

    lax.fori_loop(0, qb, past, 0)

    out = (acc_ref[...] / l_ref[...]).T
    for g in range(group):
        o_ref[0, :, g * hd:(g + 1) * hd] = out[g * blk:(g + 1) * blk, :].astype(o_ref.dtype)


def moba_prompt(q, k, v, cfg):
    bsz, seq, _ = q.shape
    blk, hd, group, kvh = cfg.moba_block, cfg.head_dim, cfg.group, cfg.n_kv_heads
    assert seq % blk == 0
    nblk = seq // blk
    rows = group * blk
    nsel = -(-nblk // 8) * 8
    body = functools.partial(_moba_prompt_body, blk=blk, nblk=nblk, group=group, hd=hd, topk=cfg.moba_topk,
                             scale=hd ** -0.5)
    return pl.pallas_call(
        body,
        grid=(bsz, kvh, nblk),
        in_specs=[
            pl.BlockSpec((1, blk, group * hd), lambda b, kh, i: (b, i, kh)),
            pl.BlockSpec((1, seq, hd), lambda b, kh, i: (b, 0, kh)),
            pl.BlockSpec((1, seq, hd), lambda b, kh, i: (b, 0, kh)),
        ],
        out_specs=pl.BlockSpec((1, blk, group * hd), lambda b, kh, i: (b, i, kh)),
        out_shape=jax.ShapeDtypeStruct(q.shape, BF16),
        scratch_shapes=[
            pltpu.VMEM((nblk, hd), F32),
            pltpu.VMEM((nblk, rows), F32),
            pltpu.VMEM((1, rows), F32),
            pltpu.VMEM((1, rows), F32),
            pltpu.VMEM((hd, rows), F32),
        ],
        compiler_params=_params("parallel", "parallel", "arbitrary"),
        name="moba_prompt",
    )(q, k, v)


def _block_means_body(pt_ref, *refs, inv_rows, ppb):
    k_refs, o_ref = refs[:-1], refs[-1]
    for i in range(len(k_refs) // ppb):
        pages = k_refs[i * ppb:(i + 1) * ppb]
        o_ref[0, i] = functools.reduce(jnp.add, [jnp.sum(k_ref[0], axis=0) for k_ref in pages]) * inv_rows


def _page_specs(ppb, ps, kvh, hd):
    return [pl.BlockSpec((1, ps, kvh, hd), functools.partial(lambda p, b, n, pt: (pt[b, n * ppb + p], 0, 0, 0), p))
            for p in range(ppb)]


def block_means(cache_k, page_table, n_full, ppb, cfg):
    db = page_table.shape[0]
    _, ps, kvh, hd = cache_k.shape
    bps = 2 if n_full % 2 == 0 else 1
    return pl.pallas_call(
        functools.partial(_block_means_body, inv_rows=1.0 / (ps * ppb), ppb=ppb),
        grid_spec=pltpu.PrefetchScalarGridSpec(
            num_scalar_prefetch=1,
            grid=(db, n_full // bps),
            in_specs=_page_specs(bps * ppb, ps, kvh, hd),
            out_specs=pl.BlockSpec((1, bps, kvh, hd), lambda b, n, pt: (b, n, 0, 0)),
        ),
        out_shape=jax.ShapeDtypeStruct((db, n_full, kvh, hd), F32),
        compiler_params=_params("parallel", "parallel"),
        name="block_means",
    )(page_table, *([cache_k] * (bps * ppb)))


def _moba_select_body(q_ref, means_ref, sel_ref, *, topk, kvh):
    nt = (((1,), (1,)), ((), ()))
    gate = jnp.concatenate(
        [lax.dot_general(q_ref[0, kh], means_ref[0, :, kh, :].astype(BF16), nt, preferred_element_type=F32)
         for kh in range(kvh)], axis=0)
    lane = lax.broadcasted_iota(jnp.int32, gate.shape, 1)
    nb = gate.shape[1]
    sel = jnp.zeros(gate.shape, F32)
    for _ in range(topk):
        m = jnp.max(gate, axis=1, keepdims=True)
        first = jnp.min(jnp.where(gate == m, lane, nb), axis=1, keepdims=True)
        pick = lane == first
        sel = jnp.where(pick, 1.0, sel)
        gate = jnp.where(pick, -3.0e38, gate)
    sel_ref[0] = sel


def moba_select(q4, means, cfg):
    db, kvh, rpk, hd = q4.shape
    n_full = means.shape[1]
    rows = kvh * rpk
    return pl.pallas_call(
        functools.partial(_moba_select_body, topk=min(cfg.moba_topk, n_full), kvh=kvh),
        grid=(db,),
        in_specs=[pl.BlockSpec((1, kvh, rpk, hd), lambda b: (b, 0, 0, 0)),
                  pl.BlockSpec((1, n_full, kvh, hd), lambda b: (b, 0, 0, 0))],
        out_specs=pl.BlockSpec((1, rows, n_full), lambda b: (b, 0, 0)),
        out_shape=jax.ShapeDtypeStruct((db, rows, n_full), F32),
        compiler_params=_params("parallel"),
        name="moba_select",
    )(q4, means)


def _moba_sample_body(pt_ref, *refs, ppb, scale, t_new, kvh, rpk):
    q_ref, sel_ref, kn_ref, vn_ref = refs[0], refs[1 + 2 * ppb], refs[2 + 2 * ppb], refs[3 + 2 * ppb]
    k_refs, v_refs = refs[1:1 + ppb], refs[1 + ppb:1 + 2 * ppb]
    o_ref, m_ref, l_ref, acc_ref = refs[4 + 2 * ppb:]
    n = pl.program_id(1)
    nt = (((1,), (1,)), ((), ()))
    rows = q_ref.shape[1]
    ps = kn_ref.shape[1] // kvh

    @pl.when(n == 0)
    def _():
        m_ref[...] = jnp.full(m_ref.shape, NEG_INF, F32)
        l_ref[...] = jnp.zeros(l_ref.shape, F32)
        acc_ref[...] = jnp.zeros(acc_ref.shape, F32)

    def head_rows(page_ref, kh):
        return page_ref[0, pl.ds(kh, ps, stride=kvh), :].astype(BF16)

    def absorb(k_pages, v_pages, on):
        s = jnp.concatenate(
            [jnp.concatenate(
                [lax.dot_general(q_ref[0, kh * rpk:(kh + 1) * rpk, :], head_rows(kp, kh), nt,
                                 preferred_element_type=F32) for kp in k_pages], axis=1)
             for kh in range(kvh)], axis=0) * scale
        s = jnp.where(on, s, NEG_INF)
        m_old = m_ref[...]
        m_new = jnp.maximum(m_old, jnp.max(s, axis=1, keepdims=True))
        p = jnp.where(on, jnp.exp(s - m_new), 0.0)
        alpha = jnp.exp(m_old - m_new)
        m_ref[...] = m_new
        l_ref[...] = alpha * l_ref[...] + jnp.sum(p, axis=1, keepdims=True)
        pb = p.astype(BF16)
        pv = jnp.concatenate(
            [functools.reduce(jnp.add, [
                jnp.dot(pb[kh * rpk:(kh + 1) * rpk, i * ps:(i + 1) * ps], head_rows(vp, kh),
                        preferred_element_type=F32) for i, vp in enumerate(v_pages)])
             for kh in range(kvh)], axis=0)
        acc_ref[...] = alpha * acc_ref[...] + pv

    absorb(k_refs, v_refs, sel_ref[0, 0] > 0.5)

    @pl.when(n == pl.num_programs(1) - 1)
    def _():
        tq = lax.broadcasted_iota(jnp.int32, (rows, ps), 0) % t_new
        tk = lax.broadcasted_iota(jnp.int32, (rows, ps), 1)
        absorb([kn_ref], [vn_ref], tk <= tq)
        o_ref[0] = acc_ref[...] / l_ref[...]


def moba_sample(q, cache_k, cache_v, page_table, sel, k_new, v_new, t_new, ppb, cfg):
    db, rows, hd = q.shape
    kvh = cfg.n_kv_heads
    n_full = sel.shape[1]
    page_rows = cache_k.shape[1]
    body = functools.partial(_moba_sample_body, ppb=ppb, scale=hd ** -0.5, t_new=t_new, kvh=kvh, rpk=rows // kvh)
    page_specs = [
        pl.BlockSpec((1, page_rows, hd), functools.partial(lambda p, b, n, pt: (pt[b, n * ppb + p], 0, 0), p))
        for p in range(ppb)]
    new_spec = pl.BlockSpec((1, page_rows, hd), lambda b, n, pt: (b, 0, 0))
    return pl.pallas_call(
        body,
        grid_spec=pltpu.PrefetchScalarGridSpec(
            num_scalar_prefetch=1,
            grid=(db, n_full),
            in_specs=([pl.BlockSpec((1, rows, hd), lambda b, n, pt: (b, 0, 0))] + page_specs + page_specs
                      + [pl.BlockSpec((1, 1, rows, 1), lambda b, n, pt: (b, n, 0, 0)), new_spec, new_spec]),
            out_specs=pl.BlockSpec((1, rows, hd), lambda b, n, pt: (b, 0, 0)),
            scratch_shapes=[
                pltpu.VMEM((rows, 1), F32),
                pltpu.VMEM((rows, 1), F32),
                pltpu.VMEM((rows, hd), F32),
            ],
        ),
        out_shape=jax.ShapeDtypeStruct((db, rows, hd), F32),
        compiler_params=_params("parallel", "arbitrary"),
        name="moba_sample",
    )(page_table, q, *([cache_k] * ppb), *([cache_v] * ppb), sel, k_new, v_new)


def _cast_body(x_ref, o_ref):
    o_ref[...] = x_ref[0].astype(o_ref.dtype)


def cast_layer(w, layer, cols=None):
    _, r, c = w.shape
    cols = c if cols is None else cols
    tr = next(t for t in (512, 256, 128, 64, 32, 16) if r % t == 0 and t * cols * 4 <= 8 * 1024 * 1024)
    return pl.pallas_call(
        _cast_body,
        grid=(r // tr,),
        in_specs=[pl.BlockSpec((1, tr, cols), lambda i: (layer, i, 0))],
        out_specs=pl.BlockSpec((tr, cols), lambda i: (i, 0)),
        out_shape=jax.ShapeDtypeStruct((r, cols), BF16),
        compiler_params=_params("parallel"),
        name="cast_bf16",
    )(w)


class Weights(NamedTuple):
    norm_mix: jax.Array
    norm_ffn: jax.Array
    w_in: jax.Array
    w_z: jax.Array
    w_gate2: jax.Array
    b_gate: jax.Array
    gla_norm: jax.Array
    w_out: jax.Array
    norm_kv: jax.Array
    w_kv: jax.Array
    k_norm: jax.Array
    w_q: jax.Array
    q_norm: jax.Array
    w_o: jax.Array
    peer_w_query: tuple
    peer_keys: tuple
    peer_u: tuple
    peer_v: tuple


def _prep_weights(norm_mix, norm_ffn, gla_w_in, gla_w_gate2, gla_b_gate, gla_norm, gla_w_out, norm_kv, w_kv,
                  k_norm, w_q, q_norm, w_o, peer_w_query, peer_sub_keys, peer_u, peer_v, cfg):
    main = 2 * cfg.gla_dk_tot + 2 * cfg.d_model
    rank = cfg.gla_gate_rank
    depth = norm_mix.shape[0]
    return Weights(
        norm_mix=norm_mix,
        norm_ffn=norm_ffn,
        w_in=gla_w_in[0][:, :main].astype(BF16),
        w_z=jnp.pad(gla_w_in[0][:, main:], ((0, 0), (0, LANES - rank))).astype(BF16),
        w_gate2=jnp.pad(gla_w_gate2[0], ((0, LANES - rank), (0, 0))).astype(BF16),
        b_gate=gla_b_gate[0][None, :],
        gla_norm=gla_norm[0].reshape(1, -1),
        w_out=gla_w_out[0].astype(BF16),
        norm_kv=norm_kv,
        w_kv=w_kv.astype(BF16),
        k_norm=k_norm[None, :],
        w_q=w_q[0].astype(BF16),
        q_norm=q_norm[0][None, :],
        w_o=w_o[0].astype(BF16),
        peer_w_query=tuple(peer_w_query[l].astype(BF16) for l in range(depth)),
        peer_keys=tuple(peer_sub_keys[l].reshape(2 * cfg.peer_heads, cfg.peer_n_keys, -1).astype(BF16)
                        for l in range(depth)),
        peer_u=tuple(cast_layer(peer_u, l) for l in range(depth)),
        peer_v=tuple(cast_layer(peer_v, l) for l in range(depth)),
    )


def _trunk(x, pos0, s0, attend, w, cfg):
    bsz, seq, d = x.shape
    n = bsz * seq
    hd = cfg.head_dim
    xf = x.reshape(n, d)

    (hn,) = rmsnorm_cast(xf, w.norm_mix[0:1])
    proj = matmul(hn, w.w_in, name="gla_in")
    z = matmul(hn, w.w_z, name="gla_gate_in")
    og, s_fin = gla(proj.reshape(bsz, seq, -1), z.reshape(bsz, seq, -1), w.w_gate2, w.b_gate, w.gla_norm, s0, cfg)
    h = matmul(og.reshape(n, -1), w.w_out, residual=xf, name="gla_out")
    h = peer_block(h, w.norm_ffn[0], w.peer_w_query[0], w.peer_keys[0], w.peer_u[0], w.peer_v[0], cfg)

    kvn, hn1 = rmsnorm_cast(h, jnp.stack([w.norm_kv, w.norm_mix[1]]))
    kv = matmul(kvn, w.w_kv, name="kv_proj")
    cos, sin = _rope_tables(pos0, seq, hd)
    (k,) = head_norm_rope(kv[:, :cfg.kv_dim], w.k_norm, cos, sin, seq, (F32,), hd)
    v = kv[:, cfg.kv_dim:]
    (q,) = head_norm_rope(matmul(hn1, w.w_q, name="q_proj"), w.q_norm, cos, sin, seq, (BF16,), hd)

    o = attend(q, k, v)
    h = matmul(o, w.w_o, residual=h, name="attn_out")
    h = peer_block(h, w.norm_ffn[1], w.peer_w_query[1], w.peer_keys[1], w.peer_u[1], w.peer_v[1], cfg)
    return (h.reshape(bsz, seq, d), s_fin[None],
            k.reshape(bsz, seq, cfg.n_kv_heads, hd), v.reshape(bsz, seq, cfg.n_kv_heads, hd))


def _attend_prompt(bsz, seq, cfg):
    def attend(q, k, v):
        n = q.shape[0]
        o = moba_prompt(q.reshape(bsz, seq, -1), k.reshape(bsz, seq, -1), v.reshape(bsz, seq, -1), cfg)
        return o.reshape(n, -1)
    return attend


def _attend_sample(db, t_new, cache_k, cache_v, page_table, past_len, cfg):
    hd, kvh, group, nh = cfg.head_dim, cfg.n_kv_heads, cfg.group, cfg.n_heads
    blk, ps = cfg.moba_block, cfg.page_size
    ppb = blk // ps
    n_full = past_len // blk
    assert past_len % blk == 0 and n_full > 0 and t_new <= ps
    pt = page_table[:, :n_full * ppb]
    ck = cache_k.reshape(cache_k.shape[0], ps * kvh, hd)
    cv = cache_v.reshape(cache_v.shape[0], ps * kvh, hd)

    def attend(q, k, v):
        qr = q.reshape(db, t_new, nh, hd).transpose(0, 2, 1, 3).reshape(db, nh * t_new, hd)
        means = block_means(cache_k, pt, n_full, ppb, cfg)
        sel = moba_select(qr.reshape(db, kvh, group * t_new, hd), means, cfg)
        sel = sel.transpose(0, 2, 1)[..., None]
        pad = ((0, 0), (0, ps - t_new), (0, 0), (0, 0))
        k_new = jnp.pad(k.reshape(db, t_new, kvh, hd), pad).reshape(db, ps * kvh, hd)
        v_new = jnp.pad(v.reshape(db, t_new, kvh, hd), pad).reshape(db, ps * kvh, hd)
        o = moba_sample(qr, ck, cv, pt, sel, k_new, v_new, t_new, ppb, cfg)
        o = o.reshape(db, nh, t_new, hd).transpose(0, 2, 1, 3).reshape(db * t_new, nh * hd)
        return o.astype(BF16)
    return attend


def _forward(x_prompt, x_sample, cache_k, cache_v, state_gla, page_table, weights, past_len, cfg):
    w = _prep_weights(*weights, cfg)
    bsz, seq, _ = x_prompt.shape
    db, t_new, _ = x_sample.shape
    s0_p = jnp.zeros((bsz, cfg.gla_heads, cfg.gla_dk, cfg.gla_dv), state_gla.dtype)
    y_p, st_p, k_p, v_p = _trunk(x_prompt, 0, s0_p, _attend_prompt(bsz, seq, cfg), w, cfg)
    attend_s = _attend_sample(db, t_new, cache_k, cache_v, page_table, past_len, cfg)
    y_s, st_s, k_s, v_s = _trunk(x_sample, past_len, state_gla[0], attend_s, w, cfg)
    return (y_p, y_s, st_p, st_s, k_p, v_p, k_s, v_s)


def kernel(x_prompt, x_sample, cache_k, cache_v, state_gla, page_table, norm_mix, norm_ffn, gla_w_in, gla_w_gate2, gla_b_gate, gla_norm, gla_w_out, norm_kv, w_kv, k_norm, w_q, q_norm, w_o, peer_w_query, peer_sub_keys, peer_u, peer_v):
    weights = (norm_mix, norm_ffn, gla_w_in, gla_w_gate2, gla_b_gate, gla_norm, gla_w_out, norm_kv, w_kv, k_norm,
               w_q, q_norm, w_o, peer_w_query, peer_sub_keys, peer_u, peer_v)
    past_len = page_table.shape[1] * CFG.page_size
    return _forward(x_prompt, x_sample, cache_k, cache_v, state_gla, page_table, weights, past_len, CFG)
```
